```python
import jax, jax.numpy as jnp
from jax import lax
import numpy as np

D_MODEL = 1024
BATCH = 8
SEQ = 4096
DEPTH = 4
DEC_BATCH = 4
DEC_SEQ = 4096
PAST_LEN = 128

CHUNK = 128
A_GROUPS = 8
A_WIDTH = D_MODEL
A_GROUP_DIM = A_WIDTH // A_GROUPS
B_PATTERNS = ((128, 1), (512, 4), (2048, 16))
B_HEADS = 4
B_HEAD_DIM = 128
B_GROUP_WIDTH = B_HEADS * B_HEAD_DIM
N_ATTN_HEADS = len(B_PATTERNS) * B_HEADS
ALIBI_SLOPES = np.array([2.0 ** (-8.0 * (h + 1) / N_ATTN_HEADS) for h in range(N_ATTN_HEADS)], dtype=np.float32).reshape(len(B_PATTERNS), B_HEADS)
IN_SIZES = (A_WIDTH, A_WIDTH) + (B_GROUP_WIDTH,) * (3 * len(B_PATTERNS)) + (D_MODEL, D_MODEL)
D_IN = sum(IN_SIZES)
N_EXPERTS = 32
TOP_K = 4
D_EXPERT = D_MODEL
SWIGLU_LIMIT = 7.0
SWIGLU_ALPHA = 1.702
ROUTE_BLOCK = 128
DN_ALPHA = (2.0 * DEPTH) ** 0.25
DN_BETA = (8.0 * DEPTH) ** -0.25
LN_EPS = 1e-5
NEG_INF = -1e30

kernel_name = 'hybrid_gmlp_dilated_moe_encoder'


def layer_norm(x, g, b):
    xf = x.astype(jnp.float32)
    xc = xf - xf.mean(-1, keepdims=True)
    var = jnp.mean(xc * xc, -1, keepdims=True)
    return (xc * lax.rsqrt(var + LN_EPS) * g.astype(jnp.float32) + b.astype(jnp.float32)).astype(x.dtype)


def spatial_gating(u, v, ln_g, ln_b, w_s, b_s):
    bsz, s, _ = v.shape
    vn = layer_norm(v, ln_g, ln_b).reshape(bsz, s // CHUNK, CHUNK, A_GROUPS, A_GROUP_DIM)
    mixed = jnp.einsum('gts,bnsgc->bntgc', w_s, vn) + b_s.T[None, None, :, :, None]
    return u * mixed.reshape(bsz, s, A_WIDTH)


def dilated_window_attention(q, k, v, dilation, n_side, slopes):
    bsz, s, h, hd = q.shape
    L = s // dilation
    nb = -(-L // n_side)
    Lp = nb * n_side

    def to_sub(t):
        return t.reshape(bsz, L, dilation, h, hd).transpose(0, 2, 1, 3, 4)

    qb = jnp.pad(to_sub(q), ((0, 0), (0, 0), (0, Lp - L), (0, 0), (0, 0))).reshape(bsz, dilation, nb, n_side, h, hd)
    pad_kv = ((0, 0), (0, 0), (n_side, Lp - L + n_side), (0, 0), (0, 0))
    kb = jnp.pad(to_sub(k), pad_kv).reshape(bsz, dilation, nb + 2, n_side, h, hd)
    vb = jnp.pad(to_sub(v), pad_kv).reshape(bsz, dilation, nb + 2, n_side, h, hd)

    def band(t):
        return jnp.concatenate([t[:, :, 0:nb], t[:, :, 1:nb + 1], t[:, :, 2:nb + 2]], axis=3)

    kw, vw = band(kb), band(vb)
    a_idx = np.arange(n_side)[:, None]
    c_idx = np.arange(3 * n_side)[None, :]
    rel = c_idx - n_side - a_idx
    q_sub = np.arange(nb)[:, None, None] * n_side + a_idx[None]
    k_sub = q_sub + rel[None]
    valid = (np.abs(rel)[None] <= n_side) & (k_sub >= 0) & (k_sub < L)
    alibi = (-(slopes[:, None, None] * (dilation * np.abs(rel))[None])).astype(np.float32)

    scores = jnp.einsum('bdiahk,bdichk->bdihac', qb, kw, preferred_element_type=jnp.float32) * (hd ** -0.5)
    scores = jnp.where(valid[None, None, :, None], scores + alibi[None, None, None], NEG_INF)
    m = scores.max(-1, keepdims=True)
    p = jnp.exp(scores - m)
    den = p.sum(-1, keepdims=True)
    out = jnp.einsum('bdihac,bdichk->bdiahk', (p / den).astype(v.dtype), vw)
    lse = (m + jnp.log(den))[..., 0]
    out = out.reshape(bsz, dilation, Lp, h, hd)[:, :, :L].transpose(0, 2, 1, 3, 4).reshape(bsz, s, h, hd)
    lse = lse.transpose(0, 1, 2, 4, 3).reshape(bsz, dilation, Lp, h)[:, :, :L].transpose(0, 2, 1, 3).reshape(bsz, s, h)
    return out, lse


def token_mixer(x, w_in, b_in, ln_v_g, ln_v_b, w_s, b_s, w_pa, w_pb, w_o):
    bsz, s, _ = x.shape
    z = jnp.dot(x, w_in) + b_in
    parts = jnp.split(z, [int(i) for i in np.cumsum(IN_SIZES)[:-1]], axis=-1)
    a_out = spatial_gating(jax.nn.gelu(parts[0], approximate=False), jax.nn.gelu(parts[1], approximate=False),
                           ln_v_g, ln_v_b, w_s, b_s)
    outs, lses = [], []
    for g, (window, dilation) in enumerate(B_PATTERNS):
        q, k, v = [t.reshape(bsz, s, B_HEADS, B_HEAD_DIM) for t in parts[2 + 3 * g: 5 + 3 * g]]
        o, lse = dilated_window_attention(q, k, v, dilation, (window // 2) // dilation, ALIBI_SLOPES[g])
        outs.append(o)
        lses.append(lse)
    wts = jax.nn.softmax(jnp.stack(lses), axis=0)
    b_out = jnp.einsum('gbsh,gbshk->bshk', wts.astype(x.dtype), jnp.stack(outs)).reshape(bsz, s, B_GROUP_WIDTH)
    gate_a = jax.nn.sigmoid(parts[-2])
    gate_b = jax.nn.sigmoid(parts[-1])
    merged = gate_a * jnp.dot(a_out, w_pa) + gate_b * jnp.dot(b_out, w_pb)
    return jnp.dot(merged, w_o)


def expert_ffn(xb, e, w_gu, b_gu, w_down, b_down):
    gu = jnp.dot(xb, w_gu[e]) + b_gu[e]
    gate, up = jnp.split(gu, 2, axis=-1)
    gate = jnp.minimum(gate, SWIGLU_LIMIT)
    up = jnp.clip(up, -SWIGLU_LIMIT, SWIGLU_LIMIT)
    h = (up + 1.0) * gate * jax.nn.sigmoid(SWIGLU_ALPHA * gate)
    return jnp.dot(h, w_down[e]) + b_down[e]


def moe(x, w_r, b_r, w_gu, b_gu, w_down, b_down):
    bsz, s, dm = x.shape
    t = bsz * s
    xf = x.reshape(t, dm)
    logits = jnp.dot(xf, w_r, preferred_element_type=jnp.float32) + b_r.astype(jnp.float32)
    top_v, top_i = lax.top_k(logits, TOP_K)
    gates = jax.nn.softmax(top_v, axis=-1).astype(x.dtype)
    n = t * TOP_K
    flat_e = top_i.reshape(n).astype(jnp.int32)
    flat_tok = jnp.repeat(jnp.arange(t, dtype=jnp.int32), TOP_K)
    order = jnp.argsort(flat_e)
    se, stok, sg = flat_e[order], flat_tok[order], gates.reshape(n)[order]
    counts = jnp.bincount(flat_e, length=N_EXPERTS)
    padded = (counts + ROUTE_BLOCK - 1) // ROUTE_BLOCK * ROUTE_BLOCK
    start = jnp.cumsum(counts) - counts
    pend = jnp.cumsum(padded)
    pstart = pend - padded
    dest = pstart[se] + jnp.arange(n, dtype=jnp.int32) - start[se]
    cap = -(-n // ROUTE_BLOCK) * ROUTE_BLOCK + N_EXPERTS * ROUTE_BLOCK
    n_blk = cap // ROUTE_BLOCK
    buf_tok = jnp.full((cap,), t, jnp.int32).at[dest].set(stok)
    blk_e = jnp.minimum(jnp.searchsorted(pend, jnp.arange(n_blk, dtype=jnp.int32) * ROUTE_BLOCK, side='right'), N_EXPERTS - 1)
    xpad = jnp.concatenate([xf, jnp.zeros((1, dm), xf.dtype)], axis=0)
    xb = xpad[buf_tok].reshape(n_blk, ROUTE_BLOCK, dm)
    yb = lax.map(lambda args: expert_ffn(args[0], args[1], w_gu, b_gu, w_down, b_down), (xb, blk_e))
    ya = yb.reshape(cap, dm)[dest] * sg[:, None]
    y = jax.ops.segment_sum(ya, stok, num_segments=t)
    return y.reshape(bsz, s, dm)


def encoder_trunk(x, params):
    (w_in, b_in, ln_v_g, ln_v_b, w_s, b_s, w_pa, w_pb, w_o, ln1_g, ln1_b,
     w_r, b_r, w_gu, b_gu, w_down, b_down, ln2_g, ln2_b) = params
    for l in range(DEPTH):
        mix = token_mixer(x, w_in[l], b_in[l], ln_v_g[l], ln_v_b[l], w_s[l], b_s[l], w_pa[l], w_pb[l], w_o[l])
        x = layer_norm(DN_ALPHA * x + mix, ln1_g[l], ln1_b[l])
        ffn = moe(x, w_r[l], b_r[l], w_gu[l], b_gu[l], w_down[l], b_down[l])
        x = layer_norm(DN_ALPHA * x + ffn, ln2_g[l], ln2_b[l])
    return x


def setup_inputs(seed: int = 0) -> dict:
    key = jax.random.key(seed)
    ks = jax.random.split(key, 24)
    nrm = jax.random.normal
    f32 = jnp.float32
    L, D, E, F = DEPTH, D_MODEL, N_EXPERTS, D_EXPERT
    return {
        'x_prompt': nrm(ks[0], (BATCH, SEQ, D), f32),
        'x_sample': nrm(ks[1], (DEC_BATCH, DEC_SEQ, D), f32),
        'w_in': nrm(ks[2], (L, D, D_IN), f32) * D ** -0.5,
        'b_in': nrm(ks[3], (L, D_IN), f32) * 0.02,
        'ln_v_g': 1.0 + 0.1 * nrm(ks[4], (L, A_WIDTH), f32),
        'ln_v_b': 0.02 * nrm(ks[5], (L, A_WIDTH), f32),
        'w_s': nrm(ks[6], (L, A_GROUPS, CHUNK, CHUNK), f32) * CHUNK ** -0.5,
        'b_s': 1.0 + 0.1 * nrm(ks[7], (L, A_GROUPS, CHUNK), f32),
        'w_pa': nrm(ks[8], (L, A_WIDTH, D), f32) * (A_WIDTH ** -0.5 * DN_BETA),
        'w_pb': nrm(ks[9], (L, B_GROUP_WIDTH, D), f32) * (B_GROUP_WIDTH ** -0.5 * DN_BETA),
        'w_o': nrm(ks[10], (L, D, D), f32) * (D ** -0.5 * DN_BETA),
        'ln1_g': 1.0 + 0.1 * nrm(ks[11], (L, D), f32),
        'ln1_b': 0.02 * nrm(ks[12], (L, D), f32),
        'w_r': nrm(ks[13], (L, D, E), f32) * D ** -0.5,
        'b_r': 0.01 * nrm(ks[14], (L, E), f32),
        'w_gu': nrm(ks[15], (L, E, D, 2 * F), f32) * (D ** -0.5 * DN_BETA),
        'b_gu': 0.02 * nrm(ks[16], (L, E, 2 * F), f32),
        'w_down': nrm(ks[17], (L, E, F, D), f32) * (F ** -0.5 * DN_BETA),
        'b_down': 0.02 * nrm(ks[18], (L, E, D), f32),
        'ln2_g': 1.0 + 0.1 * nrm(ks[19], (L, D), f32),
        'ln2_b': 0.02 * nrm(ks[20], (L, D), f32),
    }


def reference(x_prompt, x_sample, w_in, b_in, ln_v_g, ln_v_b, w_s, b_s, w_pa, w_pb, w_o, ln1_g, ln1_b,
              w_r, b_r, w_gu, b_gu, w_down, b_down, ln2_g, ln2_b):
    params = (w_in, b_in, ln_v_g, ln_v_b, w_s, b_s, w_pa, w_pb, w_o, ln1_g, ln1_b,
              w_r, b_r, w_gu, b_gu, w_down, b_down, ln2_g, ln2_b)
    y_prompt = encoder_trunk(x_prompt, params)
    y_sample = encoder_trunk(x_sample, params)
    return (y_prompt, y_sample)
```

```python
import functools

import numpy as np
import jax
import jax.numpy as jnp
from jax import lax
from jax.experimental import pallas as pl
from jax.experimental.pallas import tpu as pltpu

D_MODEL = 1024
CHUNK = 128
A_GROUPS = 8
B_PATTERNS = ((128, 1), (512, 4), (2048, 16))
B_HEADS = 4
B_HEAD_DIM = 128
B_GROUP_WIDTH = B_HEADS * B_HEAD_DIM
N_ATTN_HEADS = len(B_PATTERNS) * B_HEADS
N_SIDE = 64
D_IN = 4 * D_MODEL + 3 * len(B_PATTERNS) * B_GROUP_WIDTH
N_EXPERTS = 32
TOP_K = 4
SWIGLU_LIMIT = 7.0
SWIGLU_ALPHA = 1.702
LN_EPS = 1e-5
NEG_INF = -1e30
LANES = 128

Z_BLOCKS = D_IN // B_GROUP_WIDTH
QKV_BLOCK0 = 4 * D_MODEL // B_GROUP_WIDTH
GELU_COLS = 2 * D_MODEL

VMEM_LIMIT = 56 * 1024 * 1024

BF16 = jnp.bfloat16
F32 = jnp.float32


def _gelu(x):
    return 0.5 * x * (1.0 + lax.erf(x * np.float32(np.sqrt(0.5))))


def _layer_norm(x, g, b):
    xc = x - jnp.mean(x, axis=-1, keepdims=True)
    var = jnp.mean(xc * xc, axis=-1, keepdims=True)
    return xc * lax.rsqrt(var + LN_EPS) * g + b


def _inproj_kernel(x_ref, w_ref, b_ref, z_ref, *, n_gelu_tiles):
    acc = jnp.dot(x_ref[...], w_ref[...], preferred_element_type=F32) + b_ref[...]
    j = pl.program_id(1)

    @pl.when(j < n_gelu_tiles)
    def _():
        z_ref[...] = _gelu(acc).astype(BF16)

    @pl.when(j >= n_gelu_tiles)
    def _():
        z_ref[...] = acc.astype(BF16)


def _inproj(xb, w_in, b_in, layer, tm=1024, tn=512):
    t = xb.shape[0]
    return pl.pallas_call(
        functools.partial(_inproj_kernel, n_gelu_tiles=GELU_COLS // tn),
        grid=(t // tm, D_IN // tn),
        in_specs=[
            pl.BlockSpec((tm, D_MODEL), lambda i, j: (i, 0)),
            pl.BlockSpec((None, D_MODEL, tn), lambda i, j: (layer, 0, j)),
            pl.BlockSpec((None, 1, tn), lambda i, j: (layer, 0, j)),
        ],
        out_specs=pl.BlockSpec((tm, tn), lambda i, j: (i, j)),
        out_shape=jax.ShapeDtypeStruct((t, D_IN), BF16),
        compiler_params=pltpu.CompilerParams(
            dimension_semantics=("parallel", "arbitrary"), vmem_limit_bytes=VMEM_LIMIT),
        name="inproj",
    )(xb, w_in, b_in)


def _attn_kernel(q_ref, kp_ref, kc_ref, kn_ref, vp_ref, vc_ref, vn_ref, bias_ref, o_ref, lse_ref,
                 *, bq):
    jq = pl.program_id(2)
    nq = pl.num_programs(2)
    n_piece = bq // N_SIDE
    kcat = jnp.concatenate([kp_ref[...], kc_ref[...], kn_ref[...]], axis=0)
    vcat = jnp.concatenate([vp_ref[...], vc_ref[...], vn_ref[...]], axis=0)
    col = lax.broadcasted_iota(jnp.int32, (N_SIDE, 3 * N_SIDE), 1)
    lane = lax.broadcasted_iota(jnp.int32, (N_SIDE, LANES), 1)
    scale = np.float32(B_HEAD_DIM ** -0.5)
    for p in range(n_piece):
        r0 = p * N_SIDE
        lse_tile = jnp.zeros((N_SIDE, LANES), F32)
        for h in range(B_HEADS):
            c0 = h * B_HEAD_DIM
            qh = q_ref[r0:r0 + N_SIDE, c0:c0 + B_HEAD_DIM]
            kh = kcat[r0:r0 + 3 * N_SIDE, c0:c0 + B_HEAD_DIM]
            vh = vcat[r0:r0 + 3 * N_SIDE, c0:c0 + B_HEAD_DIM]
            s = lax.dot_general(qh, kh, (((1,), (1,)), ((), ())), preferred_element_type=F32)
            s = s * scale + bias_ref[h]
            if p == 0:
                s = jnp.where(jnp.logical_and(col < N_SIDE, jq == 0), NEG_INF, s)
            if p == n_piece - 1:
                s = jnp.where(jnp.logical_and(col >= 2 * N_SIDE, jq == nq - 1), NEG_INF, s)
            m = jnp.max(s, axis=-1, keepdims=True)
            e = jnp.exp(s - m)
            den = jnp.sum(e, axis=-1, keepdims=True)
            pn = (e / den).astype(BF16)
            o = jnp.dot(pn, vh, preferred_element_type=F32)
            o_ref[r0:r0 + N_SIDE, c0:c0 + B_HEAD_DIM] = o.astype(BF16)
            lse_tile = jnp.where(lane == h, m + jnp.log(den), lse_tile)
        lse_ref[r0:r0 + N_SIDE, :] = lse_tile


def _attn_bias(group):
    _, dilation = B_PATTERNS[group]
    slopes = np.array([2.0 ** (-8.0 * (group * B_HEADS + h + 1) / N_ATTN_HEADS) for h in range(B_HEADS)],
                      dtype=np.float32)
    a = np.arange(N_SIDE)[:, None]
    c = np.arange(3 * N_SIDE)[None, :]
    rel = c - N_SIDE - a
    alibi = (-(slopes[:, None, None] * (dilation * np.abs(rel))[None])).astype(np.float32)
    return np.where((np.abs(rel) <= N_SIDE)[None], alibi, np.float32(NEG_INF)).astype(np.float32)


def _attention(z, group, n_seq, seq):
    _, dilation = B_PATTERNS[group]
    sub_len = seq // dilation
    bq = min(256, sub_len)
    nq = sub_len // bq
    halo_per_bq = bq // N_SIDE
    n_halo = sub_len // N_SIDE
    zv = z.reshape(n_seq, sub_len, dilation * D_IN)
    cq = QKV_BLOCK0 + 3 * group

    def cur(off):
        return pl.BlockSpec((None, bq, B_GROUP_WIDTH), lambda b, r, j: (b, j, r * Z_BLOCKS + cq + off))

    def prev(off):
        return pl.BlockSpec((None, N_SIDE, B_GROUP_WIDTH),
                            lambda b, r, j: (b, jnp.maximum(j * halo_per_bq - 1, 0), r * Z_BLOCKS + cq + off))

    def nxt(off):
        return pl.BlockSpec((None, N_SIDE, B_GROUP_WIDTH),
                            lambda b, r, j: (b, jnp.minimum((j + 1) * halo_per_bq, n_halo - 1),
                                             r * Z_BLOCKS + cq + off))

    o, lse = pl.pallas_call(
        functools.partial(_attn_kernel, bq=bq),
        grid=(n_seq, dilation, nq),
        in_specs=[cur(0), prev(1), cur(1), nxt(1), prev(2), cur(2), nxt(2),
                  pl.BlockSpec((B_HEADS, N_SIDE, 3 * N_SIDE), lambda b, r, j: (0, 0, 0))],
        out_specs=[pl.BlockSpec((None, bq, B_GROUP_WIDTH), lambda b, r, j: (b, j, r)),
                   pl.BlockSpec((None, bq, LANES), lambda b, r, j: (b, j, r))],
        out_shape=[jax.ShapeDtypeStruct((n_seq, sub_len, dilation * B_GROUP_WIDTH), BF16),
                   jax.ShapeDtypeStruct((n_seq, sub_len, dilation * LANES), F32)],
        compiler_params=pltpu.CompilerParams(
            dimension_semantics=("parallel", "parallel", "arbitrary"), vmem_limit_bytes=VMEM_LIMIT),
        name=f"attn{group}",
    )(zv, zv, zv, zv, zv, zv, zv, jnp.asarray(_attn_bias(group)))
    return o.reshape(n_seq * seq, B_GROUP_WIDTH), lse.reshape(n_seq * seq, LANES)


def _mixer_kernel(u_ref, v_ref, ga_ref, gb_ref, o0_ref, o1_ref, o2_ref, l0_ref, l1_ref, l2_ref, x_ref,
                  lnvg_ref, lnvb_ref, ws_ref, bs_ref, wpa_ref, wpb_ref, wo_ref, ln1g_ref, ln1b_ref,
                  wr_ref, br_ref, x1_ref, x1b_ref, logit_ref, a_scr, b_scr, *, tm, dn_alpha):
    vn = _layer_norm(v_ref[...].astype(F32), lnvg_ref[...], lnvb_ref[...]).astype(BF16)
    for c in range(tm // CHUNK):
        r0 = c * CHUNK
        for g in range(A_GROUPS):
            c0 = g * CHUNK
            mixed = jnp.dot(ws_ref[g], vn[r0:r0 + CHUNK, c0:c0 + CHUNK], preferred_element_type=F32)
            mixed = mixed + bs_ref[g]
            a = u_ref[r0:r0 + CHUNK, c0:c0 + CHUNK].astype(F32) * mixed
            a_scr[r0:r0 + CHUNK, c0:c0 + CHUNK] = a.astype(BF16)
    o_refs = (o0_ref, o1_ref, o2_ref)
    l_refs = (l0_ref, l1_ref, l2_ref)
    for h in range(B_HEADS):
        c0 = h * B_HEAD_DIM
        ls = [l_ref[:, h:h + 1] for l_ref in l_refs]
        mx = jnp.maximum(jnp.maximum(ls[0], ls[1]), ls[2])
        es = [jnp.exp(l - mx) for l in ls]
        den = es[0] + es[1] + es[2]
        acc = jnp.zeros((tm, B_HEAD_DIM), F32)
        for g in range(len(B_PATTERNS)):
            acc = acc + (es[g] / den) * o_refs[g][:, c0:c0 + B_HEAD_DIM].astype(F32)
        b_scr[:, c0:c0 + B_HEAD_DIM] = acc.astype(BF16)
    pa = jnp.dot(a_scr[...], wpa_ref[...], preferred_element_type=F32)
    pb = jnp.dot(b_scr[...], wpb_ref[...], preferred_element_type=F32)
    merged = jax.nn.sigmoid(ga_ref[...].astype(F32)) * pa + jax.nn.sigmoid(gb_ref[...].astype(F32)) * pb
    mix = jnp.dot(merged.astype(BF16), wo_ref[...], preferred_element_type=F32)
    x1 = _layer_norm(dn_alpha * x_ref[...] + mix, ln1g_ref[...], ln1b_ref[...])
    x1_ref[...] = x1
    x1b = x1.astype(BF16)
    x1b_ref[...] = x1b
    logit_ref[...] = jnp.dot(x1b, wr_ref[...], preferred_element_type=F32) + br_ref[...]


def _mixer(z, attn, x, p, layer, dn_alpha, tm=512):
    t = x.shape[0]

    def zspec(cb):
        return pl.BlockSpec((tm, D_MODEL), lambda i: (i, cb))

    def row(width):
        return pl.BlockSpec((tm, width), lambda i: (i, 0))

    def vec(width):
        return pl.BlockSpec((None, 1, width), lambda i: (layer, 0, 0))

    def mat(k, n):
        return pl.BlockSpec((None, k, n), lambda i: (layer, 0, 0))

    (o0, l0), (o1, l1), (o2, l2) = attn
    return pl.pallas_call(
        functools.partial(_mixer_kernel, tm=tm, dn_alpha=np.float32(dn_alpha)),
        grid=(t // tm,),
        in_specs=[zspec(0), zspec(1), zspec(2), zspec(3),
                  row(B_GROUP_WIDTH), row(B_GROUP_WIDTH), row(B_GROUP_WIDTH),
                  row(LANES), row(LANES), row(LANES), row(D_MODEL),
                  vec(D_MODEL), vec(D_MODEL),
                  pl.BlockSpec((None, A_GROUPS, CHUNK, CHUNK), lambda i: (layer, 0, 0, 0)),
                  pl.BlockSpec((None, A_GROUPS, CHUNK, CHUNK), lambda i: (layer, 0, 0, 0)),
                  mat(D_MODEL, D_MODEL), mat(B_GROUP_WIDTH, D_MODEL), mat(D_MODEL, D_MODEL),
                  vec(D_MODEL), vec(D_MODEL), mat(D_MODEL, LANES), vec(LANES)],
        out_specs=[row(D_MODEL), row(D_MODEL), row(LANES)],
        out_shape=[jax.ShapeDtypeStruct((t, D_MODEL), F32), jax.ShapeDtypeStruct((t, D_MODEL), BF16),
                   jax.ShapeDtypeStruct((t, LANES), F32)],
        scratch_shapes=[pltpu.VMEM((tm, D_MODEL), BF16), pltpu.VMEM((tm, B_GROUP_WIDTH), BF16)],
        compiler_params=pltpu.CompilerParams(
            dimension_semantics=("parallel",), vmem_limit_bytes=VMEM_LIMIT),
        name="mixer",
    )(z, z, z, z, o0, o1, o2, l0, l1, l2, x,
      p["ln_v_g"], p["ln_v_b"], p["w_s"], p["b_s"], p["w_pa"], p["w_pb"], p["w_o"],
      p["ln1_g"], p["ln1_b"], p["w_r"], p["b_r"])


def _router_kernel(logit_ref, tri_ref, route_ref, gate_ref, cnt_ref, carry):
    @pl.when(pl.program_id(0) == 0)
    def _():
        carry[...] = jnp.zeros_like(carry)

    l = logit_ref[...]
    tm = l.shape[0]
    lane = lax.broadcasted_iota(jnp.int32, (tm, LANES), 1)
    onehots, vals, ids = [], [], []
    for _ in range(TOP_K):
        mx = jnp.max(l, axis=-1, keepdims=True)
        idx = jnp.min(jnp.where(l == mx, lane, LANES), axis=-1, keepdims=True)
        oh = lane == idx
        l = jnp.where(oh, -jnp.inf, l)
        onehots.append(oh)
        vals.append(mx)
        ids.append(idx)
    es = [jnp.exp(v - vals[0]) for v in vals]
    den = es[0] + es[1] + es[2] + es[3]
    multi = jnp.zeros((tm, LANES), F32)
    for oh in onehots:
        multi = multi + oh.astype(F32)
    prefix = jnp.dot(tri_ref[...], multi.astype(BF16), preferred_element_type=F32) + carry[0:1, :]
    route = jnp.zeros((tm, LANES), jnp.int32)
    gate = jnp.zeros((tm, LANES), F32)
    for k in range(TOP_K):
        rank = jnp.sum(jnp.where(onehots[k], prefix, 0.0), axis=-1, keepdims=True).astype(jnp.int32)
        route = jnp.where(lane == k, ids[k], route)
        route = jnp.where(lane == TOP_K + k, rank, route)
        gate = jnp.where(lane == k, es[k] / den, gate)
    route_ref[...] = route
    gate_ref[...] = gate
    new_carry = carry[0:1, :] + jnp.sum(multi, axis=0, keepdims=True)
    carry[...] = jnp.broadcast_to(new_carry, carry.shape)
    cnt_ref[...] = jnp.broadcast_to(new_carry, cnt_ref.shape)


def _router(logits, tm=512):
    t = logits.shape[0]
    tri = jnp.asarray(np.tril(np.ones((tm, tm), np.float32), -1), dtype=BF16)
    return pl.pallas_call(
        _router_kernel,
        grid=(t // tm,),
        in_specs=[pl.BlockSpec((tm, LANES), lambda i: (i, 0)),
                  pl.BlockSpec((tm, tm), lambda i: (0, 0))],
        out_specs=[pl.BlockSpec((tm, LANES), lambda i: (i, 0)),
                   pl.BlockSpec((tm, LANES), lambda i: (i, 0)),
                   pl.BlockSpec((8, LANES), lambda i: (0, 0))],
        out_shape=[jax.ShapeDtypeStruct((t, LANES), jnp.int32), jax.ShapeDtypeStruct((t, LANES), F32),
                   jax.ShapeDtypeStruct((8, LANES), F32)],
        scratch_shapes=[pltpu.VMEM((8, LANES), F32)],
        compiler_params=pltpu.CompilerParams(dimension_semantics=("arbitrary",)),
        name="router",
    )(logits, tri)


def _ffn_kernel(blk_e_ref, n_used_ref, x_ref, wgu_ref, bgu_ref, wd_ref, bd_ref, y_ref):
    @pl.when(pl.program_id(0) < n_used_ref[0])
    def _():
        gu = jnp.dot(x_ref[...], wgu_ref[...], preferred_element_type=F32) + bgu_ref[...]
        gate = jnp.minimum(gu[:, :D_MODEL], SWIGLU_LIMIT)
        up = jnp.clip(gu[:, D_MODEL:], -SWIGLU_LIMIT, SWIGLU_LIMIT)
        h = (up + 1.0) * gate * jax.nn.sigmoid(SWIGLU_ALPHA * gate)
        y = jnp.dot(h.astype(BF16), wd_ref[...], preferred_element_type=F32) + bd_ref[...]
        y_ref[...] = y.astype(BF16)


def _ffn(xs, blk_e, n_used, p, layer, bm):
    n_pad = xs.shape[0]

    def rows(i, be, nu):
        return (jnp.minimum(i, nu[0] - 1), 0)

    grid_spec = pltpu.PrefetchScalarGridSpec(
        num_scalar_prefetch=2,
        grid=(n_pad // bm,),
        in_specs=[pl.BlockSpec((bm, D_MODEL), rows),
                  pl.BlockSpec((None, None, D_MODEL, 2 * D_MODEL), lambda i, be, nu: (layer, be[i], 0, 0)),
                  pl.BlockSpec((None, None, 1, 2 * D_MODEL), lambda i, be, nu: (layer, be[i], 0, 0)),
                  pl.BlockSpec((None, None, D_MODEL, D_MODEL), lambda i, be, nu: (layer, be[i], 0, 0)),
                  pl.BlockSpec((None, None, 1, D_MODEL), lambda i, be, nu: (layer, be[i], 0, 0))],
        out_specs=pl.BlockSpec((bm, D_MODEL), rows),
    )
    return pl.pallas_call(
        _ffn_kernel,
        grid_spec=grid_spec,
        out_shape=jax.ShapeDtypeStruct((n_pad, D_MODEL), BF16),
        compiler_params=pltpu.CompilerParams(
            dimension_semantics=("arbitrary",), vmem_limit_bytes=VMEM_LIMIT),
        name="ffn",
    )(blk_e, n_used, xs, p["w_gu"], p["b_gu"], p["w_down"], p["b_down"])


def _combine_kernel(y_ref, gate_ref, x_ref, g_ref, b_ref, x2_ref, x2b_ref, *, dn_alpha):
    acc = dn_alpha * x_ref[...]
    for k in range(TOP_K):
        acc = acc + gate_ref[:, k:k + 1] * y_ref[k].astype(F32)
    x2 = _layer_norm(acc, g_ref[...], b_ref[...])
    x2_ref[...] = x2
    x2b_ref[...] = x2.astype(BF16)


def _combine(yg, gates, x1, p, layer, dn_alpha, tm=512):
    t = x1.shape[0]
    return pl.pallas_call(
        functools.partial(_combine_kernel, dn_alpha=np.float32(dn_alpha)),
        grid=(t // tm,),
        in_specs=[pl.BlockSpec((TOP_K, tm, D_MODEL), lambda i: (0, i, 0)),
                  pl.BlockSpec((tm, LANES), lambda i: (i, 0)),
                  pl.BlockSpec((tm, D_MODEL), lambda i: (i, 0)),
                  pl.BlockSpec((None, 1, D_MODEL), lambda i: (layer, 0, 0)),
                  pl.BlockSpec((None, 1, D_MODEL), lambda i: (layer, 0, 0))],
        out_specs=[pl.BlockSpec((tm, D_MODEL), lambda i: (i, 0)),
                   pl.BlockSpec((tm, D_MODEL), lambda i: (i, 0))],
        out_shape=[jax.ShapeDtypeStruct((t, D_MODEL), F32), jax.ShapeDtypeStruct((t, D_MODEL), BF16)],
        compiler_params=pltpu.CompilerParams(
            dimension_semantics=("parallel",), vmem_limit_bytes=VMEM_LIMIT),
        name="combine",
    )(yg, gates, x1, p["ln2_g"], p["ln2_b"])


def _prepare_params(w_in, b_in, ln_v_g, ln_v_b, w_s, b_s, w_pa, w_pb, w_o, ln1_g, ln1_b,
                    w_r, b_r, w_gu, b_gu, w_down, b_down, ln2_g, ln2_b):
    depth = w_in.shape[0]
    n_qkv = 3 * len(B_PATTERNS) * B_GROUP_WIDTH
    qkv0, qkv1 = 2 * D_MODEL, 2 * D_MODEL + n_qkv

    def permute_cols(a):
        return jnp.concatenate([a[..., :qkv0], a[..., qkv1:], a[..., qkv0:qkv1]], axis=-1)

    def vec(a):
        return a.reshape(depth, 1, a.shape[-1]).astype(F32)

    w_r_pad = jnp.pad(w_r, ((0, 0), (0, 0), (0, LANES - N_EXPERTS))).astype(BF16)
    b_r_pad = jnp.pad(b_r.astype(F32), ((0, 0), (0, LANES - N_EXPERTS)), constant_values=NEG_INF)
    return dict(
        w_in=permute_cols(w_in).astype(BF16), b_in=vec(permute_cols(b_in)),
        ln_v_g=vec(ln_v_g), ln_v_b=vec(ln_v_b),
        w_s=w_s.astype(BF16),
        b_s=jnp.broadcast_to(b_s.astype(F32)[..., None], b_s.shape + (CHUNK,)),
        w_pa=w_pa.astype(BF16), w_pb=w_pb.astype(BF16), w_o=w_o.astype(BF16),
        ln1_g=vec(ln1_g), ln1_b=vec(ln1_b),
        w_r=w_r_pad, b_r=vec(b_r_pad),
        w_gu=w_gu.astype(BF16), b_gu=b_gu.astype(F32)[:, :, None, :],
        w_down=w_down.astype(BF16), b_down=b_down.astype(F32)[:, :, None, :],
        ln2_g=vec(ln2_g), ln2_b=vec(ln2_b),
    )


def _trunk(x, params, bm=512):
    n_seq, seq, _ = x.shape
    t = n_seq * seq
    depth = params["w_in"].shape[0]
    dn_alpha = (2.0 * depth) ** 0.25
    n_pad = t * TOP_K + N_EXPERTS * bm
    n_blk = n_pad // bm
    x = x.reshape(t, D_MODEL)
    xb = x.astype(BF16)
    tok = jnp.repeat(jnp.arange(t, dtype=jnp.int32), TOP_K)
    for layer in range(depth):
        z = _inproj(xb, params["w_in"], params["b_in"], layer)
        attn = [_attention(z, g, n_seq, seq) for g in range(len(B_PATTERNS))]
        x1, x1b, logits = _mixer(z, attn, x, params, layer, dn_alpha)
        route, gates, cnt = _router(logits)
        counts = cnt[0, :N_EXPERTS].astype(jnp.int32)
        padded = (counts + bm - 1) // bm * bm
        pend = jnp.cumsum(padded)
        pstart = pend - padded
        eid = route[:, :TOP_K]
        slot = pstart[eid] + route[:, TOP_K:2 * TOP_K]
        blk_e = jnp.minimum(
            jnp.searchsorted(pend, jnp.arange(n_blk, dtype=jnp.int32) * bm, side="right"),
            N_EXPERTS - 1).astype(jnp.int32)
        n_used = (pend[-1:] // bm).astype(jnp.int32)
        tok_of_slot = jnp.zeros((n_pad,), jnp.int32).at[slot.reshape(-1)].set(tok)
        xs = x1b[tok_of_slot]
        ys = _ffn(xs, blk_e, n_used, params, layer, bm)
        yg = ys[slot.T]
        x, xb = _combine(yg, gates, x1, params, layer, dn_alpha)
    return x.reshape(n_seq, seq, D_MODEL)


def kernel(x_prompt, x_sample, w_in, b_in, ln_v_g, ln_v_b, w_s, b_s, w_pa, w_pb, w_o, ln1_g, ln1_b,
           w_r, b_r, w_gu, b_gu, w_down, b_down, ln2_g, ln2_b):
    params = _prepare_params(w_in, b_in, ln_v_g, ln_v_b, w_s, b_s, w_pa, w_pb, w_o, ln1_g, ln1_b,
                             w_r, b_r, w_gu, b_gu, w_down, b_down, ln2_g, ln2_b)
    n_prompt = x_prompt.shape[0]
    y = _trunk(jnp.concatenate([x_prompt, x_sample], axis=0), params)
    return (y[:n_prompt], y[n_prompt:])
```

```python
import functools

import numpy as np
import jax
import jax.numpy as jnp
from jax import lax
from jax.experimental import pallas as pl
from jax.experimental.pallas import tpu as pltpu

D_MODEL = 1024
CHUNK = 128
A_GROUPS = 8
B_PATTERNS = ((128, 1), (512, 4), (2048, 16))
B_HEADS = 4
B_HEAD_DIM = 128
B_GROUP_WIDTH = B_HEADS * B_HEAD_DIM
N_ATTN_HEADS = len(B_PATTERNS) * B_HEADS
N_SIDE = 64
D_IN = 4 * D_MODEL + 3 * len(B_PATTERNS) * B_GROUP_WIDTH
N_EXPERTS = 32
TOP_K = 4
SWIGLU_LIMIT = 7.0
SWIGLU_ALPHA = 1.702
LN_EPS = 1e-5
NEG_INF = -1e30
LANES = 128

MAIN_COLS = 4 * D_MODEL
QKV_COLS = 3 * B_GROUP_WIDTH
GELU_COLS = 2 * D_MODEL

VMEM_LIMIT = 56 * 1024 * 1024

BF16 = jnp.bfloat16
F32 = jnp.float32


def _gelu(x):
    return 0.5 * x * (1.0 + lax.erf(x * np.float32(np.sqrt(0.5))))


def _layer_norm(x, g, b):
    xc = x - jnp.mean(x, axis=-1, keepdims=True)
    var = jnp.mean(xc * xc, axis=-1, keepdims=True)
    return xc * lax.rsqrt(var + LN_EPS) * g + b


def _inproj_main_kernel(x_ref, w_ref, b_ref, z_ref, *, n_gelu_tiles):
    acc = jnp.dot(x_ref[...], w_ref[...], preferred_element_type=F32) + b_ref[...]
    j = pl.program_id(1)

    @pl.when(j < n_gelu_tiles)
    def _():
        z_ref[...] = _gelu(acc).astype(BF16)

    @pl.when(j >= n_gelu_tiles)
    def _():
        z_ref[...] = acc.astype(BF16)


def _inproj_main(xb, w_in, b_in, layer, tm=1024, tn=512):
    t = xb.shape[0]
    return pl.pallas_call(
        functools.partial(_inproj_main_kernel, n_gelu_tiles=GELU_COLS // tn),
        grid=(t // tm, MAIN_COLS // tn),
        in_specs=[
            pl.BlockSpec((tm, D_MODEL), lambda i, j: (i, 0)),
            pl.BlockSpec((None, D_MODEL, tn), lambda i, j: (layer, 0, j)),
            pl.BlockSpec((None, 1, tn), lambda i, j: (layer, 0, j)),
        ],
        out_specs=pl.BlockSpec((tm, tn), lambda i, j: (i, j)),
        out_shape=jax.ShapeDtypeStruct((t, MAIN_COLS), BF16),
        compiler_params=pltpu.CompilerParams(
            dimension_semantics=("parallel", "arbitrary"), vmem_limit_bytes=VMEM_LIMIT),
        name="inproj_main",
    )(xb, w_in, b_in)


def _inproj_qkv_kernel(x_ref, w_ref, b_ref, o_ref, *scratch, dilation):
    acc = jnp.dot(x_ref[...], w_ref[...], preferred_element_type=F32) + b_ref[...]
    if dilation == 1:
        o_ref[0] = acc.astype(BF16)
    else:
        (scr,) = scratch
        rows = acc.shape[0] // dilation
        for c in range(acc.shape[1] // LANES):
            scr[c] = acc[:, c * LANES:(c + 1) * LANES]
        for r in range(dilation):
            for c in range(acc.shape[1] // LANES):
                o_ref[r, :, c * LANES:(c + 1) * LANES] = (
                    scr[c, pl.ds(r, rows, stride=dilation), :].astype(BF16))


def _inproj_qkv(xb, w_in, b_in, layer, group, n_seq, seq, tm=1024, tn=512):
    _, dilation = B_PATTERNS[group]
    tiles_per_seq = seq // tm
    col0 = (MAIN_COLS + group * QKV_COLS) // tn
    return pl.pallas_call(
        functools.partial(_inproj_qkv_kernel, dilation=dilation),
        grid=(n_seq * tiles_per_seq, QKV_COLS // tn),
        in_specs=[
            pl.BlockSpec((tm, D_MODEL), lambda i, j: (i, 0)),
            pl.BlockSpec((None, D_MODEL, tn), lambda i, j: (layer, 0, col0 + j)),
            pl.BlockSpec((None, 1, tn), lambda i, j: (layer, 0, col0 + j)),
        ],
        out_specs=pl.BlockSpec((None, dilation, tm // dilation, tn),
                               lambda i, j: (i // tiles_per_seq, 0, i % tiles_per_seq, j)),
        out_shape=jax.ShapeDtypeStruct((n_seq, dilation, seq // dilation, QKV_COLS), BF16),
        scratch_shapes=[] if dilation == 1 else [pltpu.VMEM((tn // LANES, tm, LANES), F32)],
        compiler_params=pltpu.CompilerParams(
            dimension_semantics=("parallel", "arbitrary"), vmem_limit_bytes=VMEM_LIMIT),
        name=f"inproj_qkv{group}",
    )(xb, w_in, b_in)


def _attn_kernel(q_ref, kp_ref, kc_ref, kn_ref, vp_ref, vc_ref, vn_ref, bias_ref, o_ref, lse_ref,
                 s_scr, p_scr, *, bq):
    jq = pl.program_id(2)
    nq = pl.num_programs(2)
    n_piece = bq // N_SIDE
    col = lax.broadcasted_iota(jnp.int32, (N_SIDE, 3 * N_SIDE), 1)
    lane = lax.broadcasted_iota(jnp.int32, (bq, LANES), 1)
    scale = np.float32(B_HEAD_DIM ** -0.5)

    def window(p, prev_ref, cur_ref, next_ref, c0):
        cols = slice(c0, c0 + B_HEAD_DIM)
        if n_piece == 1:
            return jnp.concatenate([prev_ref[:, cols], cur_ref[:, cols], next_ref[:, cols]], axis=0)
        if p == 0:
            return jnp.concatenate([prev_ref[:, cols], cur_ref[0:2 * N_SIDE, cols]], axis=0)
        if p == n_piece - 1:
            return jnp.concatenate([cur_ref[bq - 2 * N_SIDE:bq, cols], next_ref[:, cols]], axis=0)
        return cur_ref[(p - 1) * N_SIDE:(p + 2) * N_SIDE, cols]

    for h in range(B_HEADS):
        c0 = h * B_HEAD_DIM
        for p in range(n_piece):
            r0 = p * N_SIDE
            qh = q_ref[r0:r0 + N_SIDE, c0:c0 + B_HEAD_DIM]
            kh = window(p, kp_ref, kc_ref, kn_ref, c0)
            s = lax.dot_general(qh, kh, (((1,), (1,)), ((), ())), preferred_element_type=F32) * scale
            if p == 0:
                s = jnp.where(jnp.logical_and(col < N_SIDE, jq == 0), NEG_INF, s)
            if p == n_piece - 1:
                s = jnp.where(jnp.logical_and(col >= 2 * N_SIDE, jq == nq - 1), NEG_INF, s)
            s_scr[h, r0:r0 + N_SIDE, :] = s
    lse_tile = jnp.zeros((bq, LANES), F32)
    for h in range(B_HEADS):
        s = s_scr[h] + bias_ref[h]
        m = jnp.max(s, axis=-1, keepdims=True)
        e = jnp.exp(s - m)
        den = jnp.sum(e, axis=-1, keepdims=True)
        p_scr[h] = (e / den).astype(BF16)
        lse_tile = jnp.where(lane == h, m + jnp.log(den), lse_tile)
    lse_ref[...] = lse_tile
    for h in range(B_HEADS):
        c0 = h * B_HEAD_DIM
        for p in range(n_piece):
            r0 = p * N_SIDE
            vh = window(p, vp_ref, vc_ref, vn_ref, c0)
            o = jnp.dot(p_scr[h, r0:r0 + N_SIDE, :], vh, preferred_element_type=F32)
            o_ref[r0:r0 + N_SIDE, c0:c0 + B_HEAD_DIM] = o.astype(BF16)


def _attn_bias(group, n_piece):
    _, dilation = B_PATTERNS[group]
    slopes = np.array([2.0 ** (-8.0 * (group * B_HEADS + h + 1) / N_ATTN_HEADS) for h in range(B_HEADS)],
                      dtype=np.float32)
    a = np.arange(N_SIDE)[:, None]
    c = np.arange(3 * N_SIDE)[None, :]
    rel = c - N_SIDE - a
    alibi = (-(slopes[:, None, None] * (dilation * np.abs(rel))[None])).astype(np.float32)
    bias = np.where((np.abs(rel) <= N_SIDE)[None], alibi, np.float32(NEG_INF)).astype(np.float32)
    return np.tile(bias, (1, n_piece, 1))


def _attention(qkv, group, bq_max=512):
    n_seq, dilation, sub_len, _ = qkv.shape
    bq = min(bq_max, sub_len)
    nq = sub_len // bq
    halo_per_bq = bq // N_SIDE
    n_halo = sub_len // N_SIDE

    def cur(part):
        return pl.BlockSpec((None, None, bq, B_GROUP_WIDTH), lambda b, r, j: (b, r, j, part))

    def prev(part):
        return pl.BlockSpec((None, None, N_SIDE, B_GROUP_WIDTH),
                            lambda b, r, j: (b, r, jnp.maximum(j * halo_per_bq - 1, 0), part))

    def nxt(part):
        return pl.BlockSpec((None, None, N_SIDE, B_GROUP_WIDTH),
                            lambda b, r, j: (b, r, jnp.minimum((j + 1) * halo_per_bq, n_halo - 1), part))

    return pl.pallas_call(
        functools.partial(_attn_kernel, bq=bq),
        grid=(n_seq, dilation, nq),
        in_specs=[cur(0), prev(1), cur(1), nxt(1), prev(2), cur(2), nxt(2),
                  pl.BlockSpec((B_HEADS, bq, 3 * N_SIDE), lambda b, r, j: (0, 0, 0))],
        out_specs=[pl.BlockSpec((None, None, bq, B_GROUP_WIDTH), lambda b, r, j: (b, r, j, 0)),
                   pl.BlockSpec((None, None, bq, LANES), lambda b, r, j: (b, r, j, 0))],
        out_shape=[jax.ShapeDtypeStruct((n_seq, dilation, sub_len, B_GROUP_WIDTH), BF16),
                   jax.ShapeDtypeStruct((n_seq, dilation, sub_len, LANES), F32)],
        scratch_shapes=[pltpu.VMEM((B_HEADS, bq, 3 * N_SIDE), F32),
                        pltpu.VMEM((B_HEADS, bq, 3 * N_SIDE), BF16)],
        compiler_params=pltpu.CompilerParams(
            dimension_semantics=("parallel", "parallel", "arbitrary"), vmem_limit_bytes=VMEM_LIMIT),
        name=f"attn{group}",
    )(qkv, qkv, qkv, qkv, qkv, qkv, qkv, jnp.asarray(_attn_bias(group, bq // N_SIDE)))


def _mixer_kernel(u_ref, v_ref, ga_ref, gb_ref, o0_ref, o1_ref, o2_ref, l0_ref, l1_ref, l2_ref, x_ref,
                  lnvg_ref, lnvb_ref, ws_ref, bs_ref, wpa_ref, wpb_ref, wo_ref, ln1g_ref, ln1b_ref,
                  wr_ref, br_ref, x1_ref, x1b_ref, logit_ref, a_scr, b_scr, o_scr, l_scr, *, tm, dn_alpha):
    vn = _layer_norm(v_ref[...].astype(F32), lnvg_ref[...], lnvb_ref[...]).astype(BF16)
    for c in range(tm // CHUNK):
        r0 = c * CHUNK
        for g in range(A_GROUPS):
            c0 = g * CHUNK
            mixed = jnp.dot(ws_ref[g], vn[r0:r0 + CHUNK, c0:c0 + CHUNK], preferred_element_type=F32)
            mixed = mixed + bs_ref[g]
            a = u_ref[r0:r0 + CHUNK, c0:c0 + CHUNK].astype(F32) * mixed
            a_scr[r0:r0 + CHUNK, c0:c0 + CHUNK] = a.astype(BF16)
    for g, (o_ref, l_ref) in enumerate(((o0_ref, l0_ref), (o1_ref, l1_ref), (o2_ref, l2_ref))):
        dilation = o_ref.shape[0]
        rows = tm // dilation
        for r in range(dilation):
            idx = pl.ds(r, rows, stride=dilation) if dilation > 1 else slice(None)
            for h in range(B_HEADS):
                o_scr[g, h, idx, :] = o_ref[r, :, h * B_HEAD_DIM:(h + 1) * B_HEAD_DIM].astype(F32)
            l_scr[g, idx, :] = l_ref[r]
    n_groups = len(B_PATTERNS)
    for h in range(B_HEADS):
        c0 = h * B_HEAD_DIM
        ls = [l_scr[g, :, h:h + 1] for g in range(n_groups)]
        mx = jnp.maximum(jnp.maximum(ls[0], ls[1]), ls[2])
        es = [jnp.exp(l - mx) for l in ls]
        den = es[0] + es[1] + es[2]
        acc = jnp.zeros((tm, B_HEAD_DIM), F32)
        for g in range(n_groups):
            acc = acc + (es[g] / den) * o_scr[g, h]
        b_scr[:, c0:c0 + B_HEAD_DIM] = acc.astype(BF16)
    pa = jnp.dot(a_scr[...], wpa_ref[...], preferred_element_type=F32)
    pb = jnp.dot(b_scr[...], wpb_ref[...], preferred_element_type=F32)
    merged = jax.nn.sigmoid(ga_ref[...].astype(F32)) * pa + jax.nn.sigmoid(gb_ref[...].astype(F32)) * pb
    mix = jnp.dot(merged.astype(BF16), wo_ref[...], preferred_element_type=F32)
    x1 = _layer_norm(dn_alpha * x_ref[...] + mix, ln1g_ref[...], ln1b_ref[...])
    x1_ref[...] = x1
    x1b = x1.astype(BF16)
    x1b_ref[...] = x1b
    logit_ref[...] = jnp.dot(x1b, wr_ref[...], preferred_element_type=F32) + br_ref[...]


def _mixer(z, attn, x, p, layer, dn_alpha, seq, tm=512):
    t = x.shape[0]
    tiles_per_seq = seq // tm

    def zspec(cb):
        return pl.BlockSpec((tm, D_MODEL), lambda i: (i, cb))

    def row(width):
        return pl.BlockSpec((tm, width), lambda i: (i, 0))

    def split(a):
        _, dilation, _, width = a.shape
        return pl.BlockSpec((None, dilation, tm // dilation, width),
                            lambda i: (i // tiles_per_seq, 0, i % tiles_per_seq, 0))

    def vec(width):
        return pl.BlockSpec((None, 1, width), lambda i: (layer, 0, 0))

    def mat(k, n):
        return pl.BlockSpec((None, k, n), lambda i: (layer, 0, 0))

    (o0, l0), (o1, l1), (o2, l2) = attn
    return pl.pallas_call(
        functools.partial(_mixer_kernel, tm=tm, dn_alpha=np.float32(dn_alpha)),
        grid=(t // tm,),
        in_specs=[zspec(0), zspec(1), zspec(2), zspec(3),
                  split(o0), split(o1), split(o2), split(l0), split(l1), split(l2), row(D_MODEL),
                  vec(D_MODEL), vec(D_MODEL),
                  pl.BlockSpec((None, A_GROUPS, CHUNK, CHUNK), lambda i: (layer, 0, 0, 0)),
                  pl.BlockSpec((None, A_GROUPS, CHUNK, CHUNK), lambda i: (layer, 0, 0, 0)),
                  mat(D_MODEL, D_MODEL), mat(B_GROUP_WIDTH, D_MODEL), mat(D_MODEL, D_MODEL),
                  vec(D_MODEL), vec(D_MODEL), mat(D_MODEL, LANES), vec(LANES)],
        out_specs=[row(D_MODEL), row(D_MODEL), row(LANES)],
        out_shape=[jax.ShapeDtypeStruct((t, D_MODEL), F32), jax.ShapeDtypeStruct((t, D_MODEL), BF16),
                   jax.ShapeDtypeStruct((t, LANES), F32)],
        scratch_shapes=[pltpu.VMEM((tm, D_MODEL), BF16), pltpu.VMEM((tm, B_GROUP_WIDTH), BF16),
                        pltpu.VMEM((len(B_PATTERNS), B_HEADS, tm, B_HEAD_DIM), F32),
                        pltpu.VMEM((len(B_PATTERNS), tm, LANES), F32)],
        compiler_params=pltpu.CompilerParams(
            dimension_semantics=("parallel",), vmem_limit_bytes=VMEM_LIMIT),
        name="mixer",
    )(z, z, z, z, o0, o1, o2, l0, l1, l2, x,
      p["ln_v_g"], p["ln_v_b"], p["w_s"], p["b_s"], p["w_pa"], p["w_pb"], p["w_o"],
      p["ln1_g"], p["ln1_b"], p["w_r"], p["b_r"])


def _router_kernel(logit_ref, tri_ref, route_ref, gate_ref, cnt_ref, carry):
    @pl.when(pl.program_id(0) == 0)
    def _():
        carry[...] = jnp.zeros_like(carry)

    l = logit_ref[...]
    tm = l.shape[0]
    lane = lax.broadcasted_iota(jnp.int32, (tm, LANES), 1)
    onehots, vals, ids = [], [], []
    for _ in range(TOP_K):
        mx = jnp.max(l, axis=-1, keepdims=True)
        idx = jnp.min(jnp.where(l == mx, lane, LANES), axis=-1, keepdims=True)
        oh = lane == idx
        l = jnp.where(oh, -jnp.inf, l)
        onehots.append(oh)
        vals.append(mx)
        ids.append(idx)
    es = [jnp.exp(v - vals[0]) for v in vals]
    den = es[0] + es[1] + es[2] + es[3]
    multi = jnp.zeros((tm, LANES), F32)
    for oh in onehots:
        multi = multi + oh.astype(F32)
    prefix = jnp.dot(tri_ref[...], multi.astype(BF16), preferred_element_type=F32) + carry[0:1, :]
    route = jnp.zeros((tm, LANES), jnp.int32)
    gate = jnp.zeros((tm, LANES), F32)
    for k in range(TOP_K):
        rank = jnp.sum(jnp.where(onehots[k], prefix, 0.0), axis=-1, keepdims=True).astype(jnp.int32)
        route = jnp.where(lane == k, ids[k], route)
        route = jnp.where(lane == TOP_K + k, rank, route)
        gate = jnp.where(lane == k, es[k] / den, gate)
    route_ref[...] = route
    gate_ref[...] = gate
    new_carry = carry[0:1, :] + jnp.sum(multi, axis=0, keepdims=True)
    carry[...] = jnp.broadcast_to(new_carry, carry.shape)
    cnt_ref[...] = jnp.broadcast_to(new_carry, cnt_ref.shape)


def _router(logits, tm=512):
    t = logits.shape[0]
    tri = jnp.asarray(np.tril(np.ones((tm, tm), np.float32), -1), dtype=BF16)
    return pl.pallas_call(
        _router_kernel,
        grid=(t // tm,),
        in_specs=[pl.BlockSpec((tm, LANES), lambda i: (i, 0)),
                  pl.BlockSpec((tm, tm), lambda i: (0, 0))],
        out_specs=[pl.BlockSpec((tm, LANES), lambda i: (i, 0)),
                   pl.BlockSpec((tm, LANES), lambda i: (i, 0)),
                   pl.BlockSpec((8, LANES), lambda i: (0, 0))],
        out_shape=[jax.ShapeDtypeStruct((t, LANES), jnp.int32), jax.ShapeDtypeStruct((t, LANES), F32),
                   jax.ShapeDtypeStruct((8, LANES), F32)],
        scratch_shapes=[pltpu.VMEM((8, LANES), F32)],
        compiler_params=pltpu.CompilerParams(dimension_semantics=("arbitrary",)),
        name="router",
    )(logits, tri)


def _ffn_kernel(blk_e_ref, n_used_ref, x_ref, wgu_ref, bgu_ref, wd_ref, bd_ref, y_ref):
    @pl.when(pl.program_id(0) < n_used_ref[0])
    def _():
        gu = jnp.dot(x_ref[...], wgu_ref[...], preferred_element_type=F32) + bgu_ref[...]
        gate = jnp.minimum(gu[:, :D_MODEL], SWIGLU_LIMIT)
        up = jnp.clip(gu[:, D_MODEL:], -SWIGLU_LIMIT, SWIGLU_LIMIT)
        h = (up + 1.0) * gate * jax.nn.sigmoid(SWIGLU_ALPHA * gate)
        y = jnp.dot(h.astype(BF16), wd_ref[...], preferred_element_type=F32) + bd_ref[...]
        y_ref[...] = y.astype(BF16)


def _ffn(xs, blk_e, n_used, p, layer, bm):
    n_pad = xs.shape[0]

    def rows(i, be, nu):
        return (jnp.minimum(i, nu[0] - 1), 0)

    grid_spec = pltpu.PrefetchScalarGridSpec(
        num_scalar_prefetch=2,
        grid=(n_pad // bm,),
        in_specs=[pl.BlockSpec((bm, D_MODEL), rows),
                  pl.BlockSpec((None, None, D_MODEL, 2 * D_MODEL), lambda i, be, nu: (layer, be[i], 0, 0)),
                  pl.BlockSpec((None, None, 1, 2 * D_MODEL), lambda i, be, nu: (layer, be[i], 0, 0)),
                  pl.BlockSpec((None, None, D_MODEL, D_MODEL), lambda i, be, nu: (layer, be[i], 0, 0)),
                  pl.BlockSpec((None, None, 1, D_MODEL), lambda i, be, nu: (layer, be[i], 0, 0))],
        out_specs=pl.BlockSpec((bm, D_MODEL), rows),
    )
    return pl.pallas_call(
        _ffn_kernel,
        grid_spec=grid_spec,
        out_shape=jax.ShapeDtypeStruct((n_pad, D_MODEL), BF16),
        compiler_params=pltpu.CompilerParams(
            dimension_semantics=("arbitrary",), vmem_limit_bytes=VMEM_LIMIT),
        name="ffn",
    )(blk_e, n_used, xs, p["w_gu"], p["b_gu"], p["w_down"], p["b_down"])


def _combine_kernel(y_ref, gate_ref, x_ref, g_ref, b_ref, x2_ref, x2b_ref, *, dn_alpha):
    acc = dn_alpha * x_ref[...]
    for k in range(TOP_K):
        acc = acc + gate_ref[:, k:k + 1] * y_ref[k].astype(F32)
    x2 = _layer_norm(acc, g_ref[...], b_ref[...])
    x2_ref[...] = x2
    x2b_ref[...] = x2.astype(BF16)


def _combine(yg, gates, x1, p, layer, dn_alpha, tm=512):
    t = x1.shape[0]
    return pl.pallas_call(
        functools.partial(_combine_kernel, dn_alpha=np.float32(dn_alpha)),
        grid=(t // tm,),
        in_specs=[pl.BlockSpec((TOP_K, tm, D_MODEL), lambda i: (0, i, 0)),
                  pl.BlockSpec((tm, LANES), lambda i: (i, 0)),
                  pl.BlockSpec((tm, D_MODEL), lambda i: (i, 0)),
                  pl.BlockSpec((None, 1, D_MODEL), lambda i: (layer, 0, 0)),
                  pl.BlockSpec((None, 1, D_MODEL), lambda i: (layer, 0, 0))],
        out_specs=[pl.BlockSpec((tm, D_MODEL), lambda i: (i, 0)),
                   pl.BlockSpec((tm, D_MODEL), lambda i: (i, 0))],
        out_shape=[jax.ShapeDtypeStruct((t, D_MODEL), F32), jax.ShapeDtypeStruct((t, D_MODEL), BF16)],
        compiler_params=pltpu.CompilerParams(
            dimension_semantics=("parallel",), vmem_limit_bytes=VMEM_LIMIT),
        name="combine",
    )(yg, gates, x1, p["ln2_g"], p["ln2_b"])


def _prepare_params(w_in, b_in, ln_v_g, ln_v_b, w_s, b_s, w_pa, w_pb, w_o, ln1_g, ln1_b,
                    w_r, b_r, w_gu, b_gu, w_down, b_down, ln2_g, ln2_b):
    depth = w_in.shape[0]
    n_qkv = 3 * len(B_PATTERNS) * B_GROUP_WIDTH
    qkv0, qkv1 = 2 * D_MODEL, 2 * D_MODEL + n_qkv

    def permute_cols(a):
        return jnp.concatenate([a[..., :qkv0], a[..., qkv1:], a[..., qkv0:qkv1]], axis=-1)

    def vec(a):
        return a.reshape(depth, 1, a.shape[-1]).astype(F32)

    w_r_pad = jnp.pad(w_r, ((0, 0), (0, 0), (0, LANES - N_EXPERTS))).astype(BF16)
    b_r_pad = jnp.pad(b_r.astype(F32), ((0, 0), (0, LANES - N_EXPERTS)), constant_values=NEG_INF)
    return dict(
        w_in=permute_cols(w_in).astype(BF16), b_in=vec(permute_cols(b_in)),
        ln_v_g=vec(ln_v_g), ln_v_b=vec(ln_v_b),
        w_s=w_s.astype(BF16),
        b_s=jnp.broadcast_to(b_s.astype(F32)[..., None], b_s.shape + (CHUNK,)),
        w_pa=w_pa.astype(BF16), w_pb=w_pb.astype(BF16), w_o=w_o.astype(BF16),
        ln1_g=vec(ln1_g), ln1_b=vec(ln1_b),
        w_r=w_r_pad, b_r=vec(b_r_pad),
        w_gu=w_gu.astype(BF16), b_gu=b_gu.astype(F32)[:, :, None, :],
        w_down=w_down.astype(BF16), b_down=b_down.astype(F32)[:, :, None, :],
        ln2_g=vec(ln2_g), ln2_b=vec(ln2_b),
    )


def _trunk(x, params, bm=512):
    n_seq, seq, _ = x.shape
    t = n_seq * seq
    depth = params["w_in"].shape[0]
    dn_alpha = (2.0 * depth) ** 0.25
    n_pad = t * TOP_K + N_EXPERTS * bm
    n_blk = n_pad // bm
    x = x.reshape(t, D_MODEL)
    xb = x.astype(BF16)
    tok = jnp.repeat(jnp.arange(t, dtype=jnp.int32), TOP_K)
    for layer in range(depth):
        z = _inproj_main(xb, params["w_in"], params["b_in"], layer)
        attn = [_attention(_inproj_qkv(xb, params["w_in"], params["b_in"], layer, g, n_seq, seq), g)
                for g in range(len(B_PATTERNS))]
        x1, x1b, logits = _mixer(z, attn, x, params, layer, dn_alpha, seq)
        route, gates, cnt = _router(logits)
        counts = cnt[0, :N_EXPERTS].astype(jnp.int32)
        padded = (counts + bm - 1) // bm * bm
        pend = jnp.cumsum(padded)
        pstart = pend - padded
        eid = route[:, :TOP_K]
        slot = pstart[eid] + route[:, TOP_K:2 * TOP_K]
        blk_e = jnp.minimum(
            jnp.searchsorted(pend, jnp.arange(n_blk, dtype=jnp.int32) * bm, side="right"),
            N_EXPERTS - 1).astype(jnp.int32)
        n_used = (pend[-1:] // bm).astype(jnp.int32)
        tok_of_slot = jnp.zeros((n_pad,), jnp.int32).at[slot.reshape(-1)].set(tok)
        xs = x1b[tok_of_slot]
        ys = _ffn(xs, blk_e, n_used, params, layer, bm)
        yg = ys[slot.T]
        x, xb = _combine(yg, gates, x1, params, layer, dn_alpha)
    return x.reshape(n_seq, seq, D_MODEL)


def kernel(x_prompt, x_sample, w_in, b_in, ln_v_g, ln_v_b, w_s, b_s, w_pa, w_pb, w_o, ln1_g, ln1_b,
           w_r, b_r, w_gu, b_gu, w_down, b_down, ln2_g, ln2_b):
    params = _prepare_params(w_in, b_in, ln_v_g, ln_v_b, w_s, b_s, w_pa, w_pb, w_o, ln1_g, ln1_b,
                             w_r, b_r, w_gu, b_gu, w_down, b_down, ln2_g, ln2_b)
    n_prompt = x_prompt.shape[0]
    y = _trunk(jnp.concatenate([x_prompt, x_sample], axis=0), params)
    return (y[:n_prompt], y[n_prompt:])
```

```python
import functools

import numpy as np
import jax
import jax.numpy as jnp
from jax import lax
from jax.experimental import pallas as pl
from jax.experimental.pallas import tpu as pltpu

D_MODEL = 1024
CHUNK = 128
A_GROUPS = 8
B_PATTERNS = ((128, 1), (512, 4), (2048, 16))
B_HEADS = 4
B_HEAD_DIM = 128
B_GROUP_WIDTH = B_HEADS * B_HEAD_DIM
N_ATTN_HEADS = len(B_PATTERNS) * B_HEADS
N_SIDE = 64
D_IN = 4 * D_MODEL + 3 * len(B_PATTERNS) * B_GROUP_WIDTH
N_EXPERTS = 32
TOP_K = 4
SWIGLU_LIMIT = 7.0
SWIGLU_ALPHA = 1.702
LN_EPS = 1e-5
NEG_INF = -1e30
LANES = 128

MAIN_COLS = 4 * D_MODEL
QKV_COLS = 3 * B_GROUP_WIDTH
GELU_COLS = 2 * D_MODEL

VMEM_LIMIT = 56 * 1024 * 1024

BF16 = jnp.bfloat16
F32 = jnp.float32


def _gelu(x):
    return 0.5 * x * (1.0 + lax.erf(x * np.float32(np.sqrt(0.5))))


def _layer_norm(x, g, b):
    xc = x - jnp.mean(x, axis=-1, keepdims=True)
    var = jnp.mean(xc * xc, axis=-1, keepdims=True)
    return xc * lax.rsqrt(var + LN_EPS) * g + b


def _inproj_main_kernel(x_ref, w_ref, b_ref, z_ref, *, n_gelu_tiles):
    acc = jnp.dot(x_ref[...], w_ref[...], preferred_element_type=F32) + b_ref[...]
    j = pl.program_id(1)

    @pl.when(j < n_gelu_tiles)
    def _():
        z_ref[...] = _gelu(acc).astype(BF16)

    @pl.when(j >= n_gelu_tiles)
    def _():
        z_ref[...] = acc.astype(BF16)


def _inproj_main(xb, w_in, b_in, layer, tm=1024, tn=1024):
    t = xb.shape[0]
    return pl.pallas_call(
        functools.partial(_inproj_main_kernel, n_gelu_tiles=GELU_COLS // tn),
        grid=(t // tm, MAIN_COLS // tn),
        in_specs=[
            pl.BlockSpec((tm, D_MODEL), lambda i, j: (i, 0)),
            pl.BlockSpec((None, D_MODEL, tn), lambda i, j: (layer, 0, j)),
            pl.BlockSpec((None, 1, tn), lambda i, j: (layer, 0, j)),
        ],
        out_specs=pl.BlockSpec((tm, tn), lambda i, j: (i, j)),
        out_shape=jax.ShapeDtypeStruct((t, MAIN_COLS), BF16),
        compiler_params=pltpu.CompilerParams(
            dimension_semantics=("parallel", "arbitrary"), vmem_limit_bytes=VMEM_LIMIT),
        name="inproj_main",
    )(xb, w_in, b_in)


def _inproj_qkv_kernel(x_ref, w_ref, b_ref, o_ref, *scratch, dilation):
    acc = jnp.dot(x_ref[...], w_ref[...], preferred_element_type=F32) + b_ref[...]
    if dilation == 1:
        o_ref[0] = acc.astype(BF16)
    else:
        (scr,) = scratch
        rows = acc.shape[0] // dilation
        for c in range(acc.shape[1] // LANES):
            scr[c] = acc[:, c * LANES:(c + 1) * LANES]
        for r in range(dilation):
            for c in range(acc.shape[1] // LANES):
                o_ref[r, :, c * LANES:(c + 1) * LANES] = (
                    scr[c, pl.ds(r, rows, stride=dilation), :].astype(BF16))


def _inproj_qkv(xb, w_in, b_in, layer, group, n_seq, seq, tm=1024, tn=512):
    _, dilation = B_PATTERNS[group]
    tiles_per_seq = seq // tm
    col0 = (MAIN_COLS + group * QKV_COLS) // tn
    return pl.pallas_call(
        functools.partial(_inproj_qkv_kernel, dilation=dilation),
        grid=(n_seq * tiles_per_seq, QKV_COLS // tn),
        in_specs=[
            pl.BlockSpec((tm, D_MODEL), lambda i, j: (i, 0)),
            pl.BlockSpec((None, D_MODEL, tn), lambda i, j: (layer, 0, col0 + j)),
            pl.BlockSpec((None, 1, tn), lambda i, j: (layer, 0, col0 + j)),
        ],
        out_specs=pl.BlockSpec((None, dilation, tm // dilation, tn),
                               lambda i, j: (i // tiles_per_seq, 0, i % tiles_per_seq, j)),
        out_shape=jax.ShapeDtypeStruct((n_seq, dilation, seq // dilation, QKV_COLS), BF16),
        scratch_shapes=[] if dilation == 1 else [pltpu.VMEM((tn // LANES, tm, LANES), F32)],
        compiler_params=pltpu.CompilerParams(
            dimension_semantics=("parallel", "arbitrary"), vmem_limit_bytes=VMEM_LIMIT),
        name=f"inproj_qkv{group}",
    )(xb, w_in, b_in)


def _attn_kernel(q_ref, kp_ref, kc_ref, kn_ref, vp_ref, vc_ref, vn_ref, bias_ref, o_ref, lse_ref,
                 s_scr, p_scr, *, bq):
    jq = pl.program_id(2)
    nq = pl.num_programs(2)
    n_piece = bq // N_SIDE
    col = lax.broadcasted_iota(jnp.int32, (N_SIDE, 3 * N_SIDE), 1)
    lane = lax.broadcasted_iota(jnp.int32, (bq, LANES), 1)
    scale = np.float32(B_HEAD_DIM ** -0.5)

    def window(p, prev_ref, cur_ref, next_ref, c0):
        cols = slice(c0, c0 + B_HEAD_DIM)
        if n_piece == 1:
            return jnp.concatenate([prev_ref[:, cols], cur_ref[:, cols], next_ref[:, cols]], axis=0)
        if p == 0:
            return jnp.concatenate([prev_ref[:, cols], cur_ref[0:2 * N_SIDE, cols]], axis=0)
        if p == n_piece - 1:
            return jnp.concatenate([cur_ref[bq - 2 * N_SIDE:bq, cols], next_ref[:, cols]], axis=0)
        return cur_ref[(p - 1) * N_SIDE:(p + 2) * N_SIDE, cols]

    for h in range(B_HEADS):
        c0 = h * B_HEAD_DIM
        for p in range(n_piece):
            r0 = p * N_SIDE
            qh = q_ref[r0:r0 + N_SIDE, c0:c0 + B_HEAD_DIM]
            kh = window(p, kp_ref, kc_ref, kn_ref, c0)
            s = lax.dot_general(qh, kh, (((1,), (1,)), ((), ())), preferred_element_type=F32) * scale
            if p == 0:
                s = jnp.where(jnp.logical_and(col < N_SIDE, jq == 0), NEG_INF, s)
            if p == n_piece - 1:
                s = jnp.where(jnp.logical_and(col >= 2 * N_SIDE, jq == nq - 1), NEG_INF, s)
            s_scr[h, r0:r0 + N_SIDE, :] = s
    lse_tile = jnp.zeros((bq, LANES), F32)
    for h in range(B_HEADS):
        s = s_scr[h] + bias_ref[h]
        m = jnp.max(s, axis=-1, keepdims=True)
        e = jnp.exp(s - m)
        den = jnp.sum(e, axis=-1, keepdims=True)
        p_scr[h] = (e / den).astype(BF16)
        lse_tile = jnp.where(lane == h, m + jnp.log(den), lse_tile)
    lse_ref[...] = lse_tile
    for h in range(B_HEADS):
        c0 = h * B_HEAD_DIM
        for p in range(n_piece):
            r0 = p * N_SIDE
            vh = window(p, vp_ref, vc_ref, vn_ref, c0)
            o = jnp.dot(p_scr[h, r0:r0 + N_SIDE, :], vh, preferred_element_type=F32)
            o_ref[r0:r0 + N_SIDE, c0:c0 + B_HEAD_DIM] = o.astype(BF16)


def _attn_bias(group, n_piece):
    _, dilation = B_PATTERNS[group]
    slopes = np.array([2.0 ** (-8.0 * (group * B_HEADS + h + 1) / N_ATTN_HEADS) for h in range(B_HEADS)],
                      dtype=np.float32)
    a = np.arange(N_SIDE)[:, None]
    c = np.arange(3 * N_SIDE)[None, :]
    rel = c - N_SIDE - a
    alibi = (-(slopes[:, None, None] * (dilation * np.abs(rel))[None])).astype(np.float32)
    bias = np.where((np.abs(rel) <= N_SIDE)[None], alibi, np.float32(NEG_INF)).astype(np.float32)
    return np.tile(bias, (1, n_piece, 1))


def _attention(qkv, group, bq_max=512):
    n_seq, dilation, sub_len, _ = qkv.shape
    bq = min(bq_max, sub_len)
    nq = sub_len // bq
    halo_per_bq = bq // N_SIDE
    n_halo = sub_len // N_SIDE

    def cur(part):
        return pl.BlockSpec((None, None, bq, B_GROUP_WIDTH), lambda b, r, j: (b, r, j, part))

    def prev(part):
        return pl.BlockSpec((None, None, N_SIDE, B_GROUP_WIDTH),
                            lambda b, r, j: (b, r, jnp.maximum(j * halo_per_bq - 1, 0), part))

    def nxt(part):
        return pl.BlockSpec((None, None, N_SIDE, B_GROUP_WIDTH),
                            lambda b, r, j: (b, r, jnp.minimum((j + 1) * halo_per_bq, n_halo - 1), part))

    return pl.pallas_call(
        functools.partial(_attn_kernel, bq=bq),
        grid=(n_seq, dilation, nq),
        in_specs=[cur(0), prev(1), cur(1), nxt(1), prev(2), cur(2), nxt(2),
                  pl.BlockSpec((B_HEADS, bq, 3 * N_SIDE), lambda b, r, j: (0, 0, 0))],
        out_specs=[pl.BlockSpec((None, None, bq, B_GROUP_WIDTH), lambda b, r, j: (b, r, j, 0)),
                   pl.BlockSpec((None, None, bq, LANES), lambda b, r, j: (b, r, j, 0))],
        out_shape=[jax.ShapeDtypeStruct((n_seq, dilation, sub_len, B_GROUP_WIDTH), BF16),
                   jax.ShapeDtypeStruct((n_seq, dilation, sub_len, LANES), F32)],
        scratch_shapes=[pltpu.VMEM((B_HEADS, bq, 3 * N_SIDE), F32),
                        pltpu.VMEM((B_HEADS, bq, 3 * N_SIDE), BF16)],
        compiler_params=pltpu.CompilerParams(
            dimension_semantics=("parallel", "parallel", "arbitrary"), vmem_limit_bytes=VMEM_LIMIT),
        name=f"attn{group}",
    )(qkv, qkv, qkv, qkv, qkv, qkv, qkv, jnp.asarray(_attn_bias(group, bq // N_SIDE)))


def _mixer_kernel(u_ref, v_ref, ga_ref, gb_ref, o0_ref, o1_ref, o2_ref, l0_ref, l1_ref, l2_ref, x_ref,
                  lnvg_ref, lnvb_ref, ws_ref, bs_ref, wpa_ref, wpb_ref, wo_ref, ln1g_ref, ln1b_ref,
                  wr_ref, br_ref, x1_ref, x1b_ref, logit_ref, a_scr, b_scr, o_scr, l_scr, *, tm, dn_alpha):
    vn = _layer_norm(v_ref[...].astype(F32), lnvg_ref[...], lnvb_ref[...]).astype(BF16)
    for c in range(tm // CHUNK):
        r0 = c * CHUNK
        for g in range(A_GROUPS):
            c0 = g * CHUNK
            mixed = jnp.dot(ws_ref[g], vn[r0:r0 + CHUNK, c0:c0 + CHUNK], preferred_element_type=F32)
            mixed = mixed + bs_ref[g]
            a = u_ref[r0:r0 + CHUNK, c0:c0 + CHUNK].astype(F32) * mixed
            a_scr[r0:r0 + CHUNK, c0:c0 + CHUNK] = a.astype(BF16)
    for g, (o_ref, l_ref) in enumerate(((o0_ref, l0_ref), (o1_ref, l1_ref), (o2_ref, l2_ref))):
        dilation = o_ref.shape[0]
        rows = tm // dilation
        for r in range(dilation):
            idx = pl.ds(r, rows, stride=dilation) if dilation > 1 else slice(None)
            for h in range(B_HEADS):
                o_scr[g, h, idx, :] = o_ref[r, :, h * B_HEAD_DIM:(h + 1) * B_HEAD_DIM].astype(F32)
            l_scr[g, idx, :] = l_ref[r]
    n_groups = len(B_PATTERNS)
    for h in range(B_HEADS):
        c0 = h * B_HEAD_DIM
        ls = [l_scr[g, :, h:h + 1] for g in range(n_groups)]
        mx = jnp.maximum(jnp.maximum(ls[0], ls[1]), ls[2])
        es = [jnp.exp(l - mx) for l in ls]
        den = es[0] + es[1] + es[2]
        acc = jnp.zeros((tm, B_HEAD_DIM), F32)
        for g in range(n_groups):
            acc = acc + (es[g] / den) * o_scr[g, h]
        b_scr[:, c0:c0 + B_HEAD_DIM] = acc.astype(BF16)
    pa = jnp.dot(a_scr[...], wpa_ref[...], preferred_element_type=F32)
    pb = jnp.dot(b_scr[...], wpb_ref[...], preferred_element_type=F32)
    merged = jax.nn.sigmoid(ga_ref[...].astype(F32)) * pa + jax.nn.sigmoid(gb_ref[...].astype(F32)) * pb
    mix = jnp.dot(merged.astype(BF16), wo_ref[...], preferred_element_type=F32)
    x1 = _layer_norm(dn_alpha * x_ref[...] + mix, ln1g_ref[...], ln1b_ref[...])
    x1_ref[...] = x1
    x1b = x1.astype(BF16)
    x1b_ref[...] = x1b
    logit_ref[...] = jnp.dot(x1b, wr_ref[...], preferred_element_type=F32) + br_ref[...]


def _mixer(z, attn, x, p, layer, dn_alpha, seq, tm=512):
    t = x.shape[0]
    tiles_per_seq = seq // tm

    def zspec(cb):
        return pl.BlockSpec((tm, D_MODEL), lambda i: (i, cb))

    def row(width):
        return pl.BlockSpec((tm, width), lambda i: (i, 0))

    def split(a):
        _, dilation, _, width = a.shape
        return pl.BlockSpec((None, dilation, tm // dilation, width),
                            lambda i: (i // tiles_per_seq, 0, i % tiles_per_seq, 0))

    def vec(width):
        return pl.BlockSpec((None, 1, width), lambda i: (layer, 0, 0))

    def mat(k, n):
        return pl.BlockSpec((None, k, n), lambda i: (layer, 0, 0))

    (o0, l0), (o1, l1), (o2, l2) = attn
    return pl.pallas_call(
        functools.partial(_mixer_kernel, tm=tm, dn_alpha=np.float32(dn_alpha)),
        grid=(t // tm,),
        in_specs=[zspec(0), zspec(1), zspec(2), zspec(3),
                  split(o0), split(o1), split(o2), split(l0), split(l1), split(l2), row(D_MODEL),
                  vec(D_MODEL), vec(D_MODEL),
                  pl.BlockSpec((None, A_GROUPS, CHUNK, CHUNK), lambda i: (layer, 0, 0, 0)),
                  pl.BlockSpec((None, A_GROUPS, CHUNK, CHUNK), lambda i: (layer, 0, 0, 0)),
                  mat(D_MODEL, D_MODEL), mat(B_GROUP_WIDTH, D_MODEL), mat(D_MODEL, D_MODEL),
                  vec(D_MODEL), vec(D_MODEL), mat(D_MODEL, LANES), vec(LANES)],
        out_specs=[row(D_MODEL), row(D_MODEL), row(LANES)],
        out_shape=[jax.ShapeDtypeStruct((t, D_MODEL), F32), jax.ShapeDtypeStruct((t, D_MODEL), BF16),
                   jax.ShapeDtypeStruct((t, LANES), F32)],
        scratch_shapes=[pltpu.VMEM((tm, D_MODEL), BF16), pltpu.VMEM((tm, B_GROUP_WIDTH), BF16),
                        pltpu.VMEM((len(B_PATTERNS), B_HEADS, tm, B_HEAD_DIM), F32),
                        pltpu.VMEM((len(B_PATTERNS), tm, LANES), F32)],
        compiler_params=pltpu.CompilerParams(
            dimension_semantics=("parallel",), vmem_limit_bytes=VMEM_LIMIT),
        name="mixer",
    )(z, z, z, z, o0, o1, o2, l0, l1, l2, x,
      p["ln_v_g"], p["ln_v_b"], p["w_s"], p["b_s"], p["w_pa"], p["w_pb"], p["w_o"],
      p["ln1_g"], p["ln1_b"], p["w_r"], p["b_r"])


TS = 512
CHUNK_ROWS = 16
REGION = TS * TOP_K + N_EXPERTS * CHUNK_ROWS
P_ROWS = 256


def _chunks_per_expert(cnt):
    return jnp.floor((cnt + (CHUNK_ROWS - 1)) * (1.0 / CHUNK_ROWS))


def _dispatch_kernel(logit_ref, x_ref, tri_ref, upper_ref, xs_ref, pos_ref, gate_ref, cnt_ref, p_scr):
    l = logit_ref[...]
    lane = lax.broadcasted_iota(jnp.int32, (TS, LANES), 1)
    onehots, vals = [], []
    for _ in range(TOP_K):
        mx = jnp.max(l, axis=-1, keepdims=True)
        idx = jnp.min(jnp.where(l == mx, lane, LANES), axis=-1, keepdims=True)
        oh = lane == idx
        l = jnp.where(oh, -jnp.inf, l)
        onehots.append(oh)
        vals.append(mx)
    es = [jnp.exp(v - vals[0]) for v in vals]
    den = es[0] + es[1] + es[2] + es[3]
    multi = jnp.zeros((TS, LANES), F32)
    for oh in onehots:
        multi = multi + oh.astype(F32)
    cnt = jnp.sum(multi, axis=0, keepdims=True)
    n_chunks = jnp.broadcast_to(_chunks_per_expert(cnt), (8, LANES)).astype(BF16)
    run_off = CHUNK_ROWS * jnp.dot(n_chunks, upper_ref[...], preferred_element_type=F32)[0:1, :]
    rank = jnp.dot(tri_ref[...], multi.astype(BF16), preferred_element_type=F32)
    pos = jnp.full((TS, LANES), -1.0, F32)
    gate = jnp.zeros((TS, LANES), F32)
    for k in range(TOP_K):
        pos_k = jnp.sum(jnp.where(onehots[k], rank + run_off, 0.0), axis=-1, keepdims=True)
        pos = jnp.where(lane == k, pos_k, pos)
        gate = jnp.where(lane == k, es[k] / den, gate)
    pos_ref[...] = pos.astype(jnp.int32)
    gate_ref[...] = gate
    cnt_ref[...] = jnp.broadcast_to(cnt, cnt_ref.shape).astype(jnp.int32)
    pos_t = jnp.transpose(pos)[0:8, :].astype(jnp.int32)
    for c in range(REGION // P_ROWS):
        row = lax.broadcasted_iota(jnp.int32, (P_ROWS, TS), 0) + c * P_ROWS
        hit = row == pos_t[0:1, :]
        for k in range(1, TOP_K):
            hit = jnp.logical_or(hit, row == pos_t[k:k + 1, :])
        p_scr[c * P_ROWS:(c + 1) * P_ROWS, :] = jnp.where(hit, 1.0, 0.0).astype(BF16)
    xs = jnp.dot(p_scr[...], x_ref[...], preferred_element_type=F32).astype(BF16)
    xs_ref[...] = jnp.where(pl.program_id(0) < pl.num_programs(0) - 1, xs, jnp.zeros((), BF16))


def _dispatch(logits, x1b):
    t = logits.shape[0]
    n_tiles = t // TS
    tri = jnp.asarray(np.tril(np.ones((TS, TS), np.float32), -1), dtype=BF16)
    upper = jnp.asarray(np.triu(np.ones((LANES, LANES), np.float32), 1), dtype=BF16)

    def tile(i):
        return jnp.minimum(i, n_tiles - 1)

    return pl.pallas_call(
        _dispatch_kernel,
        grid=(n_tiles + 1,),
        in_specs=[pl.BlockSpec((TS, LANES), lambda i: (tile(i), 0)),
                  pl.BlockSpec((TS, D_MODEL), lambda i: (tile(i), 0)),
                  pl.BlockSpec((TS, TS), lambda i: (0, 0)),
                  pl.BlockSpec((LANES, LANES), lambda i: (0, 0))],
        out_specs=[pl.BlockSpec((REGION, D_MODEL), lambda i: (i, 0)),
                   pl.BlockSpec((TS, LANES), lambda i: (tile(i), 0)),
                   pl.BlockSpec((TS, LANES), lambda i: (tile(i), 0)),
                   pl.BlockSpec((None, 8, LANES), lambda i: (tile(i), 0, 0))],
        out_shape=[jax.ShapeDtypeStruct(((n_tiles + 1) * REGION, D_MODEL), BF16),
                   jax.ShapeDtypeStruct((t, LANES), jnp.int32), jax.ShapeDtypeStruct((t, LANES), F32),
                   jax.ShapeDtypeStruct((n_tiles, 8, LANES), jnp.int32)],
        scratch_shapes=[pltpu.VMEM((REGION, TS), BF16)],
        compiler_params=pltpu.CompilerParams(
            dimension_semantics=("arbitrary",), vmem_limit_bytes=VMEM_LIMIT),
        name="dispatch",
    )(logits, x1b, tri, upper)


def _chunk_plan(cnt, bm):
    assert REGION >= 3 * bm
    n_tiles = cnt.shape[0]
    ch = bm // CHUNK_ROWS
    n_rows_max = n_tiles * REGION + N_EXPERTS * (bm - CHUNK_ROWS)
    n_blk = -(-n_rows_max // bm)
    run_chunks = (cnt + CHUNK_ROWS - 1) // CHUNK_ROWS
    run_off = (jnp.cumsum(run_chunks, axis=1) - run_chunks) * CHUNK_ROWS
    rows_e = jnp.sum(run_chunks, axis=0) * CHUNK_ROWS
    padded_e = (rows_e + bm - 1) // bm * bm
    pend = jnp.cumsum(padded_e)
    pstart = pend - padded_e
    runs_t = run_chunks.T
    run_start = (pstart[:, None] // CHUNK_ROWS + jnp.cumsum(runs_t, axis=1) - runs_t).reshape(-1)
    run_len = runs_t.reshape(-1)
    run_row = (jnp.arange(n_tiles, dtype=jnp.int32)[None, :] * REGION + run_off.T).reshape(-1)
    c = jnp.arange(n_blk * ch, dtype=jnp.int32)
    run = jnp.sum((run_start[None, :] <= c[:, None]).astype(jnp.int32), axis=1) - 1
    j = c - run_start[run]
    valid = j < run_len[run]
    row = run_row[run] + j * CHUNK_ROWS
    spare = n_tiles * REGION + (c % ch) * CHUNK_ROWS
    src = jnp.where(valid, row, spare + 2 * bm).astype(jnp.int32).reshape(n_blk, 1, ch)
    dst = jnp.where(valid, row, spare + ((c // ch) % 2) * bm).astype(jnp.int32).reshape(n_blk, 1, ch)
    blk_e = jnp.minimum(
        jnp.sum((pend[None, :] <= (jnp.arange(n_blk, dtype=jnp.int32) * bm)[:, None]).astype(jnp.int32), axis=1),
        N_EXPERTS - 1).astype(jnp.int32)
    n_used = (pend[-1:] // bm).astype(jnp.int32)
    return src, dst, blk_e, n_used


def _ffn_kernel(blk_e_ref, n_used_ref, src0_ref, srcn_ref, dst_ref, xs_hbm, wgu_ref, bgu_ref, wd_ref, bd_ref,
                ys_hbm, xbuf, ybuf, wgu_b, wd_b, in_sem, out_sem, *, bm):
    i = pl.program_id(0)
    n_used = n_used_ref[0]
    slot = i % 2
    ch = bm // CHUNK_ROWS

    def gather(tbl_ref, s):
        for j in range(ch):
            row = pl.multiple_of(tbl_ref[0, j], CHUNK_ROWS)
            pltpu.make_async_copy(xs_hbm.at[pl.ds(row, CHUNK_ROWS)],
                                  xbuf.at[s, pl.ds(j * CHUNK_ROWS, CHUNK_ROWS)], in_sem.at[s]).start()

    def wait_in(s):
        pltpu.make_async_copy(xs_hbm.at[pl.ds(0, bm)], xbuf.at[s], in_sem.at[s]).wait()

    def wait_out(s):
        pltpu.make_async_copy(ybuf.at[s], ys_hbm.at[pl.ds(0, bm)], out_sem.at[s]).wait()

    @pl.when(jnp.logical_and(i == 0, n_used > 0))
    def _():
        gather(src0_ref, 0)

    @pl.when(i + 1 < n_used)
    def _():
        gather(srcn_ref, 1 - slot)

    @pl.when(i < n_used)
    def _():
        e_changed = jnp.logical_or(i == 0, blk_e_ref[i] != blk_e_ref[jnp.maximum(i - 1, 0)])

        @pl.when(e_changed)
        def _():
            wgu_b[...] = wgu_ref[...].astype(BF16)
            wd_b[...] = wd_ref[...].astype(BF16)

        wait_in(slot)

        @pl.when(i >= 2)
        def _():
            wait_out(slot)

        gu = jnp.dot(xbuf[slot], wgu_b[...], preferred_element_type=F32) + bgu_ref[...]
        gate = jnp.minimum(gu[:, :D_MODEL], SWIGLU_LIMIT)
        up = jnp.clip(gu[:, D_MODEL:], -SWIGLU_LIMIT, SWIGLU_LIMIT)
        h = (up + 1.0) * gate * jax.nn.sigmoid(SWIGLU_ALPHA * gate)
        y = jnp.dot(h.astype(BF16), wd_b[...], preferred_element_type=F32) + bd_ref[...]
        ybuf[slot] = y.astype(BF16)
        for j in range(ch):
            row = pl.multiple_of(dst_ref[0, j], CHUNK_ROWS)
            pltpu.make_async_copy(ybuf.at[slot, pl.ds(j * CHUNK_ROWS, CHUNK_ROWS)],
                                  ys_hbm.at[pl.ds(row, CHUNK_ROWS)], out_sem.at[slot]).start()

        @pl.when(i == n_used - 1)
        def _():
            @pl.when(i >= 1)
            def _():
                wait_out(1 - slot)

            wait_out(slot)


def _ffn(xs, src, dst, blk_e, n_used, p, layer, bm):
    n_blk, _, ch = src.shape
    smem = functools.partial(pl.BlockSpec, memory_space=pltpu.SMEM)
    grid_spec = pltpu.PrefetchScalarGridSpec(
        num_scalar_prefetch=2,
        grid=(n_blk,),
        in_specs=[smem((None, 1, ch), lambda i, be, nu: (0, 0, 0)),
                  smem((None, 1, ch), lambda i, be, nu: (jnp.minimum(i + 1, n_blk - 1), 0, 0)),
                  smem((None, 1, ch), lambda i, be, nu: (i, 0, 0)),
                  pl.BlockSpec(memory_space=pl.ANY),
                  pl.BlockSpec((None, None, D_MODEL, 2 * D_MODEL), lambda i, be, nu: (layer, be[i], 0, 0)),
                  pl.BlockSpec((None, None, 1, 2 * D_MODEL), lambda i, be, nu: (layer, be[i], 0, 0)),
                  pl.BlockSpec((None, None, D_MODEL, D_MODEL), lambda i, be, nu: (layer, be[i], 0, 0)),
                  pl.BlockSpec((None, None, 1, D_MODEL), lambda i, be, nu: (layer, be[i], 0, 0))],
        out_specs=pl.BlockSpec(memory_space=pl.ANY),
        scratch_shapes=[pltpu.VMEM((2, bm, D_MODEL), BF16), pltpu.VMEM((2, bm, D_MODEL), BF16),
                        pltpu.VMEM((D_MODEL, 2 * D_MODEL), BF16), pltpu.VMEM((D_MODEL, D_MODEL), BF16),
                        pltpu.SemaphoreType.DMA((2,)), pltpu.SemaphoreType.DMA((2,))],
    )
    return pl.pallas_call(
        functools.partial(_ffn_kernel, bm=bm),
        grid_spec=grid_spec,
        out_shape=jax.ShapeDtypeStruct(xs.shape, BF16),
        input_output_aliases={5: 0},
        compiler_params=pltpu.CompilerParams(
            dimension_semantics=("arbitrary",), vmem_limit_bytes=VMEM_LIMIT),
        name="ffn",
    )(blk_e, n_used, src, src, dst, xs, p["w_gu"], p["b_gu"], p["w_down"], p["b_down"])


G_COLS = 512


def _combine_kernel(ys_ref, pos_ref, gate_ref, x_ref, g_ref, b_ref, x2_ref, x2b_ref, g_scr, *, dn_alpha):
    for c in range(REGION // G_COLS):
        col = lax.broadcasted_iota(jnp.int32, (TS, G_COLS), 1) + c * G_COLS
        g = jnp.zeros((TS, G_COLS), F32)
        for k in range(TOP_K):
            g = g + jnp.where(col == pos_ref[:, k:k + 1], gate_ref[:, k:k + 1], 0.0)
        g_scr[:, c * G_COLS:(c + 1) * G_COLS] = g.astype(BF16)
    y = jnp.dot(g_scr[...], ys_ref[...], preferred_element_type=F32)
    x2 = _layer_norm(dn_alpha * x_ref[...] + y, g_ref[...], b_ref[...])
    x2_ref[...] = x2
    x2b_ref[...] = x2.astype(BF16)


def _combine(ys, pos, gates, x1, p, layer, dn_alpha):
    t = x1.shape[0]
    return pl.pallas_call(
        functools.partial(_combine_kernel, dn_alpha=np.float32(dn_alpha)),
        grid=(t // TS,),
        in_specs=[pl.BlockSpec((REGION, D_MODEL), lambda i: (i, 0)),
                  pl.BlockSpec((TS, LANES), lambda i: (i, 0)),
                  pl.BlockSpec((TS, LANES), lambda i: (i, 0)),
                  pl.BlockSpec((TS, D_MODEL), lambda i: (i, 0)),
                  pl.BlockSpec((None, 1, D_MODEL), lambda i: (layer, 0, 0)),
                  pl.BlockSpec((None, 1, D_MODEL), lambda i: (layer, 0, 0))],
        out_specs=[pl.BlockSpec((TS, D_MODEL), lambda i: (i, 0)),
                   pl.BlockSpec((TS, D_MODEL), lambda i: (i, 0))],
        out_shape=[jax.ShapeDtypeStruct((t, D_MODEL), F32), jax.ShapeDtypeStruct((t, D_MODEL), BF16)],
        scratch_shapes=[pltpu.VMEM((TS, REGION), BF16)],
        compiler_params=pltpu.CompilerParams(
            dimension_semantics=("parallel",), vmem_limit_bytes=VMEM_LIMIT),
        name="combine",
    )(ys, pos, gates, x1, p["ln2_g"], p["ln2_b"])


def _prepare_params(w_in, b_in, ln_v_g, ln_v_b, w_s, b_s, w_pa, w_pb, w_o, ln1_g, ln1_b,
                    w_r, b_r, w_gu, b_gu, w_down, b_down, ln2_g, ln2_b):
    depth = w_in.shape[0]
    n_qkv = 3 * len(B_PATTERNS) * B_GROUP_WIDTH
    qkv0, qkv1 = 2 * D_MODEL, 2 * D_MODEL + n_qkv

    def permute_cols(a):
        return jnp.concatenate([a[..., :qkv0], a[..., qkv1:], a[..., qkv0:qkv1]], axis=-1)

    def vec(a):
        return a.reshape(depth, 1, a.shape[-1]).astype(F32)

    w_r_pad = jnp.pad(w_r, ((0, 0), (0, 0), (0, LANES - N_EXPERTS))).astype(BF16)
    b_r_pad = jnp.pad(b_r.astype(F32), ((0, 0), (0, LANES - N_EXPERTS)), constant_values=NEG_INF)
    return dict(
        w_in=permute_cols(w_in).astype(BF16), b_in=vec(permute_cols(b_in)),
        ln_v_g=vec(ln_v_g), ln_v_b=vec(ln_v_b),
        w_s=w_s.astype(BF16),
        b_s=jnp.broadcast_to(b_s.astype(F32)[..., None], b_s.shape + (CHUNK,)),
        w_pa=w_pa.astype(BF16), w_pb=w_pb.astype(BF16), w_o=w_o.astype(BF16),
        ln1_g=vec(ln1_g), ln1_b=vec(ln1_b),
        w_r=w_r_pad, b_r=vec(b_r_pad),
        w_gu=w_gu, b_gu=b_gu.astype(F32)[:, :, None, :],
        w_down=w_down, b_down=b_down.astype(F32)[:, :, None, :],
        ln2_g=vec(ln2_g), ln2_b=vec(ln2_b),
    )


def _trunk(x, params, bm=512):
    n_seq, seq, _ = x.shape
    t = n_seq * seq
    depth = params["w_in"].shape[0]
    dn_alpha = (2.0 * depth) ** 0.25
    x = x.reshape(t, D_MODEL)
    xb = x.astype(BF16)
    for layer in range(depth):
        z = _inproj_main(xb, params["w_in"], params["b_in"], layer)
        attn = [_attention(_inproj_qkv(xb, params["w_in"], params["b_in"], layer, g, n_seq, seq), g)
                for g in range(len(B_PATTERNS))]
        x1, x1b, logits = _mixer(z, attn, x, params, layer, dn_alpha, seq)
        xs, pos, gates, cnt = _dispatch(logits, x1b)
        src, dst, blk_e, n_used = _chunk_plan(cnt[:, 0, :N_EXPERTS], bm)
        ys = _ffn(xs, src, dst, blk_e, n_used, params, layer, bm)
        x, xb = _combine(ys, pos, gates, x1, params, layer, dn_alpha)
    return x.reshape(n_seq, seq, D_MODEL)


def kernel(x_prompt, x_sample, w_in, b_in, ln_v_g, ln_v_b, w_s, b_s, w_pa, w_pb, w_o, ln1_g, ln1_b,
           w_r, b_r, w_gu, b_gu, w_down, b_down, ln2_g, ln2_b):
    params = _prepare_params(w_in, b_in, ln_v_g, ln_v_b, w_s, b_s, w_pa, w_pb, w_o, ln1_g, ln1_b,
                             w_r, b_r, w_gu, b_gu, w_down, b_down, ln2_g, ln2_b)
    n_prompt = x_prompt.shape[0]
    y = _trunk(jnp.concatenate([x_prompt, x_sample], axis=0), params)
    return (y[:n_prompt], y[n_prompt:])
```

```python
import functools

import numpy as np
import jax
import jax.numpy as jnp
from jax import lax
from jax.experimental import pallas as pl
from jax.experimental.pallas import tpu as pltpu

D_MODEL = 1024
CHUNK = 128
A_GROUPS = 8
B_PATTERNS = ((128, 1), (512, 4), (2048, 16))
B_HEADS = 4
B_HEAD_DIM = 128
B_GROUP_WIDTH = B_HEADS * B_HEAD_DIM
N_ATTN_HEADS = len(B_PATTERNS) * B_HEADS
N_SIDE = 64
D_IN = 4 * D_MODEL + 3 * len(B_PATTERNS) * B_GROUP_WIDTH
N_EXPERTS = 32
TOP_K = 4
SWIGLU_LIMIT = 7.0
SWIGLU_ALPHA = 1.702
LN_EPS = 1e-5
NEG_INF = -1e30
LANES = 128

MAIN_COLS = 4 * D_MODEL
QKV_COLS = 3 * B_GROUP_WIDTH
MAIN_COL0 = -(-len(B_PATTERNS) * QKV_COLS // D_MODEL) * D_MODEL
GELU_COLS = 2 * D_MODEL

VMEM_LIMIT = 56 * 1024 * 1024

BF16 = jnp.bfloat16
F32 = jnp.float32


def _gelu(x):
    return 0.5 * x * (1.0 + lax.erf(x * np.float32(np.sqrt(0.5))))


def _layer_norm(x, g, b):
    xc = x - jnp.mean(x, axis=-1, keepdims=True)
    var = jnp.mean(xc * xc, axis=-1, keepdims=True)
    return xc * lax.rsqrt(var + LN_EPS) * g + b


def _inproj_main_kernel(x_ref, w_ref, b_ref, z_ref, *, n_gelu_tiles):
    acc = jnp.dot(x_ref[...], w_ref[...], preferred_element_type=F32) + b_ref[...]
    j = pl.program_id(1)

    @pl.when(j < n_gelu_tiles)
    def _():
        z_ref[...] = _gelu(acc).astype(BF16)

    @pl.when(j >= n_gelu_tiles)
    def _():
        z_ref[...] = acc.astype(BF16)


def _inproj_main(xb, w_in, b_in, layer, tm=1024, tn=1024):
    t = xb.shape[0]
    return pl.pallas_call(
        functools.partial(_inproj_main_kernel, n_gelu_tiles=GELU_COLS // tn),
        grid=(t // tm, MAIN_COLS // tn),
        in_specs=[
            pl.BlockSpec((tm, D_MODEL), lambda i, j: (i, 0)),
            pl.BlockSpec((None, D_MODEL, tn), lambda i, j: (layer, 0, MAIN_COL0 // tn + j)),
            pl.BlockSpec((None, 1, tn), lambda i, j: (layer, 0, MAIN_COL0 // tn + j)),
        ],
        out_specs=pl.BlockSpec((tm, tn), lambda i, j: (i, j)),
        out_shape=jax.ShapeDtypeStruct((t, MAIN_COLS), BF16),
        compiler_params=pltpu.CompilerParams(
            dimension_semantics=("parallel", "arbitrary"), vmem_limit_bytes=VMEM_LIMIT),
        name="inproj_main",
    )(xb, w_in, b_in)


def _inproj_qkv_kernel(x_ref, w_ref, b_ref, o_ref, *scratch, dilation):
    acc = jnp.dot(x_ref[...], w_ref[...], preferred_element_type=F32) + b_ref[...]
    if dilation == 1:
        o_ref[0] = acc.astype(BF16)
    else:
        (scr,) = scratch
        rows = acc.shape[0] // dilation
        for c in range(acc.shape[1] // LANES):
            scr[c] = acc[:, c * LANES:(c + 1) * LANES]
        for r in range(dilation):
            for c in range(acc.shape[1] // LANES):
                o_ref[r, :, c * LANES:(c + 1) * LANES] = (
                    scr[c, pl.ds(r, rows, stride=dilation), :].astype(BF16))


def _inproj_qkv(xb, w_in, b_in, layer, group, n_seq, seq, tm=1024, tn=QKV_COLS):
    _, dilation = B_PATTERNS[group]
    tiles_per_seq = seq // tm
    col0 = group * QKV_COLS // tn
    return pl.pallas_call(
        functools.partial(_inproj_qkv_kernel, dilation=dilation),
        grid=(n_seq * tiles_per_seq, QKV_COLS // tn),
        in_specs=[
            pl.BlockSpec((tm, D_MODEL), lambda i, j: (i, 0)),
            pl.BlockSpec((None, D_MODEL, tn), lambda i, j: (layer, 0, col0 + j)),
            pl.BlockSpec((None, 1, tn), lambda i, j: (layer, 0, col0 + j)),
        ],
        out_specs=pl.BlockSpec((None, dilation, tm // dilation, tn),
                               lambda i, j: (i // tiles_per_seq, 0, i % tiles_per_seq, j)),
        out_shape=jax.ShapeDtypeStruct((n_seq, dilation, seq // dilation, QKV_COLS), BF16),
        scratch_shapes=[] if dilation == 1 else [pltpu.VMEM((tn // LANES, tm, LANES), F32)],
        compiler_params=pltpu.CompilerParams(
            dimension_semantics=("parallel", "arbitrary"), vmem_limit_bytes=VMEM_LIMIT),
        name=f"inproj_qkv{group}",
    )(xb, w_in, b_in)


def _attn_kernel(q_ref, kp_ref, kc_ref, kn_ref, vp_ref, vc_ref, vn_ref, bias_ref, o_ref, lse_ref,
                 s_scr, p_scr, *, bq):
    jq = pl.program_id(2)
    nq = pl.num_programs(2)
    n_piece = bq // N_SIDE
    col = lax.broadcasted_iota(jnp.int32, (N_SIDE, 3 * N_SIDE), 1)
    lane = lax.broadcasted_iota(jnp.int32, (N_SIDE, LANES), 1)
    scale = np.float32(B_HEAD_DIM ** -0.5)

    def window(p, prev_ref, cur_ref, next_ref, c0):
        cols = slice(c0, c0 + B_HEAD_DIM)
        if n_piece == 1:
            return jnp.concatenate([prev_ref[:, cols], cur_ref[:, cols], next_ref[:, cols]], axis=0)
        if p == 0:
            return jnp.concatenate([prev_ref[:, cols], cur_ref[0:2 * N_SIDE, cols]], axis=0)
        if p == n_piece - 1:
            return jnp.concatenate([cur_ref[bq - 2 * N_SIDE:bq, cols], next_ref[:, cols]], axis=0)
        return cur_ref[(p - 1) * N_SIDE:(p + 2) * N_SIDE, cols]

    for h in range(B_HEADS):
        c0 = h * B_HEAD_DIM
        for p in range(n_piece):
            r0 = p * N_SIDE
            qh = q_ref[r0:r0 + N_SIDE, c0:c0 + B_HEAD_DIM]
            kh = window(p, kp_ref, kc_ref, kn_ref, c0)
            s = lax.dot_general(qh, kh, (((1,), (1,)), ((), ())), preferred_element_type=F32) * scale
            if p == 0:
                s = jnp.where(jnp.logical_and(col < N_SIDE, jq == 0), NEG_INF, s)
            if p == n_piece - 1:
                s = jnp.where(jnp.logical_and(col >= 2 * N_SIDE, jq == nq - 1), NEG_INF, s)
            s_scr[h, r0:r0 + N_SIDE, :] = s
    for p in range(n_piece):
        r0 = p * N_SIDE
        lse_tile = jnp.zeros((N_SIDE, LANES), F32)
        for h in range(B_HEADS):
            s = s_scr[h, r0:r0 + N_SIDE, :] + bias_ref[h]
            m = jnp.max(s, axis=-1, keepdims=True)
            e = jnp.exp(s - m)
            den = jnp.sum(e, axis=-1, keepdims=True)
            p_scr[h, r0:r0 + N_SIDE, :] = (e / den).astype(BF16)
            lse_tile = jnp.where(lane == h, m + jnp.log(den), lse_tile)
        lse_ref[r0:r0 + N_SIDE, :] = lse_tile
    for h in range(B_HEADS):
        c0 = h * B_HEAD_DIM
        for p in range(n_piece):
            r0 = p * N_SIDE
            vh = window(p, vp_ref, vc_ref, vn_ref, c0)
            o = jnp.dot(p_scr[h, r0:r0 + N_SIDE, :], vh, preferred_element_type=F32)
            o_ref[r0:r0 + N_SIDE, c0:c0 + B_HEAD_DIM] = o.astype(BF16)


def _attn_bias(group):
    _, dilation = B_PATTERNS[group]
    slopes = np.array([2.0 ** (-8.0 * (group * B_HEADS + h + 1) / N_ATTN_HEADS) for h in range(B_HEADS)],
                      dtype=np.float32)
    a = np.arange(N_SIDE)[:, None]
    c = np.arange(3 * N_SIDE)[None, :]
    rel = c - N_SIDE - a
    alibi = (-(slopes[:, None, None] * (dilation * np.abs(rel))[None])).astype(np.float32)
    return np.where((np.abs(rel) <= N_SIDE)[None], alibi, np.float32(NEG_INF)).astype(np.float32)


def _attention(qkv, group, bq_max=512):
    n_seq, dilation, sub_len, _ = qkv.shape
    bq = min(bq_max, sub_len)
    nq = sub_len // bq
    halo_per_bq = bq // N_SIDE
    n_halo = sub_len // N_SIDE

    def cur(part):
        return pl.BlockSpec((None, None, bq, B_GROUP_WIDTH), lambda b, r, j: (b, r, j, part))

    def prev(part):
        return pl.BlockSpec((None, None, N_SIDE, B_GROUP_WIDTH),
                            lambda b, r, j: (b, r, jnp.maximum(j * halo_per_bq - 1, 0), part))

    def nxt(part):
        return pl.BlockSpec((None, None, N_SIDE, B_GROUP_WIDTH),
                            lambda b, r, j: (b, r, jnp.minimum((j + 1) * halo_per_bq, n_halo - 1), part))

    return pl.pallas_call(
        functools.partial(_attn_kernel, bq=bq),
        grid=(n_seq, dilation, nq),
        in_specs=[cur(0), prev(1), cur(1), nxt(1), prev(2), cur(2), nxt(2),
                  pl.BlockSpec((B_HEADS, N_SIDE, 3 * N_SIDE), lambda b, r, j: (0, 0, 0))],
        out_specs=[pl.BlockSpec((None, None, bq, B_GROUP_WIDTH), lambda b, r, j: (b, r, j, 0)),
                   pl.BlockSpec((None, None, bq, LANES), lambda b, r, j: (b, r, j, 0))],
        out_shape=[jax.ShapeDtypeStruct((n_seq, dilation, sub_len, B_GROUP_WIDTH), BF16),
                   jax.ShapeDtypeStruct((n_seq, dilation, sub_len, LANES), F32)],
        scratch_shapes=[pltpu.VMEM((B_HEADS, bq, 3 * N_SIDE), F32),
                        pltpu.VMEM((B_HEADS, bq, 3 * N_SIDE), BF16)],
        compiler_params=pltpu.CompilerParams(
            dimension_semantics=("parallel", "parallel", "arbitrary"), vmem_limit_bytes=VMEM_LIMIT),
        name=f"attn{group}",
    )(qkv, qkv, qkv, qkv, qkv, qkv, qkv, jnp.asarray(_attn_bias(group)))


def _mixer_kernel(u_ref, v_ref, ga_ref, gb_ref, o0_ref, o1_ref, o2_ref, l0_ref, l1_ref, l2_ref, x_ref,
                  lnvg_ref, lnvb_ref, ws_ref, bs_ref, wpa_ref, wpb_ref, wo_ref, ln1g_ref, ln1b_ref,
                  wr_ref, br_ref, x1_ref, x1b_ref, logit_ref, a_scr, b_scr, o_scr, l_scr, *, tm, dn_alpha):
    vn = _layer_norm(v_ref[...].astype(F32), lnvg_ref[...], lnvb_ref[...]).astype(BF16)
    for c in range(tm // CHUNK):
        r0 = c * CHUNK
        for g in range(A_GROUPS):
            c0 = g * CHUNK
            mixed = jnp.dot(ws_ref[g], vn[r0:r0 + CHUNK, c0:c0 + CHUNK], preferred_element_type=F32)
            mixed = mixed + bs_ref[g]
            a = u_ref[r0:r0 + CHUNK, c0:c0 + CHUNK].astype(F32) * mixed
            a_scr[r0:r0 + CHUNK, c0:c0 + CHUNK] = a.astype(BF16)
    for g, (o_ref, l_ref) in enumerate(((o0_ref, l0_ref), (o1_ref, l1_ref), (o2_ref, l2_ref))):
        dilation = o_ref.shape[0]
        rows = tm // dilation
        for r in range(dilation):
            idx = pl.ds(r, rows, stride=dilation) if dilation > 1 else slice(None)
            for h in range(B_HEADS):
                o_scr[g, h, idx, :] = o_ref[r, :, h * B_HEAD_DIM:(h + 1) * B_HEAD_DIM].astype(F32)
            l_scr[g, idx, :] = l_ref[r]
    n_groups = len(B_PATTERNS)
    for h in range(B_HEADS):
        c0 = h * B_HEAD_DIM
        ls = [l_scr[g, :, h:h + 1] for g in range(n_groups)]
        mx = jnp.maximum(jnp.maximum(ls[0], ls[1]), ls[2])
        es = [jnp.exp(l - mx) for l in ls]
        den = es[0] + es[1] + es[2]
        acc = jnp.zeros((tm, B_HEAD_DIM), F32)
        for g in range(n_groups):
            acc = acc + (es[g] / den) * o_scr[g, h]
        b_scr[:, c0:c0 + B_HEAD_DIM] = acc.astype(BF16)
    pa = jnp.dot(a_scr[...], wpa_ref[...], preferred_element_type=F32)
    pb = jnp.dot(b_scr[...], wpb_ref[...], preferred_element_type=F32)
    merged = jax.nn.sigmoid(ga_ref[...].astype(F32)) * pa + jax.nn.sigmoid(gb_ref[...].astype(F32)) * pb
    mix = jnp.dot(merged.astype(BF16), wo_ref[...], preferred_element_type=F32)
    x1 = _layer_norm(dn_alpha * x_ref[...] + mix, ln1g_ref[...], ln1b_ref[...])
    x1_ref[...] = x1
    x1b = x1.astype(BF16)
    x1b_ref[...] = x1b
    logit_ref[...] = jnp.dot(x1b, wr_ref[...], preferred_element_type=F32) + br_ref[...]


def _mixer(z, attn, x, p, layer, dn_alpha, seq, tm=512):
    t = x.shape[0]
    tiles_per_seq = seq // tm

    def zspec(cb):
        return pl.BlockSpec((tm, D_MODEL), lambda i: (i, cb))

    def row(width):
        return pl.BlockSpec((tm, width), lambda i: (i, 0))

    def split(a):
        _, dilation, _, width = a.shape
        return pl.BlockSpec((None, dilation, tm // dilation, width),
                            lambda i: (i // tiles_per_seq, 0, i % tiles_per_seq, 0))

    def vec(width):
        return pl.BlockSpec((None, 1, width), lambda i: (layer, 0, 0))

    def mat(k, n):
        return pl.BlockSpec((None, k, n), lambda i: (layer, 0, 0))

    (o0, l0), (o1, l1), (o2, l2) = attn
    return pl.pallas_call(
        functools.partial(_mixer_kernel, tm=tm, dn_alpha=np.float32(dn_alpha)),
        grid=(t // tm,),
        in_specs=[zspec(0), zspec(1), zspec(2), zspec(3),
                  split(o0), split(o1), split(o2), split(l0), split(l1), split(l2), row(D_MODEL),
                  vec(D_MODEL), vec(D_MODEL),
                  pl.BlockSpec((None, A_GROUPS, CHUNK, CHUNK), lambda i: (layer, 0, 0, 0)),
                  pl.BlockSpec((None, A_GROUPS, CHUNK, CHUNK), lambda i: (layer, 0, 0, 0)),
                  mat(D_MODEL, D_MODEL), mat(B_GROUP_WIDTH, D_MODEL), mat(D_MODEL, D_MODEL),
                  vec(D_MODEL), vec(D_MODEL), mat(D_MODEL, LANES), vec(LANES)],
        out_specs=[row(D_MODEL), row(D_MODEL), row(LANES)],
        out_shape=[jax.ShapeDtypeStruct((t, D_MODEL), F32), jax.ShapeDtypeStruct((t, D_MODEL), BF16),
                   jax.ShapeDtypeStruct((t, LANES), F32)],
        scratch_shapes=[pltpu.VMEM((tm, D_MODEL), BF16), pltpu.VMEM((tm, B_GROUP_WIDTH), BF16),
                        pltpu.VMEM((len(B_PATTERNS), B_HEADS, tm, B_HEAD_DIM), F32),
                        pltpu.VMEM((len(B_PATTERNS), tm, LANES), F32)],
        compiler_params=pltpu.CompilerParams(
            dimension_semantics=("parallel",), vmem_limit_bytes=VMEM_LIMIT),
        name="mixer",
    )(z, z, z, z, o0, o1, o2, l0, l1, l2, x,
      p["ln_v_g"], p["ln_v_b"], p["w_s"], p["b_s"], p["w_pa"], p["w_pb"], p["w_o"],
      p["ln1_g"], p["ln1_b"], p["w_r"], p["b_r"])


TS = 512
CHUNK_ROWS = 16
REGION = TS * TOP_K + N_EXPERTS * CHUNK_ROWS
P_ROWS = 256


def _chunks_per_expert(cnt):
    return jnp.floor((cnt + (CHUNK_ROWS - 1)) * (1.0 / CHUNK_ROWS))


def _dispatch_kernel(logit_ref, x_ref, tri_ref, upper_ref, xs_ref, pos_ref, gate_ref, cnt_ref, p_scr):
    l = logit_ref[...]
    lane = lax.broadcasted_iota(jnp.int32, (TS, LANES), 1)
    onehots, vals = [], []
    for _ in range(TOP_K):
        mx = jnp.max(l, axis=-1, keepdims=True)
        idx = jnp.min(jnp.where(l == mx, lane, LANES), axis=-1, keepdims=True)
        oh = lane == idx
        l = jnp.where(oh, -jnp.inf, l)
        onehots.append(oh)
        vals.append(mx)
    es = [jnp.exp(v - vals[0]) for v in vals]
    den = es[0] + es[1] + es[2] + es[3]
    multi = jnp.zeros((TS, LANES), F32)
    for oh in onehots:
        multi = multi + oh.astype(F32)
    cnt = jnp.sum(multi, axis=0, keepdims=True)
    n_chunks = jnp.broadcast_to(_chunks_per_expert(cnt), (8, LANES)).astype(BF16)
    run_off = CHUNK_ROWS * jnp.dot(n_chunks, upper_ref[...], preferred_element_type=F32)[0:1, :]
    rank = jnp.dot(tri_ref[...], multi.astype(BF16), preferred_element_type=F32)
    pos = jnp.full((TS, LANES), -1.0, F32)
    gate = jnp.zeros((TS, LANES), F32)
    for k in range(TOP_K):
        pos_k = jnp.sum(jnp.where(onehots[k], rank + run_off, 0.0), axis=-1, keepdims=True)
        pos = jnp.where(lane == k, pos_k, pos)
        gate = jnp.where(lane == k, es[k] / den, gate)
    pos_ref[...] = pos.astype(jnp.int32)
    gate_ref[...] = gate
    cnt_ref[...] = jnp.broadcast_to(cnt, cnt_ref.shape).astype(jnp.int32)
    pos_t = jnp.transpose(pos)[0:8, :].astype(jnp.int32)
    for c in range(REGION // P_ROWS):
        row = lax.broadcasted_iota(jnp.int32, (P_ROWS, TS), 0) + c * P_ROWS
        hit = row == pos_t[0:1, :]
        for k in range(1, TOP_K):
            hit = jnp.logical_or(hit, row == pos_t[k:k + 1, :])
        p_scr[c * P_ROWS:(c + 1) * P_ROWS, :] = jnp.where(hit, 1.0, 0.0).astype(BF16)
    xs = jnp.dot(p_scr[...], x_ref[...], preferred_element_type=F32).astype(BF16)
    xs_ref[...] = jnp.where(pl.program_id(0) < pl.num_programs(0) - 1, xs, jnp.zeros((), BF16))


def _dispatch(logits, x1b):
    t = logits.shape[0]
    n_tiles = t // TS
    tri = jnp.asarray(np.tril(np.ones((TS, TS), np.float32), -1), dtype=BF16)
    upper = jnp.asarray(np.triu(np.ones((LANES, LANES), np.float32), 1), dtype=BF16)

    def tile(i):
        return jnp.minimum(i, n_tiles - 1)

    return pl.pallas_call(
        _dispatch_kernel,
        grid=(n_tiles + 1,),
        in_specs=[pl.BlockSpec((TS, LANES), lambda i: (tile(i), 0)),
                  pl.BlockSpec((TS, D_MODEL), lambda i: (tile(i), 0)),
                  pl.BlockSpec((TS, TS), lambda i: (0, 0)),
                  pl.BlockSpec((LANES, LANES), lambda i: (0, 0))],
        out_specs=[pl.BlockSpec((REGION, D_MODEL), lambda i: (i, 0)),
                   pl.BlockSpec((TS, LANES), lambda i: (tile(i), 0)),
                   pl.BlockSpec((TS, LANES), lambda i: (tile(i), 0)),
                   pl.BlockSpec((None, 8, LANES), lambda i: (tile(i), 0, 0))],
        out_shape=[jax.ShapeDtypeStruct(((n_tiles + 1) * REGION, D_MODEL), BF16),
                   jax.ShapeDtypeStruct((t, LANES), jnp.int32), jax.ShapeDtypeStruct((t, LANES), F32),
                   jax.ShapeDtypeStruct((n_tiles, 8, LANES), jnp.int32)],
        scratch_shapes=[pltpu.VMEM((REGION, TS), BF16)],
        compiler_params=pltpu.CompilerParams(
            dimension_semantics=("arbitrary",), vmem_limit_bytes=VMEM_LIMIT),
        name="dispatch",
    )(logits, x1b, tri, upper)


def _chunk_plan(cnt, bm):
    assert REGION >= 3 * bm
    n_tiles = cnt.shape[0]
    ch = bm // CHUNK_ROWS
    n_rows_max = n_tiles * REGION + N_EXPERTS * (bm - CHUNK_ROWS)
    n_blk = -(-n_rows_max // bm)
    run_chunks = (cnt + CHUNK_ROWS - 1) // CHUNK_ROWS
    run_off = (jnp.cumsum(run_chunks, axis=1) - run_chunks) * CHUNK_ROWS
    rows_e = jnp.sum(run_chunks, axis=0) * CHUNK_ROWS
    padded_e = (rows_e + bm - 1) // bm * bm
    pend = jnp.cumsum(padded_e)
    pstart = pend - padded_e
    runs_t = run_chunks.T
    run_start = (pstart[:, None] // CHUNK_ROWS + jnp.cumsum(runs_t, axis=1) - runs_t).reshape(-1)
    run_len = runs_t.reshape(-1)
    run_row = (jnp.arange(n_tiles, dtype=jnp.int32)[None, :] * REGION + run_off.T).reshape(-1)
    c = jnp.arange((n_blk + 1) * ch, dtype=jnp.int32)
    started = run_start[None, :] <= c[:, None]

    def of_run(v):
        step = v - jnp.concatenate([jnp.zeros((1,), v.dtype), v[:-1]])
        return jnp.sum(jnp.where(started, step[None, :], 0), axis=1)

    j = c - of_run(run_start)
    valid = j < of_run(run_len)
    row = of_run(run_row) + j * CHUNK_ROWS
    spare = n_tiles * REGION + (c % ch) * CHUNK_ROWS
    src = jnp.where(valid, row, spare + 2 * bm).astype(jnp.int32).reshape(n_blk + 1, 1, ch)
    dst = jnp.where(valid, row, spare + ((c // ch) % 2) * bm).astype(jnp.int32).reshape(n_blk + 1, 1, ch)
    blk_e = jnp.minimum(
        jnp.sum((pend[None, :] <= (jnp.arange(n_blk, dtype=jnp.int32) * bm)[:, None]).astype(jnp.int32), axis=1),
        N_EXPERTS - 1).astype(jnp.int32)
    n_used = (pend[-1:] // bm).astype(jnp.int32)
    return src, dst, blk_e, n_used


def _ffn_kernel(blk_e_ref, n_used_ref, src0_ref, srcn_ref, dst_ref, xs_hbm, wgu_ref, bgu_ref, wd_ref, bd_ref,
                ys_hbm, xbuf, ybuf, wgu_b, wd_b, in_sem, out_sem, *, bm):
    i = pl.program_id(0)
    n_used = n_used_ref[0]
    slot = i % 2
    ch = bm // CHUNK_ROWS

    def gather(tbl_ref, s):
        for j in range(ch):
            row = pl.multiple_of(tbl_ref[0, j], CHUNK_ROWS)
            pltpu.make_async_copy(xs_hbm.at[pl.ds(row, CHUNK_ROWS)],
                                  xbuf.at[s, pl.ds(j * CHUNK_ROWS, CHUNK_ROWS)], in_sem.at[s]).start()

    def wait_in(s):
        pltpu.make_async_copy(xs_hbm.at[pl.ds(0, bm)], xbuf.at[s], in_sem.at[s]).wait()

    def wait_out(s):
        pltpu.make_async_copy(ybuf.at[s], ys_hbm.at[pl.ds(0, bm)], out_sem.at[s]).wait()

    used = i < n_used
    e_changed = jnp.logical_or(i == 0, blk_e_ref[i] != blk_e_ref[jnp.maximum(i - 1, 0)])

    @pl.when(jnp.logical_and(i == 0, used))
    def _():
        gather(src0_ref, 0)

    @pl.when(jnp.logical_and(used, e_changed))
    def _():
        wgu_b[...] = wgu_ref[...].astype(BF16)
        wd_b[...] = wd_ref[...].astype(BF16)

    @pl.when(jnp.logical_and(used, i >= 2))
    def _():
        wait_out(slot)

    @pl.when(used)
    def _():
        gather(srcn_ref, 1 - slot)
        wait_in(slot)
        gu = jnp.dot(xbuf[slot], wgu_b[...], preferred_element_type=F32) + bgu_ref[...]
        gate = jnp.minimum(gu[:, :D_MODEL], SWIGLU_LIMIT)
        up = jnp.clip(gu[:, D_MODEL:], -SWIGLU_LIMIT, SWIGLU_LIMIT)
        h = (up + 1.0) * gate * jax.nn.sigmoid(SWIGLU_ALPHA * gate)
        y = jnp.dot(h.astype(BF16), wd_b[...], preferred_element_type=F32) + bd_ref[...]
        ybuf[slot] = y.astype(BF16)
        for j in range(ch):
            row = pl.multiple_of(dst_ref[0, j], CHUNK_ROWS)
            pltpu.make_async_copy(ybuf.at[slot, pl.ds(j * CHUNK_ROWS, CHUNK_ROWS)],
                                  ys_hbm.at[pl.ds(row, CHUNK_ROWS)], out_sem.at[slot]).start()

    @pl.when(i == n_used - 1)
    def _():
        wait_in(1 - slot)

        @pl.when(i >= 1)
        def _():
            wait_out(1 - slot)

        wait_out(slot)


def _ffn(xs, src, dst, blk_e, n_used, p, layer, bm):
    n_blk = blk_e.shape[0]
    ch = src.shape[-1]
    smem = functools.partial(pl.BlockSpec, memory_space=pltpu.SMEM)
    grid_spec = pltpu.PrefetchScalarGridSpec(
        num_scalar_prefetch=2,
        grid=(n_blk,),
        in_specs=[smem((None, 1, ch), lambda i, be, nu: (0, 0, 0)),
                  smem((None, 1, ch), lambda i, be, nu: (i + 1, 0, 0)),
                  smem((None, 1, ch), lambda i, be, nu: (i, 0, 0)),
                  pl.BlockSpec(memory_space=pl.ANY),
                  pl.BlockSpec((None, None, D_MODEL, 2 * D_MODEL), lambda i, be, nu: (layer, be[i], 0, 0)),
                  pl.BlockSpec((None, None, 1, 2 * D_MODEL), lambda i, be, nu: (layer, be[i], 0, 0)),
                  pl.BlockSpec((None, None, D_MODEL, D_MODEL), lambda i, be, nu: (layer, be[i], 0, 0)),
                  pl.BlockSpec((None, None, 1, D_MODEL), lambda i, be, nu: (layer, be[i], 0, 0))],
        out_specs=pl.BlockSpec(memory_space=pl.ANY),
        scratch_shapes=[pltpu.VMEM((2, bm, D_MODEL), BF16), pltpu.VMEM((2, bm, D_MODEL), BF16),
                        pltpu.VMEM((D_MODEL, 2 * D_MODEL), BF16), pltpu.VMEM((D_MODEL, D_MODEL), BF16),
                        pltpu.SemaphoreType.DMA((2,)), pltpu.SemaphoreType.DMA((2,))],
    )
    return pl.pallas_call(
        functools.partial(_ffn_kernel, bm=bm),
        grid_spec=grid_spec,
        out_shape=jax.ShapeDtypeStruct(xs.shape, BF16),
        input_output_aliases={5: 0},
        compiler_params=pltpu.CompilerParams(
            dimension_semantics=("arbitrary",), vmem_limit_bytes=VMEM_LIMIT),
        name="ffn",
    )(blk_e, n_used, src, src, dst, xs, p["w_gu"], p["b_gu"], p["w_down"], p["b_down"])


def _combine_kernel(ys_ref, pos_ref, gate_ref, x_ref, g_ref, b_ref, x2_ref, x2b_ref, g_scr, *, dn_alpha):
    lane = lax.broadcasted_iota(jnp.int32, (TS, LANES), 1)
    rel = [jnp.broadcast_to(pos_ref[:, k:k + 1], (TS, LANES)) - lane for k in range(TOP_K)]
    gate = [jnp.broadcast_to(gate_ref[:, k:k + 1], (TS, LANES)) for k in range(TOP_K)]
    for c in range(REGION // LANES):
        g = jnp.zeros((TS, LANES), F32)
        for k in range(TOP_K):
            g = jnp.where(rel[k] == c * LANES, gate[k], g)
        g_scr[:, c * LANES:(c + 1) * LANES] = g.astype(BF16)
    y = jnp.dot(g_scr[...], ys_ref[...], preferred_element_type=F32)
    x2 = _layer_norm(dn_alpha * x_ref[...] + y, g_ref[...], b_ref[...])
    x2_ref[...] = x2
    x2b_ref[...] = x2.astype(BF16)


def _combine(ys, pos, gates, x1, p, layer, dn_alpha):
    t = x1.shape[0]
    return pl.pallas_call(
        functools.partial(_combine_kernel, dn_alpha=np.float32(dn_alpha)),
        grid=(t // TS,),
        in_specs=[pl.BlockSpec((REGION, D_MODEL), lambda i: (i, 0)),
                  pl.BlockSpec((TS, LANES), lambda i: (i, 0)),
                  pl.BlockSpec((TS, LANES), lambda i: (i, 0)),
                  pl.BlockSpec((TS, D_MODEL), lambda i: (i, 0)),
                  pl.BlockSpec((None, 1, D_MODEL), lambda i: (layer, 0, 0)),
                  pl.BlockSpec((None, 1, D_MODEL), lambda i: (layer, 0, 0))],
        out_specs=[pl.BlockSpec((TS, D_MODEL), lambda i: (i, 0)),
                   pl.BlockSpec((TS, D_MODEL), lambda i: (i, 0))],
        out_shape=[jax.ShapeDtypeStruct((t, D_MODEL), F32), jax.ShapeDtypeStruct((t, D_MODEL), BF16)],
        scratch_shapes=[pltpu.VMEM((TS, REGION), BF16)],
        compiler_params=pltpu.CompilerParams(
            dimension_semantics=("parallel",), vmem_limit_bytes=VMEM_LIMIT),
        name="combine",
    )(ys, pos, gates, x1, p["ln2_g"], p["ln2_b"])


def _prepare_params(w_in, b_in, ln_v_g, ln_v_b, w_s, b_s, w_pa, w_pb, w_o, ln1_g, ln1_b,
                    w_r, b_r, w_gu, b_gu, w_down, b_down, ln2_g, ln2_b):
    depth = w_in.shape[0]
    n_qkv = 3 * len(B_PATTERNS) * B_GROUP_WIDTH
    qkv0, qkv1 = 2 * D_MODEL, 2 * D_MODEL + n_qkv

    def permute_cols(a):
        gap = jnp.zeros(a.shape[:-1] + (MAIN_COL0 - n_qkv,), a.dtype)
        return jnp.concatenate([a[..., qkv0:qkv1], gap, a[..., :qkv0], a[..., qkv1:]], axis=-1)

    def vec(a):
        return a.reshape(depth, 1, a.shape[-1]).astype(F32)

    w_r_pad = jnp.pad(w_r, ((0, 0), (0, 0), (0, LANES - N_EXPERTS))).astype(BF16)
    b_r_pad = jnp.pad(b_r.astype(F32), ((0, 0), (0, LANES - N_EXPERTS)), constant_values=NEG_INF)
    return dict(
        w_in=permute_cols(w_in).astype(BF16), b_in=vec(permute_cols(b_in)),
        ln_v_g=vec(ln_v_g), ln_v_b=vec(ln_v_b),
        w_s=w_s.astype(BF16),
        b_s=jnp.broadcast_to(b_s.astype(F32)[..., None], b_s.shape + (CHUNK,)),
        w_pa=w_pa.astype(BF16), w_pb=w_pb.astype(BF16), w_o=w_o.astype(BF16),
        ln1_g=vec(ln1_g), ln1_b=vec(ln1_b),
        w_r=w_r_pad, b_r=vec(b_r_pad),
        w_gu=w_gu, b_gu=b_gu.astype(F32)[:, :, None, :],
        w_down=w_down, b_down=b_down.astype(F32)[:, :, None, :],
        ln2_g=vec(ln2_g), ln2_b=vec(ln2_b),
    )


def _trunk(x, params, bm=512):
    n_seq, seq, _ = x.shape
    t = n_seq * seq
    depth = params["w_in"].shape[0]
    dn_alpha = (2.0 * depth) ** 0.25
    x = x.reshape(t, D_MODEL)
    xb = x.astype(BF16)
    for layer in range(depth):
        z = _inproj_main(xb, params["w_in"], params["b_in"], layer)
        attn = [_attention(_inproj_qkv(xb, params["w_in"], params["b_in"], layer, g, n_seq, seq), g)
                for g in range(len(B_PATTERNS))]
        x1, x1b, logits = _mixer(z, attn, x, params, layer, dn_alpha, seq)
        xs, pos, gates, cnt = _dispatch(logits, x1b)
        src, dst, blk_e, n_used = _chunk_plan(cnt[:, 0, :N_EXPERTS], bm)
        ys = _ffn(xs, src, dst, blk_e, n_used, params, layer, bm)
        x, xb = _combine(ys, pos, gates, x1, params, layer, dn_alpha)
    return x.reshape(n_seq, seq, D_MODEL)


def kernel(x_prompt, x_sample, w_in, b_in, ln_v_g, ln_v_b, w_s, b_s, w_pa, w_pb, w_o, ln1_g, ln1_b,
           w_r, b_r, w_gu, b_gu, w_down, b_down, ln2_g, ln2_b):
    params = _prepare_params(w_in, b_in, ln_v_g, ln_v_b, w_s, b_s, w_pa, w_pb, w_o, ln1_g, ln1_b,
                             w_r, b_r, w_gu, b_gu, w_down, b_down, ln2_g, ln2_b)
    n_prompt = x_prompt.shape[0]
    y = _trunk(jnp.concatenate([x_prompt, x_sample], axis=0), params)
    return (y[:n_prompt], y[n_prompt:])
```

```python
import functools

import numpy as np
import jax
import jax.numpy as jnp
from jax import lax
from jax.experimental import pallas as pl
from jax.experimental.pallas import tpu as pltpu

D_MODEL = 1024
CHUNK = 128
A_GROUPS = 8
B_PATTERNS = ((128, 1), (512, 4), (2048, 16))
B_HEADS = 4
B_HEAD_DIM = 128
B_GROUP_WIDTH = B_HEADS * B_HEAD_DIM
N_ATTN_HEADS = len(B_PATTERNS) * B_HEADS
N_SIDE = 64
D_IN = 4 * D_MODEL + 3 * len(B_PATTERNS) * B_GROUP_WIDTH
N_EXPERTS = 32
TOP_K = 4
SWIGLU_LIMIT = 7.0
SWIGLU_ALPHA = 1.702
LN_EPS = 1e-5
NEG_INF = -1e30
LANES = 128

MAIN_COLS = 4 * D_MODEL
QKV_COLS = 3 * B_GROUP_WIDTH
MAIN_COL0 = -(-len(B_PATTERNS) * QKV_COLS // D_MODEL) * D_MODEL
GELU_COLS = 2 * D_MODEL

VMEM_LIMIT = 56 * 1024 * 1024

BF16 = jnp.bfloat16
F32 = jnp.float32


def _gelu(x):
    return 0.5 * x * (1.0 + lax.erf(x * np.float32(np.sqrt(0.5))))


def _layer_norm(x, g, b):
    xc = x - jnp.mean(x, axis=-1, keepdims=True)
    var = jnp.mean(xc * xc, axis=-1, keepdims=True)
    return xc * lax.rsqrt(var + LN_EPS) * g + b


def _inproj_main_kernel(x_ref, w_ref, b_ref, z_ref, *, n_gelu_tiles):
    acc = jnp.dot(x_ref[...], w_ref[...], preferred_element_type=F32) + b_ref[...]
    j = pl.program_id(1)

    @pl.when(j < n_gelu_tiles)
    def _():
        z_ref[...] = _gelu(acc).astype(BF16)

    @pl.when(j >= n_gelu_tiles)
    def _():
        z_ref[...] = acc.astype(BF16)


def _inproj_main(xb, w_in, b_in, layer, tm=1024, tn=1024):
    t = xb.shape[0]
    return pl.pallas_call(
        functools.partial(_inproj_main_kernel, n_gelu_tiles=GELU_COLS // tn),
        grid=(t // tm, MAIN_COLS // tn),
        in_specs=[
            pl.BlockSpec((tm, D_MODEL), lambda i, j: (i, 0)),
            pl.BlockSpec((None, D_MODEL, tn), lambda i, j: (layer, 0, MAIN_COL0 // tn + j)),
            pl.BlockSpec((None, 1, tn), lambda i, j: (layer, 0, MAIN_COL0 // tn + j)),
        ],
        out_specs=pl.BlockSpec((tm, tn), lambda i, j: (i, j)),
        out_shape=jax.ShapeDtypeStruct((t, MAIN_COLS), BF16),
        compiler_params=pltpu.CompilerParams(
            dimension_semantics=("parallel", "arbitrary"), vmem_limit_bytes=VMEM_LIMIT),
        name="inproj_main",
    )(xb, w_in, b_in)


def _inproj_qkv_kernel(x_ref, w_ref, b_ref, o_ref, *scratch, dilation):
    acc = jnp.dot(x_ref[...], w_ref[...], preferred_element_type=F32) + b_ref[...]
    if dilation == 1:
        o_ref[0] = acc.astype(BF16)
    else:
        (scr,) = scratch
        rows = acc.shape[0] // dilation
        for c in range(acc.shape[1] // LANES):
            scr[c] = acc[:, c * LANES:(c + 1) * LANES]
        for r in range(dilation):
            for c in range(acc.shape[1] // LANES):
                o_ref[r, :, c * LANES:(c + 1) * LANES] = (
                    scr[c, pl.ds(r, rows, stride=dilation), :].astype(BF16))


def _inproj_qkv(xb, w_in, b_in, layer, group, n_seq, seq, tm=1024, tn=QKV_COLS):
    _, dilation = B_PATTERNS[group]
    tiles_per_seq = seq // tm
    col0 = group * QKV_COLS // tn
    return pl.pallas_call(
        functools.partial(_inproj_qkv_kernel, dilation=dilation),
        grid=(n_seq * tiles_per_seq, QKV_COLS // tn),
        in_specs=[
            pl.BlockSpec((tm, D_MODEL), lambda i, j: (i, 0)),
            pl.BlockSpec((None, D_MODEL, tn), lambda i, j: (layer, 0, col0 + j)),
            pl.BlockSpec((None, 1, tn), lambda i, j: (layer, 0, col0 + j)),
        ],
        out_specs=pl.BlockSpec((None, dilation, tm // dilation, tn),
                               lambda i, j: (i // tiles_per_seq, 0, i % tiles_per_seq, j)),
        out_shape=jax.ShapeDtypeStruct((n_seq, dilation, seq // dilation, QKV_COLS), BF16),
        scratch_shapes=[] if dilation == 1 else [pltpu.VMEM((tn // LANES, tm, LANES), F32)],
        compiler_params=pltpu.CompilerParams(
            dimension_semantics=("parallel", "arbitrary"), vmem_limit_bytes=VMEM_LIMIT),
        name=f"inproj_qkv{group}",
    )(xb, w_in, b_in)


def _attn_kernel(cur_ref, prev_ref, next_ref, bias_ref, o_ref, lse_ref, *scratch, bq):
    s_scr, p_scr = scratch[:B_HEADS], scratch[B_HEADS:]
    jq = pl.program_id(2)
    nq = pl.num_programs(2)
    n_piece = bq // N_SIDE
    col = lax.broadcasted_iota(jnp.int32, (N_SIDE, 3 * N_SIDE), 1)
    lane = lax.broadcasted_iota(jnp.int32, (N_SIDE, LANES), 1)
    scale = np.float32(B_HEAD_DIM ** -0.5)

    def window(p, part, h):
        c0 = part * B_GROUP_WIDTH + h * B_HEAD_DIM
        cols = slice(c0, c0 + B_HEAD_DIM)
        if n_piece == 1:
            return jnp.concatenate([prev_ref[:, cols], cur_ref[:, cols], next_ref[:, cols]], axis=0)
        if p == 0:
            return jnp.concatenate([prev_ref[:, cols], cur_ref[0:2 * N_SIDE, cols]], axis=0)
        if p == n_piece - 1:
            return jnp.concatenate([cur_ref[bq - 2 * N_SIDE:bq, cols], next_ref[:, cols]], axis=0)
        return cur_ref[(p - 1) * N_SIDE:(p + 2) * N_SIDE, cols]

    def scores(p, h):
        qh = cur_ref[p * N_SIDE:(p + 1) * N_SIDE, h * B_HEAD_DIM:(h + 1) * B_HEAD_DIM]
        s = lax.dot_general(qh, window(p, 1, h), (((1,), (1,)), ((), ())), preferred_element_type=F32) * scale
        if p == 0:
            s = jnp.where(jnp.logical_and(col < N_SIDE, jq == 0), NEG_INF, s)
        if p == n_piece - 1:
            s = jnp.where(jnp.logical_and(col >= 2 * N_SIDE, jq == nq - 1), NEG_INF, s)
        return s + bias_ref[h]

    def softmax(s):
        m = jnp.max(s, axis=-1, keepdims=True)
        e = jnp.exp(s - m)
        den = jnp.sum(e, axis=-1, keepdims=True)
        return (e / den).astype(BF16), m + jnp.log(den)

    def values(p, h, pn):
        o = jnp.dot(pn, window(p, 2, h), preferred_element_type=F32)
        o_ref[p * N_SIDE:(p + 1) * N_SIDE, h * B_HEAD_DIM:(h + 1) * B_HEAD_DIM] = o.astype(BF16)

    lse = {}
    for t in range(B_HEADS + 2):
        if t < B_HEADS:
            for p in range(n_piece):
                s_scr[t][p * N_SIDE:(p + 1) * N_SIDE, :] = scores(p, t)
        if 1 <= t <= B_HEADS:
            for p in range(n_piece):
                rows = slice(p * N_SIDE, (p + 1) * N_SIDE)
                p_scr[t - 1][rows, :], lse[p, t - 1] = softmax(s_scr[t - 1][rows, :])
        if t >= 2:
            for p in range(n_piece):
                values(p, t - 2, p_scr[t - 2][p * N_SIDE:(p + 1) * N_SIDE, :])
    for p in range(n_piece):
        tile = jnp.zeros((N_SIDE, LANES), F32)
        for h in range(B_HEADS):
            tile = jnp.where(lane == h, lse[p, h], tile)
        lse_ref[p * N_SIDE:(p + 1) * N_SIDE, :] = tile


def _attn_bias(group):
    _, dilation = B_PATTERNS[group]
    slopes = np.array([2.0 ** (-8.0 * (group * B_HEADS + h + 1) / N_ATTN_HEADS) for h in range(B_HEADS)],
                      dtype=np.float32)
    a = np.arange(N_SIDE)[:, None]
    c = np.arange(3 * N_SIDE)[None, :]
    rel = c - N_SIDE - a
    alibi = (-(slopes[:, None, None] * (dilation * np.abs(rel))[None])).astype(np.float32)
    return np.where((np.abs(rel) <= N_SIDE)[None], alibi, np.float32(NEG_INF)).astype(np.float32)


def _attention(qkv, group, bq_max=512):
    n_seq, dilation, sub_len, _ = qkv.shape
    bq = min(bq_max, sub_len)
    nq = sub_len // bq
    halo_per_bq = bq // N_SIDE
    n_halo = sub_len // N_SIDE
    return pl.pallas_call(
        functools.partial(_attn_kernel, bq=bq),
        grid=(n_seq, dilation, nq),
        in_specs=[pl.BlockSpec((None, None, bq, QKV_COLS), lambda b, r, j: (b, r, j, 0)),
                  pl.BlockSpec((None, None, N_SIDE, QKV_COLS),
                               lambda b, r, j: (b, r, jnp.maximum(j * halo_per_bq - 1, 0), 0)),
                  pl.BlockSpec((None, None, N_SIDE, QKV_COLS),
                               lambda b, r, j: (b, r, jnp.minimum((j + 1) * halo_per_bq, n_halo - 1), 0)),
                  pl.BlockSpec((B_HEADS, N_SIDE, 3 * N_SIDE), lambda b, r, j: (0, 0, 0))],
        out_specs=[pl.BlockSpec((None, None, bq, B_GROUP_WIDTH), lambda b, r, j: (b, r, j, 0)),
                   pl.BlockSpec((None, None, bq, LANES), lambda b, r, j: (b, r, j, 0))],
        out_shape=[jax.ShapeDtypeStruct((n_seq, dilation, sub_len, B_GROUP_WIDTH), BF16),
                   jax.ShapeDtypeStruct((n_seq, dilation, sub_len, LANES), F32)],
        scratch_shapes=([pltpu.VMEM((bq, 3 * N_SIDE), F32)] * B_HEADS
                        + [pltpu.VMEM((bq, 3 * N_SIDE), BF16)] * B_HEADS),
        compiler_params=pltpu.CompilerParams(
            dimension_semantics=("parallel", "parallel", "arbitrary"), vmem_limit_bytes=VMEM_LIMIT),
        name=f"attn{group}",
    )(qkv, qkv, qkv, jnp.asarray(_attn_bias(group)))


def _mixer_kernel(u_ref, v_ref, ga_ref, gb_ref, o0_ref, o1_ref, o2_ref, l0_ref, l1_ref, l2_ref, x_ref,
                  lnvg_ref, lnvb_ref, ws_ref, bs_ref, wpa_ref, wpb_ref, wo_ref, ln1g_ref, ln1b_ref,
                  wr_ref, br_ref, x1_ref, x1b_ref, logit_ref, a_scr, b_scr, o_scr, l_scr, *, tm, dn_alpha):
    vn = _layer_norm(v_ref[...].astype(F32), lnvg_ref[...], lnvb_ref[...]).astype(BF16)
    for c in range(tm // CHUNK):
        r0 = c * CHUNK
        for g in range(A_GROUPS):
            c0 = g * CHUNK
            mixed = jnp.dot(ws_ref[g], vn[r0:r0 + CHUNK, c0:c0 + CHUNK], preferred_element_type=F32)
            mixed = mixed + bs_ref[g]
            a = u_ref[r0:r0 + CHUNK, c0:c0 + CHUNK].astype(F32) * mixed
            a_scr[r0:r0 + CHUNK, c0:c0 + CHUNK] = a.astype(BF16)
    for g, (o_ref, l_ref) in enumerate(((o0_ref, l0_ref), (o1_ref, l1_ref), (o2_ref, l2_ref))):
        dilation = o_ref.shape[0]
        rows = tm // dilation
        for r in range(dilation):
            idx = pl.ds(r, rows, stride=dilation) if dilation > 1 else slice(None)
            for h in range(B_HEADS):
                o_scr[g, h, idx, :] = o_ref[r, :, h * B_HEAD_DIM:(h + 1) * B_HEAD_DIM].astype(F32)
            l_scr[g, idx, :] = l_ref[r]
    n_groups = len(B_PATTERNS)
    for h in range(B_HEADS):
        c0 = h * B_HEAD_DIM
        ls = [l_scr[g, :, h:h + 1] for g in range(n_groups)]
        mx = jnp.maximum(jnp.maximum(ls[0], ls[1]), ls[2])
        es = [jnp.exp(l - mx) for l in ls]
        den = es[0] + es[1] + es[2]
        acc = jnp.zeros((tm, B_HEAD_DIM), F32)
        for g in range(n_groups):
            acc = acc + (es[g] / den) * o_scr[g, h]
        b_scr[:, c0:c0 + B_HEAD_DIM] = acc.astype(BF16)
    pa = jnp.dot(a_scr[...], wpa_ref[...], preferred_element_type=F32)
    pb = jnp.dot(b_scr[...], wpb_ref[...], preferred_element_type=F32)
    merged = jax.nn.sigmoid(ga_ref[...].astype(F32)) * pa + jax.nn.sigmoid(gb_ref[...].astype(F32)) * pb
    mix = jnp.dot(merged.astype(BF16), wo_ref[...], preferred_element_type=F32)
    x1 = _layer_norm(dn_alpha * x_ref[...] + mix, ln1g_ref[...], ln1b_ref[...])
    x1_ref[...] = x1
    x1b = x1.astype(BF16)
    x1b_ref[...] = x1b
    logit_ref[...] = jnp.dot(x1b, wr_ref[...], preferred_element_type=F32) + br_ref[...]


def _mixer(z, attn, x, p, layer, dn_alpha, seq, tm=512):
    t = x.shape[0]
    tiles_per_seq = seq // tm

    def zspec(cb):
        return pl.BlockSpec((tm, D_MODEL), lambda i: (i, cb))

    def row(width):
        return pl.BlockSpec((tm, width), lambda i: (i, 0))

    def split(a):
        _, dilation, _, width = a.shape
        return pl.BlockSpec((None, dilation, tm // dilation, width),
                            lambda i: (i // tiles_per_seq, 0, i % tiles_per_seq, 0))

    def vec(width):
        return pl.BlockSpec((None, 1, width), lambda i: (layer, 0, 0))

    def mat(k, n):
        return pl.BlockSpec((None, k, n), lambda i: (layer, 0, 0))

    (o0, l0), (o1, l1), (o2, l2) = attn
    return pl.pallas_call(
        functools.partial(_mixer_kernel, tm=tm, dn_alpha=np.float32(dn_alpha)),
        grid=(t // tm,),
        in_specs=[zspec(0), zspec(1), zspec(2), zspec(3),
                  split(o0), split(o1), split(o2), split(l0), split(l1), split(l2), row(D_MODEL),
                  vec(D_MODEL), vec(D_MODEL),
                  pl.BlockSpec((None, A_GROUPS, CHUNK, CHUNK), lambda i: (layer, 0, 0, 0)),
                  pl.BlockSpec((None, A_GROUPS, CHUNK, CHUNK), lambda i: (layer, 0, 0, 0)),
                  mat(D_MODEL, D_MODEL), mat(B_GROUP_WIDTH, D_MODEL), mat(D_MODEL, D_MODEL),
                  vec(D_MODEL), vec(D_MODEL), mat(D_MODEL, LANES), vec(LANES)],
        out_specs=[row(D_MODEL), row(D_MODEL), row(LANES)],
        out_shape=[jax.ShapeDtypeStruct((t, D_MODEL), F32), jax.ShapeDtypeStruct((t, D_MODEL), BF16),
                   jax.ShapeDtypeStruct((t, LANES), F32)],
        scratch_shapes=[pltpu.VMEM((tm, D_MODEL), BF16), pltpu.VMEM((tm, B_GROUP_WIDTH), BF16),
                        pltpu.VMEM((len(B_PATTERNS), B_HEADS, tm, B_HEAD_DIM), F32),
                        pltpu.VMEM((len(B_PATTERNS), tm, LANES), F32)],
        compiler_params=pltpu.CompilerParams(
            dimension_semantics=("parallel",), vmem_limit_bytes=VMEM_LIMIT),
        name="mixer",
    )(z, z, z, z, o0, o1, o2, l0, l1, l2, x,
      p["ln_v_g"], p["ln_v_b"], p["w_s"], p["b_s"], p["w_pa"], p["w_pb"], p["w_o"],
      p["ln1_g"], p["ln1_b"], p["w_r"], p["b_r"])


TS = 512
CHUNK_ROWS = 16
REGION = TS * TOP_K + N_EXPERTS * CHUNK_ROWS
P_ROWS = 256


def _chunks_per_expert(cnt):
    return jnp.floor((cnt + (CHUNK_ROWS - 1)) * (1.0 / CHUNK_ROWS))


def _dispatch_kernel(logit_ref, x_ref, tri_ref, upper_ref, xs_ref, pos_ref, gate_ref, cnt_ref, p_scr):
    l = logit_ref[...]
    lane = lax.broadcasted_iota(jnp.int32, (TS, LANES), 1)
    onehots, vals = [], []
    for _ in range(TOP_K):
        mx = jnp.max(l, axis=-1, keepdims=True)
        idx = jnp.min(jnp.where(l == mx, lane, LANES), axis=-1, keepdims=True)
        oh = lane == idx
        l = jnp.where(oh, -jnp.inf, l)
        onehots.append(oh)
        vals.append(mx)
    es = [jnp.exp(v - vals[0]) for v in vals]
    den = es[0] + es[1] + es[2] + es[3]
    multi = jnp.zeros((TS, LANES), F32)
    for oh in onehots:
        multi = multi + oh.astype(F32)
    cnt = jnp.sum(multi, axis=0, keepdims=True)
    n_chunks = jnp.broadcast_to(_chunks_per_expert(cnt), (8, LANES)).astype(BF16)
    run_off = CHUNK_ROWS * jnp.dot(n_chunks, upper_ref[...], preferred_element_type=F32)[0:1, :]
    rank = jnp.dot(tri_ref[...], multi.astype(BF16), preferred_element_type=F32)
    pos = jnp.full((TS, LANES), -1.0, F32)
    gate = jnp.zeros((TS, LANES), F32)
    for k in range(TOP_K):
        pos_k = jnp.sum(jnp.where(onehots[k], rank + run_off, 0.0), axis=-1, keepdims=True)
        pos = jnp.where(lane == k, pos_k, pos)
        gate = jnp.where(lane == k, es[k] / den, gate)
    pos_ref[...] = pos.astype(jnp.int32)
    gate_ref[...] = gate
    cnt_ref[...] = jnp.broadcast_to(cnt, cnt_ref.shape).astype(jnp.int32)
    live = pl.program_id(0) < pl.num_programs(0) - 1
    pos_t = jnp.where(live, jnp.transpose(pos)[0:8, :], -1.0)
    row = lax.broadcasted_iota(jnp.int32, (P_ROWS, TS), 0).astype(F32).astype(BF16)
    for c in range(REGION // P_ROWS):
        rel = pos_t - float(c * P_ROWS)
        rel = jnp.where(jnp.logical_and(rel >= 0.0, rel < float(P_ROWS)), rel, -1.0).astype(BF16)
        perm = jnp.zeros((P_ROWS, TS), BF16)
        for k in range(TOP_K):
            perm = jnp.where(row == rel[k:k + 1, :], jnp.ones((), BF16), perm)
        xs_ref[c * P_ROWS:(c + 1) * P_ROWS, :] = jnp.dot(
            perm, x_ref[...], preferred_element_type=F32).astype(BF16)


def _dispatch(logits, x1b):
    t = logits.shape[0]
    n_tiles = t // TS
    tri = jnp.asarray(np.tril(np.ones((TS, TS), np.float32), -1), dtype=BF16)
    upper = jnp.asarray(np.triu(np.ones((LANES, LANES), np.float32), 1), dtype=BF16)

    def tile(i):
        return jnp.minimum(i, n_tiles - 1)

    return pl.pallas_call(
        _dispatch_kernel,
        grid=(n_tiles + 1,),
        in_specs=[pl.BlockSpec((TS, LANES), lambda i: (tile(i), 0)),
                  pl.BlockSpec((TS, D_MODEL), lambda i: (tile(i), 0)),
                  pl.BlockSpec((TS, TS), lambda i: (0, 0)),
                  pl.BlockSpec((LANES, LANES), lambda i: (0, 0))],
        out_specs=[pl.BlockSpec((REGION, D_MODEL), lambda i: (i, 0)),
                   pl.BlockSpec((TS, LANES), lambda i: (tile(i), 0)),
                   pl.BlockSpec((TS, LANES), lambda i: (tile(i), 0)),
                   pl.BlockSpec((None, 8, LANES), lambda i: (tile(i), 0, 0))],
        out_shape=[jax.ShapeDtypeStruct(((n_tiles + 1) * REGION, D_MODEL), BF16),
                   jax.ShapeDtypeStruct((t, LANES), jnp.int32), jax.ShapeDtypeStruct((t, LANES), F32),
                   jax.ShapeDtypeStruct((n_tiles, 8, LANES), jnp.int32)],
        scratch_shapes=[pltpu.VMEM((REGION, TS), BF16)],
        compiler_params=pltpu.CompilerParams(
            dimension_semantics=("arbitrary",), vmem_limit_bytes=VMEM_LIMIT),
        name="dispatch",
    )(logits, x1b, tri, upper)


def _chunk_plan(cnt, bm):
    assert REGION >= 3 * bm
    n_tiles = cnt.shape[0]
    ch = bm // CHUNK_ROWS
    n_rows_max = n_tiles * REGION + N_EXPERTS * (bm - CHUNK_ROWS)
    n_blk = -(-n_rows_max // bm)
    run_chunks = (cnt + CHUNK_ROWS - 1) // CHUNK_ROWS
    run_off = (jnp.cumsum(run_chunks, axis=1) - run_chunks) * CHUNK_ROWS
    rows_e = jnp.sum(run_chunks, axis=0) * CHUNK_ROWS
    padded_e = (rows_e + bm - 1) // bm * bm
    pend = jnp.cumsum(padded_e)
    pstart = pend - padded_e
    runs_t = run_chunks.T
    run_start = (pstart[:, None] // CHUNK_ROWS + jnp.cumsum(runs_t, axis=1) - runs_t).reshape(-1)
    run_len = runs_t.reshape(-1)
    run_row = (jnp.arange(n_tiles, dtype=jnp.int32)[None, :] * REGION + run_off.T).reshape(-1)
    c = jnp.arange((n_blk + 1) * ch, dtype=jnp.int32)
    started = run_start[None, :] <= c[:, None]

    def of_run(v):
        step = v - jnp.concatenate([jnp.zeros((1,), v.dtype), v[:-1]])
        return jnp.sum(jnp.where(started, step[None, :], 0), axis=1)

    j = c - of_run(run_start)
    valid = j < of_run(run_len)
    row = of_run(run_row) + j * CHUNK_ROWS
    spare = n_tiles * REGION + (c % ch) * CHUNK_ROWS
    src = jnp.where(valid, row, spare + 2 * bm).astype(jnp.int32).reshape(n_blk + 1, 1, ch)
    dst = jnp.where(valid, row, spare + ((c // ch) % 2) * bm).astype(jnp.int32).reshape(n_blk + 1, 1, ch)
    dst = jnp.concatenate([dst[n_blk:], dst[:n_blk]])
    blk_e = jnp.minimum(
        jnp.sum((pend[None, :] <= (jnp.arange(n_blk, dtype=jnp.int32) * bm)[:, None]).astype(jnp.int32), axis=1),
        N_EXPERTS - 1).astype(jnp.int32)
    n_used = (pend[-1:] // bm).astype(jnp.int32)
    return src, dst, blk_e, n_used


def _ffn_kernel(blk_e_ref, n_used_ref, src0_ref, srcn_ref, dstp_ref, dstc_ref, xs_hbm, wgu_ref, bgu_ref, wd_ref,
                bd_ref, ys_hbm, xbuf, ybuf, wgu_b, wd_b, in_sem, out_sem, *, bm):
    i = pl.program_id(0)
    n_used = n_used_ref[0]
    slot = i % 2
    ch = bm // CHUNK_ROWS

    def gather(tbl_ref, s):
        for j in range(ch):
            row = pl.multiple_of(tbl_ref[0, j], CHUNK_ROWS)
            pltpu.make_async_copy(xs_hbm.at[pl.ds(row, CHUNK_ROWS)],
                                  xbuf.at[s, pl.ds(j * CHUNK_ROWS, CHUNK_ROWS)], in_sem.at[s]).start()

    def scatter(tbl_ref, s):
        for j in range(ch):
            row = pl.multiple_of(tbl_ref[0, j], CHUNK_ROWS)
            pltpu.make_async_copy(ybuf.at[s, pl.ds(j * CHUNK_ROWS, CHUNK_ROWS)],
                                  ys_hbm.at[pl.ds(row, CHUNK_ROWS)], out_sem.at[s]).start()

    def wait_in(s):
        pltpu.make_async_copy(xs_hbm.at[pl.ds(0, bm)], xbuf.at[s], in_sem.at[s]).wait()

    def wait_out(s):
        pltpu.make_async_copy(ybuf.at[s], ys_hbm.at[pl.ds(0, bm)], out_sem.at[s]).wait()

    used = i < n_used
    e_changed = jnp.logical_or(i == 0, blk_e_ref[i] != blk_e_ref[jnp.maximum(i - 1, 0)])

    @pl.when(jnp.logical_and(i == 0, used))
    def _():
        gather(src0_ref, 0)
        ybuf[1] = jnp.zeros((bm, D_MODEL), BF16)

    @pl.when(jnp.logical_and(used, e_changed))
    def _():
        wgu_b[...] = wgu_ref[...].astype(BF16)
        wd_b[...] = wd_ref[...].astype(BF16)

    @pl.when(jnp.logical_and(used, i >= 1))
    def _():
        wait_out(slot)

    @pl.when(used)
    def _():
        wait_in(slot)
        gu = jnp.dot(xbuf[slot], wgu_b[...], preferred_element_type=F32) + bgu_ref[...]
        gather(srcn_ref, 1 - slot)
        scatter(dstp_ref, 1 - slot)
        gate = jnp.minimum(gu[:, :D_MODEL], SWIGLU_LIMIT)
        up = jnp.clip(gu[:, D_MODEL:], -SWIGLU_LIMIT, SWIGLU_LIMIT)
        h = (up + 1.0) * gate * jax.nn.sigmoid(SWIGLU_ALPHA * gate)
        y = jnp.dot(h.astype(BF16), wd_b[...], preferred_element_type=F32) + bd_ref[...]
        ybuf[slot] = y.astype(BF16)

    @pl.when(i == n_used - 1)
    def _():
        scatter(dstc_ref, slot)
        wait_in(1 - slot)
        wait_out(1 - slot)
        wait_out(slot)


def _ffn(xs, src, dst, blk_e, n_used, p, layer, bm):
    n_blk = blk_e.shape[0]
    ch = src.shape[-1]
    smem = functools.partial(pl.BlockSpec, memory_space=pltpu.SMEM)
    grid_spec = pltpu.PrefetchScalarGridSpec(
        num_scalar_prefetch=2,
        grid=(n_blk,),
        in_specs=[smem((None, 1, ch), lambda i, be, nu: (0, 0, 0)),
                  smem((None, 1, ch), lambda i, be, nu: (i + 1, 0, 0)),
                  smem((None, 1, ch), lambda i, be, nu: (i, 0, 0)),
                  smem((None, 1, ch), lambda i, be, nu: (i + 1, 0, 0)),
                  pl.BlockSpec(memory_space=pl.ANY),
                  pl.BlockSpec((None, None, D_MODEL, 2 * D_MODEL), lambda i, be, nu: (layer, be[i], 0, 0)),
                  pl.BlockSpec((None, None, 1, 2 * D_MODEL), lambda i, be, nu: (layer, be[i], 0, 0)),
                  pl.BlockSpec((None, None, D_MODEL, D_MODEL), lambda i, be, nu: (layer, be[i], 0, 0)),
                  pl.BlockSpec((None, None, 1, D_MODEL), lambda i, be, nu: (layer, be[i], 0, 0))],
        out_specs=pl.BlockSpec(memory_space=pl.ANY),
        scratch_shapes=[pltpu.VMEM((2, bm, D_MODEL), BF16), pltpu.VMEM((2, bm, D_MODEL), BF16),
                        pltpu.VMEM((D_MODEL, 2 * D_MODEL), BF16), pltpu.VMEM((D_MODEL, D_MODEL), BF16),
                        pltpu.SemaphoreType.DMA((2,)), pltpu.SemaphoreType.DMA((2,))],
    )
    return pl.pallas_call(
        functools.partial(_ffn_kernel, bm=bm),
        grid_spec=grid_spec,
        out_shape=jax.ShapeDtypeStruct(xs.shape, BF16),
        input_output_aliases={6: 0},
        compiler_params=pltpu.CompilerParams(
            dimension_semantics=("arbitrary",), vmem_limit_bytes=VMEM_LIMIT),
        name="ffn",
    )(blk_e, n_used, src, src, dst, dst, xs, p["w_gu"], p["b_gu"], p["w_down"], p["b_down"])


def _combine_kernel(ys_ref, pos_ref, gate_ref, x_ref, g_ref, b_ref, x2_ref, x2b_ref, g_scr, *, dn_alpha):
    lane = lax.broadcasted_iota(jnp.int32, (TS, LANES), 1)
    rel = [jnp.broadcast_to(pos_ref[:, k:k + 1], (TS, LANES)) - lane for k in range(TOP_K)]
    gate = [jnp.broadcast_to(gate_ref[:, k:k + 1], (TS, LANES)) for k in range(TOP_K)]
    for c in range(REGION // LANES):
        g = jnp.zeros((TS, LANES), F32)
        for k in range(TOP_K):
            g = jnp.where(rel[k] == c * LANES, gate[k], g)
        g_scr[:, c * LANES:(c + 1) * LANES] = g.astype(BF16)
    y = jnp.dot(g_scr[...], ys_ref[...], preferred_element_type=F32)
    x2 = _layer_norm(dn_alpha * x_ref[...] + y, g_ref[...], b_ref[...])
    x2_ref[...] = x2
    x2b_ref[...] = x2.astype(BF16)


def _combine(ys, pos, gates, x1, p, layer, dn_alpha):
    t = x1.shape[0]
    return pl.pallas_call(
        functools.partial(_combine_kernel, dn_alpha=np.float32(dn_alpha)),
        grid=(t // TS,),
        in_specs=[pl.BlockSpec((REGION, D_MODEL), lambda i: (i, 0)),
                  pl.BlockSpec((TS, LANES), lambda i: (i, 0)),
                  pl.BlockSpec((TS, LANES), lambda i: (i, 0)),
                  pl.BlockSpec((TS, D_MODEL), lambda i: (i, 0)),
                  pl.BlockSpec((None, 1, D_MODEL), lambda i: (layer, 0, 0)),
                  pl.BlockSpec((None, 1, D_MODEL), lambda i: (layer, 0, 0))],
        out_specs=[pl.BlockSpec((TS, D_MODEL), lambda i: (i, 0)),
                   pl.BlockSpec((TS, D_MODEL), lambda i: (i, 0))],
        out_shape=[jax.ShapeDtypeStruct((t, D_MODEL), F32), jax.ShapeDtypeStruct((t, D_MODEL), BF16)],
        scratch_shapes=[pltpu.VMEM((TS, REGION), BF16)],
        compiler_params=pltpu.CompilerParams(
            dimension_semantics=("parallel",), vmem_limit_bytes=VMEM_LIMIT),
        name="combine",
    )(ys, pos, gates, x1, p["ln2_g"], p["ln2_b"])


def _prepare_params(w_in, b_in, ln_v_g, ln_v_b, w_s, b_s, w_pa, w_pb, w_o, ln1_g, ln1_b,
                    w_r, b_r, w_gu, b_gu, w_down, b_down, ln2_g, ln2_b):
    depth = w_in.shape[0]
    n_qkv = 3 * len(B_PATTERNS) * B_GROUP_WIDTH
    qkv0, qkv1 = 2 * D_MODEL, 2 * D_MODEL + n_qkv

    def permute_cols(a):
        gap = jnp.zeros(a.shape[:-1] + (MAIN_COL0 - n_qkv,), a.dtype)
        return jnp.concatenate([a[..., qkv0:qkv1], gap, a[..., :qkv0], a[..., qkv1:]], axis=-1)

    def vec(a):
        return a.reshape(depth, 1, a.shape[-1]).astype(F32)

    w_r_pad = jnp.pad(w_r, ((0, 0), (0, 0), (0, LANES - N_EXPERTS))).astype(BF16)
    b_r_pad = jnp.pad(b_r.astype(F32), ((0, 0), (0, LANES - N_EXPERTS)), constant_values=NEG_INF)
    return dict(
        w_in=permute_cols(w_in).astype(BF16), b_in=vec(permute_cols(b_in)),
        ln_v_g=vec(ln_v_g), ln_v_b=vec(ln_v_b),
        w_s=w_s.astype(BF16),
        b_s=jnp.broadcast_to(b_s.astype(F32)[..., None], b_s.shape + (CHUNK,)),
        w_pa=w_pa.astype(BF16), w_pb=w_pb.astype(BF16), w_o=w_o.astype(BF16),
        ln1_g=vec(ln1_g), ln1_b=vec(ln1_b),
        w_r=w_r_pad, b_r=vec(b_r_pad),
        w_gu=w_gu, b_gu=b_gu.astype(F32)[:, :, None, :],
        w_down=w_down, b_down=b_down.astype(F32)[:, :, None, :],
        ln2_g=vec(ln2_g), ln2_b=vec(ln2_b),
    )


def _trunk(x, params, bm=512):
    n_seq, seq, _ = x.shape
    t = n_seq * seq
    depth = params["w_in"].shape[0]
    dn_alpha = (2.0 * depth) ** 0.25
    x = x.reshape(t, D_MODEL)
    xb = x.astype(BF16)
    for layer in range(depth):
        z = _inproj_main(xb, params["w_in"], params["b_in"], layer)
        attn = [_attention(_inproj_qkv(xb, params["w_in"], params["b_in"], layer, g, n_seq, seq), g)
                for g in range(len(B_PATTERNS))]
        x1, x1b, logits = _mixer(z, attn, x, params, layer, dn_alpha, seq)
        xs, pos, gates, cnt = _dispatch(logits, x1b)
        src, dst, blk_e, n_used = _chunk_plan(cnt[:, 0, :N_EXPERTS], bm)
        ys = _ffn(xs, src, dst, blk_e, n_used, params, layer, bm)
        x, xb = _combine(ys, pos, gates, x1, params, layer, dn_alpha)
    return x.reshape(n_seq, seq, D_MODEL)


def kernel(x_prompt, x_sample, w_in, b_in, ln_v_g, ln_v_b, w_s, b_s, w_pa, w_pb, w_o, ln1_g, ln1_b,
           w_r, b_r, w_gu, b_gu, w_down, b_down, ln2_g, ln2_b):
    params = _prepare_params(w_in, b_in, ln_v_g, ln_v_b, w_s, b_s, w_pa, w_pb, w_o, ln1_g, ln1_b,
                             w_r, b_r, w_gu, b_gu, w_down, b_down, ln2_g, ln2_b)
    n_prompt = x_prompt.shape[0]
    y = _trunk(jnp.concatenate([x_prompt, x_sample], axis=0), params)
    return (y[:n_prompt], y[n_prompt:])
```

```python
import functools

import numpy as np
import jax
import jax.numpy as jnp
from jax import lax
from jax.experimental import pallas as pl
from jax.experimental.pallas import tpu as pltpu

D_MODEL = 1024
CHUNK = 128
A_GROUPS = 8
B_PATTERNS = ((128, 1), (512, 4), (2048, 16))
B_HEADS = 4
B_HEAD_DIM = 128
B_GROUP_WIDTH = B_HEADS * B_HEAD_DIM
N_ATTN_HEADS = len(B_PATTERNS) * B_HEADS
N_SIDE = 64
D_IN = 4 * D_MODEL + 3 * len(B_PATTERNS) * B_GROUP_WIDTH
N_EXPERTS = 32
TOP_K = 4
SWIGLU_LIMIT = 7.0
SWIGLU_ALPHA = 1.702
LN_EPS = 1e-5
NEG_INF = -1e30
LANES = 128

MAIN_COLS = 4 * D_MODEL
QKV_COLS = 3 * B_GROUP_WIDTH
MAIN_COL0 = -(-len(B_PATTERNS) * QKV_COLS // D_MODEL) * D_MODEL
GELU_COLS = 2 * D_MODEL

VMEM_LIMIT = 56 * 1024 * 1024

BF16 = jnp.bfloat16
F32 = jnp.float32


def _gelu(x):
    return 0.5 * x * (1.0 + lax.erf(x * np.float32(np.sqrt(0.5))))


def _layer_norm(x, g, b):
    xc = x - jnp.mean(x, axis=-1, keepdims=True)
    var = jnp.mean(xc * xc, axis=-1, keepdims=True)
    return xc * lax.rsqrt(var + LN_EPS) * g + b


N_CHUNK = 256
MAX_ROW_STRIDE = 4


def _inproj_main_kernel(x_ref, w_ref, b_ref, z_ref, *, n_gelu_tiles):
    j = pl.program_id(1)

    def run(act):
        for c in range(z_ref.shape[1] // N_CHUNK):
            cols = slice(c * N_CHUNK, (c + 1) * N_CHUNK)
            acc = jnp.dot(x_ref[...], w_ref[:, cols], preferred_element_type=F32) + b_ref[:, cols]
            z_ref[:, cols] = act(acc).astype(BF16)

    @pl.when(j < n_gelu_tiles)
    def _():
        run(_gelu)

    @pl.when(j >= n_gelu_tiles)
    def _():
        run(lambda a: a)


def _inproj_main(xb, w_in, b_in, layer, tm=1024, tn=1024):
    t = xb.shape[0]
    return pl.pallas_call(
        functools.partial(_inproj_main_kernel, n_gelu_tiles=GELU_COLS // tn),
        grid=(t // tm, MAIN_COLS // tn),
        in_specs=[
            pl.BlockSpec((tm, D_MODEL), lambda i, j: (i, 0)),
            pl.BlockSpec((None, D_MODEL, tn), lambda i, j: (layer, 0, MAIN_COL0 // tn + j)),
            pl.BlockSpec((None, 1, tn), lambda i, j: (layer, 0, MAIN_COL0 // tn + j)),
        ],
        out_specs=pl.BlockSpec((tm, tn), lambda i, j: (i, j)),
        out_shape=jax.ShapeDtypeStruct((t, MAIN_COLS), BF16),
        compiler_params=pltpu.CompilerParams(
            dimension_semantics=("parallel", "arbitrary"), vmem_limit_bytes=VMEM_LIMIT),
        name="inproj_main",
    )(xb, w_in, b_in)


def _inproj_qkv_kernel(x_ref, w_ref, b_ref, o_ref, *scratch, dilation):
    rows = x_ref.shape[0] // dilation
    for c in range(o_ref.shape[-1] // N_CHUNK):
        cols = slice(c * N_CHUNK, (c + 1) * N_CHUNK)
        acc = jnp.dot(x_ref[...], w_ref[:, cols], preferred_element_type=F32) + b_ref[:, cols]
        if dilation == 1:
            o_ref[0, :, cols] = acc.astype(BF16)
            continue
        scr, scr2 = scratch[0], scratch[-1]
        d1 = min(dilation, MAX_ROW_STRIDE)
        d2 = dilation // d1
        for q in range(c * N_CHUNK // LANES, (c + 1) * N_CHUNK // LANES):
            scr[q] = acc[:, q * LANES - c * N_CHUNK:(q + 1) * LANES - c * N_CHUNK]
            for r1 in range(d1):
                part = scr[q, pl.ds(r1, rows * d2, stride=d1), :]
                if d2 == 1:
                    o_ref[r1, :, q * LANES:(q + 1) * LANES] = part.astype(BF16)
                    continue
                scr2[q, r1 * rows * d2:(r1 + 1) * rows * d2, :] = part
                for r2 in range(d2):
                    o_ref[r2 * d1 + r1, :, q * LANES:(q + 1) * LANES] = (
                        scr2[q, pl.ds(r1 * rows * d2 + r2, rows, stride=d2), :].astype(BF16))


def _inproj_qkv(xb, w_in, b_in, layer, group, n_seq, seq, tm=1024, tn=QKV_COLS):
    _, dilation = B_PATTERNS[group]
    tiles_per_seq = seq // tm
    col0 = group * QKV_COLS // tn
    return pl.pallas_call(
        functools.partial(_inproj_qkv_kernel, dilation=dilation),
        grid=(n_seq * tiles_per_seq, QKV_COLS // tn),
        in_specs=[
            pl.BlockSpec((tm, D_MODEL), lambda i, j: (i, 0)),
            pl.BlockSpec((None, D_MODEL, tn), lambda i, j: (layer, 0, col0 + j)),
            pl.BlockSpec((None, 1, tn), lambda i, j: (layer, 0, col0 + j)),
        ],
        out_specs=pl.BlockSpec((None, dilation, tm // dilation, tn),
                               lambda i, j: (i // tiles_per_seq, 0, i % tiles_per_seq, j)),
        out_shape=jax.ShapeDtypeStruct((n_seq, dilation, seq // dilation, QKV_COLS), BF16),
        scratch_shapes=[pltpu.VMEM((tn // LANES, tm, LANES), F32)] * (
            0 if dilation == 1 else 1 if dilation <= MAX_ROW_STRIDE else 2),
        compiler_params=pltpu.CompilerParams(
            dimension_semantics=("parallel", "arbitrary"), vmem_limit_bytes=VMEM_LIMIT),
        name=f"inproj_qkv{group}",
    )(xb, w_in, b_in)


def _attn_kernel(cur_ref, prev_ref, next_ref, bias_ref, o_ref, lse_ref, *scratch, bq):
    s_scr, p_scr = scratch[:B_HEADS], scratch[B_HEADS:]
    jq = pl.program_id(2)
    nq = pl.num_programs(2)
    n_piece = bq // N_SIDE
    col = lax.broadcasted_iota(jnp.int32, (N_SIDE, 3 * N_SIDE), 1)
    lane = lax.broadcasted_iota(jnp.int32, (N_SIDE, LANES), 1)
    scale = np.float32(B_HEAD_DIM ** -0.5)

    def window(p, part, h):
        c0 = part * B_GROUP_WIDTH + h * B_HEAD_DIM
        cols = slice(c0, c0 + B_HEAD_DIM)
        if n_piece == 1:
            return jnp.concatenate([prev_ref[:, cols], cur_ref[:, cols], next_ref[:, cols]], axis=0)
        if p == 0:
            return jnp.concatenate([prev_ref[:, cols], cur_ref[0:2 * N_SIDE, cols]], axis=0)
        if p == n_piece - 1:
            return jnp.concatenate([cur_ref[bq - 2 * N_SIDE:bq, cols], next_ref[:, cols]], axis=0)
        return cur_ref[(p - 1) * N_SIDE:(p + 2) * N_SIDE, cols]

    def scores(p, h):
        qh = cur_ref[p * N_SIDE:(p + 1) * N_SIDE, h * B_HEAD_DIM:(h + 1) * B_HEAD_DIM]
        s = lax.dot_general(qh, window(p, 1, h), (((1,), (1,)), ((), ())), preferred_element_type=F32) * scale
        if p == 0:
            s = jnp.where(jnp.logical_and(col < N_SIDE, jq == 0), NEG_INF, s)
        if p == n_piece - 1:
            s = jnp.where(jnp.logical_and(col >= 2 * N_SIDE, jq == nq - 1), NEG_INF, s)
        return s + bias_ref[h]

    def softmax(s):
        m = jnp.max(s, axis=-1, keepdims=True)
        e = jnp.exp(s - m)
        den = jnp.sum(e, axis=-1, keepdims=True)
        return (e / den).astype(BF16), m + jnp.log(den)

    def values(p, h, pn):
        o = jnp.dot(pn, window(p, 2, h), preferred_element_type=F32)
        o_ref[p * N_SIDE:(p + 1) * N_SIDE, h * B_HEAD_DIM:(h + 1) * B_HEAD_DIM] = o.astype(BF16)

    lse = {}
    for t in range(B_HEADS + 2):
        if t < B_HEADS:
            for p in range(n_piece):
                s_scr[t][p * N_SIDE:(p + 1) * N_SIDE, :] = scores(p, t)
        if 1 <= t <= B_HEADS:
            for p in range(n_piece):
                rows = slice(p * N_SIDE, (p + 1) * N_SIDE)
                p_scr[t - 1][rows, :], lse[p, t - 1] = softmax(s_scr[t - 1][rows, :])
        if t >= 2:
            for p in range(n_piece):
                values(p, t - 2, p_scr[t - 2][p * N_SIDE:(p + 1) * N_SIDE, :])
    for p in range(n_piece):
        tile = jnp.zeros((N_SIDE, LANES), F32)
        for h in range(B_HEADS):
            tile = jnp.where(lane == h, lse[p, h], tile)
        lse_ref[p * N_SIDE:(p + 1) * N_SIDE, :] = tile


def _attn_bias(group):
    _, dilation = B_PATTERNS[group]
    slopes = np.array([2.0 ** (-8.0 * (group * B_HEADS + h + 1) / N_ATTN_HEADS) for h in range(B_HEADS)],
                      dtype=np.float32)
    a = np.arange(N_SIDE)[:, None]
    c = np.arange(3 * N_SIDE)[None, :]
    rel = c - N_SIDE - a
    alibi = (-(slopes[:, None, None] * (dilation * np.abs(rel))[None])).astype(np.float32)
    return np.where((np.abs(rel) <= N_SIDE)[None], alibi, np.float32(NEG_INF)).astype(np.float32)


def _attention(qkv, group, bq_max=512):
    n_seq, dilation, sub_len, _ = qkv.shape
    bq = min(bq_max, sub_len)
    nq = sub_len // bq
    halo_per_bq = bq // N_SIDE
    n_halo = sub_len // N_SIDE
    return pl.pallas_call(
        functools.partial(_attn_kernel, bq=bq),
        grid=(n_seq, dilation, nq),
        in_specs=[pl.BlockSpec((None, None, bq, QKV_COLS), lambda b, r, j: (b, r, j, 0)),
                  pl.BlockSpec((None, None, N_SIDE, QKV_COLS),
                               lambda b, r, j: (b, r, jnp.maximum(j * halo_per_bq - 1, 0), 0)),
                  pl.BlockSpec((None, None, N_SIDE, QKV_COLS),
                               lambda b, r, j: (b, r, jnp.minimum((j + 1) * halo_per_bq, n_halo - 1), 0)),
                  pl.BlockSpec((B_HEADS, N_SIDE, 3 * N_SIDE), lambda b, r, j: (0, 0, 0))],
        out_specs=[pl.BlockSpec((None, None, bq, B_GROUP_WIDTH), lambda b, r, j: (b, r, j, 0)),
                   pl.BlockSpec((None, None, bq, LANES), lambda b, r, j: (b, r, j, 0))],
        out_shape=[jax.ShapeDtypeStruct((n_seq, dilation, sub_len, B_GROUP_WIDTH), BF16),
                   jax.ShapeDtypeStruct((n_seq, dilation, sub_len, LANES), F32)],
        scratch_shapes=([pltpu.VMEM((bq, 3 * N_SIDE), F32)] * B_HEADS
                        + [pltpu.VMEM((bq, 3 * N_SIDE), BF16)] * B_HEADS),
        compiler_params=pltpu.CompilerParams(
            dimension_semantics=("parallel", "parallel", "arbitrary"), vmem_limit_bytes=VMEM_LIMIT),
        name=f"attn{group}",
    )(qkv, qkv, qkv, jnp.asarray(_attn_bias(group)))


def _mixer_kernel(u_ref, v_ref, ga_ref, gb_ref, o0_ref, o1_ref, o2_ref, l0_ref, l1_ref, l2_ref, x_ref,
                  lnvg_ref, lnvb_ref, ws_ref, bs_ref, wpa_ref, wpb_ref, wo_ref, ln1g_ref, ln1b_ref,
                  wr_ref, br_ref, x1_ref, x1b_ref, logit_ref, a_scr, b_scr, o_scr, l_scr, *, tm, dn_alpha):
    for g, (o_ref, l_ref) in enumerate(((o0_ref, l0_ref), (o1_ref, l1_ref), (o2_ref, l2_ref))):
        dilation = o_ref.shape[0]
        rows = tm // dilation
        for r in range(dilation):
            idx = pl.ds(r, rows, stride=dilation) if dilation > 1 else slice(None)
            for h in range(B_HEADS):
                o_scr[g, h, idx, :] = o_ref[r, :, h * B_HEAD_DIM:(h + 1) * B_HEAD_DIM].astype(F32)
            l_scr[g, idx, :] = l_ref[r]
    n_groups = len(B_PATTERNS)

    def branches(r0, nr):
        rows = slice(r0, r0 + nr)
        vn = _layer_norm(v_ref[rows, :].astype(F32), lnvg_ref[...], lnvb_ref[...]).astype(BF16)
        for c in range(nr // CHUNK):
            q0 = c * CHUNK
            for g in range(A_GROUPS):
                c0 = g * CHUNK
                mixed = jnp.dot(ws_ref[g], vn[q0:q0 + CHUNK, c0:c0 + CHUNK], preferred_element_type=F32)
                mixed = mixed + bs_ref[g]
                a = u_ref[r0 + q0:r0 + q0 + CHUNK, c0:c0 + CHUNK].astype(F32) * mixed
                a_scr[r0 + q0:r0 + q0 + CHUNK, c0:c0 + CHUNK] = a.astype(BF16)
        for h in range(B_HEADS):
            c0 = h * B_HEAD_DIM
            ls = [l_scr[g, rows, h:h + 1] for g in range(n_groups)]
            mx = jnp.maximum(jnp.maximum(ls[0], ls[1]), ls[2])
            es = [jnp.exp(l - mx) for l in ls]
            den = es[0] + es[1] + es[2]
            acc = jnp.zeros((nr, B_HEAD_DIM), F32)
            for g in range(n_groups):
                acc = acc + (es[g] / den) * o_scr[g, h, rows, :]
            b_scr[rows, c0:c0 + B_HEAD_DIM] = acc.astype(BF16)

    def project(r0, nr):
        rows = slice(r0, r0 + nr)
        pa = jnp.dot(a_scr[rows, :], wpa_ref[...], preferred_element_type=F32)
        pb = jnp.dot(b_scr[rows, :], wpb_ref[...], preferred_element_type=F32)
        merged = (jax.nn.sigmoid(ga_ref[rows, :].astype(F32)) * pa
                  + jax.nn.sigmoid(gb_ref[rows, :].astype(F32)) * pb)
        mix = jnp.dot(merged.astype(BF16), wo_ref[...], preferred_element_type=F32)
        x1 = _layer_norm(dn_alpha * x_ref[rows, :] + mix, ln1g_ref[...], ln1b_ref[...])
        x1_ref[rows, :] = x1
        x1b = x1.astype(BF16)
        x1b_ref[rows, :] = x1b
        logit_ref[rows, :] = jnp.dot(x1b, wr_ref[...], preferred_element_type=F32) + br_ref[...]

    half = tm // 2
    branches(0, half)
    project(0, half)
    branches(half, half)
    project(half, half)


def _mixer(z, attn, x, p, layer, dn_alpha, seq, tm=512):
    t = x.shape[0]
    tiles_per_seq = seq // tm

    def zspec(cb):
        return pl.BlockSpec((tm, D_MODEL), lambda i: (i, cb))

    def row(width):
        return pl.BlockSpec((tm, width), lambda i: (i, 0))

    def split(a):
        _, dilation, _, width = a.shape
        return pl.BlockSpec((None, dilation, tm // dilation, width),
                            lambda i: (i // tiles_per_seq, 0, i % tiles_per_seq, 0))

    def vec(width):
        return pl.BlockSpec((None, 1, width), lambda i: (layer, 0, 0))

    def mat(k, n):
        return pl.BlockSpec((None, k, n), lambda i: (layer, 0, 0))

    (o0, l0), (o1, l1), (o2, l2) = attn
    return pl.pallas_call(
        functools.partial(_mixer_kernel, tm=tm, dn_alpha=np.float32(dn_alpha)),
        grid=(t // tm,),
        in_specs=[zspec(0), zspec(1), zspec(2), zspec(3),
                  split(o0), split(o1), split(o2), split(l0), split(l1), split(l2), row(D_MODEL),
                  vec(D_MODEL), vec(D_MODEL),
                  pl.BlockSpec((None, A_GROUPS, CHUNK, CHUNK), lambda i: (layer, 0, 0, 0)),
                  pl.BlockSpec((None, A_GROUPS, CHUNK, CHUNK), lambda i: (layer, 0, 0, 0)),
                  mat(D_MODEL, D_MODEL), mat(B_GROUP_WIDTH, D_MODEL), mat(D_MODEL, D_MODEL),
                  vec(D_MODEL), vec(D_MODEL), mat(D_MODEL, LANES), vec(LANES)],
        out_specs=[row(D_MODEL), row(D_MODEL), row(LANES)],
        out_shape=[jax.ShapeDtypeStruct((t, D_MODEL), F32), jax.ShapeDtypeStruct((t, D_MODEL), BF16),
                   jax.ShapeDtypeStruct((t, LANES), F32)],
        scratch_shapes=[pltpu.VMEM((tm, D_MODEL), BF16), pltpu.VMEM((tm, B_GROUP_WIDTH), BF16),
                        pltpu.VMEM((len(B_PATTERNS), B_HEADS, tm, B_HEAD_DIM), F32),
                        pltpu.VMEM((len(B_PATTERNS), tm, LANES), F32)],
        compiler_params=pltpu.CompilerParams(
            dimension_semantics=("parallel",), vmem_limit_bytes=VMEM_LIMIT),
        name="mixer",
    )(z, z, z, z, o0, o1, o2, l0, l1, l2, x,
      p["ln_v_g"], p["ln_v_b"], p["w_s"], p["b_s"], p["w_pa"], p["w_pb"], p["w_o"],
      p["ln1_g"], p["ln1_b"], p["w_r"], p["b_r"])


TS = 512
CHUNK_ROWS = 16
REGION = TS * TOP_K + N_EXPERTS * CHUNK_ROWS
P_ROWS = 256


def _chunks_per_expert(cnt):
    return jnp.floor((cnt + (CHUNK_ROWS - 1)) * (1.0 / CHUNK_ROWS))


def _dispatch_kernel(logit_ref, x_ref, tri_ref, upper_ref, xs_ref, pos_ref, gate_ref, cnt_ref, p_scr):
    l = logit_ref[...]
    lane = lax.broadcasted_iota(jnp.int32, (TS, LANES), 1)
    onehots, vals = [], []
    for _ in range(TOP_K):
        mx = jnp.max(l, axis=-1, keepdims=True)
        idx = jnp.min(jnp.where(l == mx, lane, LANES), axis=-1, keepdims=True)
        oh = lane == idx
        l = jnp.where(oh, -jnp.inf, l)
        onehots.append(oh)
        vals.append(mx)
    es = [jnp.exp(v - vals[0]) for v in vals]
    den = es[0] + es[1] + es[2] + es[3]
    multi = jnp.zeros((TS, LANES), F32)
    for oh in onehots:
        multi = multi + oh.astype(F32)
    cnt = jnp.sum(multi, axis=0, keepdims=True)
    n_chunks = jnp.broadcast_to(_chunks_per_expert(cnt), (8, LANES)).astype(BF16)
    run_off = CHUNK_ROWS * jnp.dot(n_chunks, upper_ref[...], preferred_element_type=F32)[0:1, :]
    rank = jnp.dot(tri_ref[...], multi.astype(BF16), preferred_element_type=F32)
    pos = jnp.full((TS, LANES), -1.0, F32)
    gate = jnp.zeros((TS, LANES), F32)
    for k in range(TOP_K):
        pos_k = jnp.sum(jnp.where(onehots[k], rank + run_off, 0.0), axis=-1, keepdims=True)
        pos = jnp.where(lane == k, pos_k, pos)
        gate = jnp.where(lane == k, es[k] / den, gate)
    pos_ref[...] = pos.astype(jnp.int32)
    gate_ref[...] = gate
    cnt_ref[...] = jnp.broadcast_to(cnt, cnt_ref.shape).astype(jnp.int32)
    live = pl.program_id(0) < pl.num_programs(0) - 1
    pos_t = jnp.where(live, jnp.transpose(pos)[0:8, :], -1.0)
    row = lax.broadcasted_iota(jnp.int32, (P_ROWS, TS), 0).astype(F32).astype(BF16)
    for c in range(REGION // P_ROWS):
        rel = pos_t - float(c * P_ROWS)
        rel = jnp.where(jnp.logical_and(rel >= 0.0, rel < float(P_ROWS)), rel, -1.0).astype(BF16)
        perm = jnp.zeros((P_ROWS, TS), BF16)
        for k in range(TOP_K):
            perm = jnp.where(row == rel[k:k + 1, :], jnp.ones((), BF16), perm)
        xs_ref[c * P_ROWS:(c + 1) * P_ROWS, :] = jnp.dot(
            perm, x_ref[...], preferred_element_type=F32).astype(BF16)


def _dispatch(logits, x1b):
    t = logits.shape[0]
    n_tiles = t // TS
    tri = jnp.asarray(np.tril(np.ones((TS, TS), np.float32), -1), dtype=BF16)
    upper = jnp.asarray(np.triu(np.ones((LANES, LANES), np.float32), 1), dtype=BF16)

    def tile(i):
        return jnp.minimum(i, n_tiles - 1)

    return pl.pallas_call(
        _dispatch_kernel,
        grid=(n_tiles + 1,),
        in_specs=[pl.BlockSpec((TS, LANES), lambda i: (tile(i), 0)),
                  pl.BlockSpec((TS, D_MODEL), lambda i: (tile(i), 0)),
                  pl.BlockSpec((TS, TS), lambda i: (0, 0)),
                  pl.BlockSpec((LANES, LANES), lambda i: (0, 0))],
        out_specs=[pl.BlockSpec((REGION, D_MODEL), lambda i: (i, 0)),
                   pl.BlockSpec((TS, LANES), lambda i: (tile(i), 0)),
                   pl.BlockSpec((TS, LANES), lambda i: (tile(i), 0)),
                   pl.BlockSpec((None, 8, LANES), lambda i: (tile(i), 0, 0))],
        out_shape=[jax.ShapeDtypeStruct(((n_tiles + 1) * REGION, D_MODEL), BF16),
                   jax.ShapeDtypeStruct((t, LANES), jnp.int32), jax.ShapeDtypeStruct((t, LANES), F32),
                   jax.ShapeDtypeStruct((n_tiles, 8, LANES), jnp.int32)],
        scratch_shapes=[pltpu.VMEM((REGION, TS), BF16)],
        compiler_params=pltpu.CompilerParams(
            dimension_semantics=("arbitrary",), vmem_limit_bytes=VMEM_LIMIT),
        name="dispatch",
    )(logits, x1b, tri, upper)


def _chunk_plan(cnt, bm):
    assert REGION >= 3 * bm
    n_tiles = cnt.shape[0]
    ch = bm // CHUNK_ROWS
    n_rows_max = n_tiles * REGION + N_EXPERTS * (bm - CHUNK_ROWS)
    n_blk = -(-n_rows_max // bm)
    run_chunks = (cnt + CHUNK_ROWS - 1) // CHUNK_ROWS
    run_off = (jnp.cumsum(run_chunks, axis=1) - run_chunks) * CHUNK_ROWS
    rows_e = jnp.sum(run_chunks, axis=0) * CHUNK_ROWS
    padded_e = (rows_e + bm - 1) // bm * bm
    pend = jnp.cumsum(padded_e)
    pstart = pend - padded_e
    runs_t = run_chunks.T
    run_start = pstart[:, None] // CHUNK_ROWS + jnp.cumsum(runs_t, axis=1) - runs_t
    run_row = jnp.arange(n_tiles, dtype=jnp.int32)[None, :] * REGION + run_off.T
    blk = jnp.arange(n_blk + 1, dtype=jnp.int32)
    blk_e = jnp.minimum(jnp.sum((pend[None, :] <= blk[:, None] * bm).astype(jnp.int32), axis=1), N_EXPERTS - 1)
    is_e = blk_e[:, None, None] == jnp.arange(N_EXPERTS, dtype=jnp.int32)[None, :, None]
    c = (blk[:, None] * ch + jnp.arange(ch, dtype=jnp.int32)[None, :]).reshape(-1)

    def of_run(v):
        v_b = jnp.sum(jnp.where(is_e, v[None], 0), axis=1)
        step = v_b - jnp.concatenate([jnp.zeros_like(v_b[:, :1]), v_b[:, :-1]], axis=1)
        return jnp.sum(jnp.where(started, step[:, None, :], 0), axis=-1).reshape(-1)

    start_b = jnp.sum(jnp.where(is_e, run_start[None], 0), axis=1)
    started = start_b[:, None, :] <= c.reshape(n_blk + 1, ch)[:, :, None]
    j = c - of_run(run_start)
    valid = j < of_run(runs_t)
    row = of_run(run_row) + j * CHUNK_ROWS
    spare = n_tiles * REGION + (c % ch) * CHUNK_ROWS
    src = jnp.where(valid, row, spare + 2 * bm).astype(jnp.int32).reshape(n_blk + 1, 1, ch)
    dst = jnp.where(valid, row, spare + ((c // ch) % 2) * bm).astype(jnp.int32).reshape(n_blk + 1, 1, ch)
    dst = jnp.concatenate([dst[n_blk:], dst[:n_blk]])
    blk_e = blk_e[:n_blk].astype(jnp.int32)
    n_used = (pend[-1:] // bm).astype(jnp.int32)
    return src, dst, blk_e, n_used


def _ffn_kernel(blk_e_ref, n_used_ref, src0_ref, srcn_ref, dstp_ref, dstc_ref, xs_hbm, wgu_ref, bgu_ref, wd_ref,
                bd_ref, ys_hbm, xbuf, ybuf, wgu_b, wd_b, in_sem, out_sem, *, bm):
    i = pl.program_id(0)
    n_used = n_used_ref[0]
    slot = i % 2
    ch = bm // CHUNK_ROWS

    def gather(tbl_ref, s):
        for j in range(ch):
            row = pl.multiple_of(tbl_ref[0, j], CHUNK_ROWS)
            pltpu.make_async_copy(xs_hbm.at[pl.ds(row, CHUNK_ROWS)],
                                  xbuf.at[s, pl.ds(j * CHUNK_ROWS, CHUNK_ROWS)], in_sem.at[s]).start()

    def scatter(tbl_ref, s):
        for j in range(ch):
            row = pl.multiple_of(tbl_ref[0, j], CHUNK_ROWS)
            pltpu.make_async_copy(ybuf.at[s, pl.ds(j * CHUNK_ROWS, CHUNK_ROWS)],
                                  ys_hbm.at[pl.ds(row, CHUNK_ROWS)], out_sem.at[s]).start()

    def wait_in(s):
        pltpu.make_async_copy(xs_hbm.at[pl.ds(0, bm)], xbuf.at[s], in_sem.at[s]).wait()

    def wait_out(s):
        pltpu.make_async_copy(ybuf.at[s], ys_hbm.at[pl.ds(0, bm)], out_sem.at[s]).wait()

    used = i < n_used
    e_changed = jnp.logical_or(i == 0, blk_e_ref[i] != blk_e_ref[jnp.maximum(i - 1, 0)])

    @pl.when(jnp.logical_and(i == 0, used))
    def _():
        gather(src0_ref, 0)
        ybuf[1] = jnp.zeros((bm, D_MODEL), BF16)

    @pl.when(jnp.logical_and(used, e_changed))
    def _():
        wgu_b[...] = wgu_ref[...].astype(BF16)
        wd_b[...] = wd_ref[...].astype(BF16)

    @pl.when(jnp.logical_and(used, i >= 1))
    def _():
        wait_out(slot)

    @pl.when(used)
    def _():
        wait_in(slot)
        gu = jnp.dot(xbuf[slot], wgu_b[...], preferred_element_type=F32) + bgu_ref[...]
        gather(srcn_ref, 1 - slot)
        scatter(dstp_ref, 1 - slot)
        gate = jnp.minimum(gu[:, :D_MODEL], SWIGLU_LIMIT)
        up = jnp.clip(gu[:, D_MODEL:], -SWIGLU_LIMIT, SWIGLU_LIMIT)
        h = (up + 1.0) * gate * jax.nn.sigmoid(SWIGLU_ALPHA * gate)
        y = jnp.dot(h.astype(BF16), wd_b[...], preferred_element_type=F32) + bd_ref[...]
        ybuf[slot] = y.astype(BF16)

    @pl.when(i == n_used - 1)
    def _():
        scatter(dstc_ref, slot)
        wait_in(1 - slot)
        wait_out(1 - slot)
        wait_out(slot)


def _ffn(xs, src, dst, blk_e, n_used, p, layer, bm):
    n_blk = blk_e.shape[0]
    ch = src.shape[-1]
    smem = functools.partial(pl.BlockSpec, memory_space=pltpu.SMEM)
    grid_spec = pltpu.PrefetchScalarGridSpec(
        num_scalar_prefetch=2,
        grid=(n_blk,),
        in_specs=[smem((None, 1, ch), lambda i, be, nu: (0, 0, 0)),
                  smem((None, 1, ch), lambda i, be, nu: (i + 1, 0, 0)),
                  smem((None, 1, ch), lambda i, be, nu: (i, 0, 0)),
                  smem((None, 1, ch), lambda i, be, nu: (i + 1, 0, 0)),
                  pl.BlockSpec(memory_space=pl.ANY),
                  pl.BlockSpec((None, None, D_MODEL, 2 * D_MODEL), lambda i, be, nu: (layer, be[i], 0, 0)),
                  pl.BlockSpec((None, None, 1, 2 * D_MODEL), lambda i, be, nu: (layer, be[i], 0, 0)),
                  pl.BlockSpec((None, None, D_MODEL, D_MODEL), lambda i, be, nu: (layer, be[i], 0, 0)),
                  pl.BlockSpec((None, None, 1, D_MODEL), lambda i, be, nu: (layer, be[i], 0, 0))],
        out_specs=pl.BlockSpec(memory_space=pl.ANY),
        scratch_shapes=[pltpu.VMEM((2, bm, D_MODEL), BF16), pltpu.VMEM((2, bm, D_MODEL), BF16),
                        pltpu.VMEM((D_MODEL, 2 * D_MODEL), BF16), pltpu.VMEM((D_MODEL, D_MODEL), BF16),
                        pltpu.SemaphoreType.DMA((2,)), pltpu.SemaphoreType.DMA((2,))],
    )
    return pl.pallas_call(
        functools.partial(_ffn_kernel, bm=bm),
        grid_spec=grid_spec,
        out_shape=jax.ShapeDtypeStruct(xs.shape, BF16),
        input_output_aliases={6: 0},
        compiler_params=pltpu.CompilerParams(
            dimension_semantics=("arbitrary",), vmem_limit_bytes=VMEM_LIMIT),
        name="ffn",
    )(blk_e, n_used, src, src, dst, dst, xs, p["w_gu"], p["b_gu"], p["w_down"], p["b_down"])


def _combine_kernel(ys_ref, pos_ref, gate_ref, x_ref, g_ref, b_ref, x2_ref, x2b_ref, g_scr, *, dn_alpha):
    lane = lax.broadcasted_iota(jnp.int32, (TS, LANES), 1)
    tile_of, gate = [], []
    for k in range(TOP_K):
        rel = jnp.broadcast_to(pos_ref[:, k:k + 1], (TS, LANES)) - lane
        hit = jnp.where((rel & (LANES - 1)) == 0, rel >> (LANES.bit_length() - 1), -1)
        tile_of.append(hit.astype(F32).astype(BF16))
        gate.append(jnp.broadcast_to(gate_ref[:, k:k + 1], (TS, LANES)).astype(BF16))
    for c in range(REGION // LANES):
        g = jnp.zeros((TS, LANES), BF16)
        for k in range(TOP_K):
            g = jnp.where(tile_of[k] == float(c), gate[k], g)
        g_scr[:, c * LANES:(c + 1) * LANES] = g
    y = jnp.dot(g_scr[...], ys_ref[...], preferred_element_type=F32)
    x2 = _layer_norm(dn_alpha * x_ref[...] + y, g_ref[...], b_ref[...])
    x2_ref[...] = x2
    x2b_ref[...] = x2.astype(BF16)


def _combine(ys, pos, gates, x1, p, layer, dn_alpha):
    t = x1.shape[0]
    return pl.pallas_call(
        functools.partial(_combine_kernel, dn_alpha=np.float32(dn_alpha)),
        grid=(t // TS,),
        in_specs=[pl.BlockSpec((REGION, D_MODEL), lambda i: (i, 0)),
                  pl.BlockSpec((TS, LANES), lambda i: (i, 0)),
                  pl.BlockSpec((TS, LANES), lambda i: (i, 0)),
                  pl.BlockSpec((TS, D_MODEL), lambda i: (i, 0)),
                  pl.BlockSpec((None, 1, D_MODEL), lambda i: (layer, 0, 0)),
                  pl.BlockSpec((None, 1, D_MODEL), lambda i: (layer, 0, 0))],
        out_specs=[pl.BlockSpec((TS, D_MODEL), lambda i: (i, 0)),
                   pl.BlockSpec((TS, D_MODEL), lambda i: (i, 0))],
        out_shape=[jax.ShapeDtypeStruct((t, D_MODEL), F32), jax.ShapeDtypeStruct((t, D_MODEL), BF16)],
        scratch_shapes=[pltpu.VMEM((TS, REGION), BF16)],
        compiler_params=pltpu.CompilerParams(
            dimension_semantics=("parallel",), vmem_limit_bytes=VMEM_LIMIT),
        name="combine",
    )(ys, pos, gates, x1, p["ln2_g"], p["ln2_b"])


def _prepare_params(w_in, b_in, ln_v_g, ln_v_b, w_s, b_s, w_pa, w_pb, w_o, ln1_g, ln1_b,
                    w_r, b_r, w_gu, b_gu, w_down, b_down, ln2_g, ln2_b):
    depth = w_in.shape[0]
    n_qkv = 3 * len(B_PATTERNS) * B_GROUP_WIDTH
    qkv0, qkv1 = 2 * D_MODEL, 2 * D_MODEL + n_qkv

    def permute_cols(a):
        gap = jnp.zeros(a.shape[:-1] + (MAIN_COL0 - n_qkv,), a.dtype)
        return jnp.concatenate([a[..., qkv0:qkv1], gap, a[..., :qkv0], a[..., qkv1:]], axis=-1)

    def vec(a):
        return a.reshape(depth, 1, a.shape[-1]).astype(F32)

    w_r_pad = jnp.pad(w_r, ((0, 0), (0, 0), (0, LANES - N_EXPERTS))).astype(BF16)
    b_r_pad = jnp.pad(b_r.astype(F32), ((0, 0), (0, LANES - N_EXPERTS)), constant_values=NEG_INF)
    return dict(
        w_in=permute_cols(w_in).astype(BF16), b_in=vec(permute_cols(b_in)),
        ln_v_g=vec(ln_v_g), ln_v_b=vec(ln_v_b),
        w_s=w_s.astype(BF16),
        b_s=jnp.broadcast_to(b_s.astype(F32)[..., None], b_s.shape + (CHUNK,)),
        w_pa=w_pa.astype(BF16), w_pb=w_pb.astype(BF16), w_o=w_o.astype(BF16),
        ln1_g=vec(ln1_g), ln1_b=vec(ln1_b),
        w_r=w_r_pad, b_r=vec(b_r_pad),
        w_gu=w_gu, b_gu=b_gu.astype(F32)[:, :, None, :],
        w_down=w_down, b_down=b_down.astype(F32)[:, :, None, :],
        ln2_g=vec(ln2_g), ln2_b=vec(ln2_b),
    )


def _trunk(x, params, bm=512):
    n_seq, seq, _ = x.shape
    t = n_seq * seq
    depth = params["w_in"].shape[0]
    dn_alpha = (2.0 * depth) ** 0.25
    x = x.reshape(t, D_MODEL)
    xb = x.astype(BF16)
    for layer in range(depth):
        z = _inproj_main(xb, params["w_in"], params["b_in"], layer)
        attn = [_attention(_inproj_qkv(xb, params["w_in"], params["b_in"], layer, g, n_seq, seq), g)
                for g in range(len(B_PATTERNS))]
        x1, x1b, logits = _mixer(z, attn, x, params, layer, dn_alpha, seq)
        xs, pos, gates, cnt = _dispatch(logits, x1b)
        src, dst, blk_e, n_used = _chunk_plan(cnt[:, 0, :N_EXPERTS], bm)
        ys = _ffn(xs, src, dst, blk_e, n_used, params, layer, bm)
        x, xb = _combine(ys, pos, gates, x1, params, layer, dn_alpha)
    return x.reshape(n_seq, seq, D_MODEL)


def kernel(x_prompt, x_sample, w_in, b_in, ln_v_g, ln_v_b, w_s, b_s, w_pa, w_pb, w_o, ln1_g, ln1_b,
           w_r, b_r, w_gu, b_gu, w_down, b_down, ln2_g, ln2_b):
    params = _prepare_params(w_in, b_in, ln_v_g, ln_v_b, w_s, b_s, w_pa, w_pb, w_o, ln1_g, ln1_b,
                             w_r, b_r, w_gu, b_gu, w_down, b_down, ln2_g, ln2_b)
    n_prompt = x_prompt.shape[0]
    y = _trunk(jnp.concatenate([x_prompt, x_sample], axis=0), params)
    return (y[:n_prompt], y[n_prompt:])
```

```python
import functools

import numpy as np
import jax
import jax.numpy as jnp
from jax import lax
from jax.experimental import pallas as pl
from jax.experimental.pallas import tpu as pltpu

D_MODEL = 1024
CHUNK = 128
A_GROUPS = 8
B_PATTERNS = ((128, 1), (512, 4), (2048, 16))
B_HEADS = 4
B_HEAD_DIM = 128
B_GROUP_WIDTH = B_HEADS * B_HEAD_DIM
N_ATTN_HEADS = len(B_PATTERNS) * B_HEADS
N_SIDE = 64
D_IN = 4 * D_MODEL + 3 * len(B_PATTERNS) * B_GROUP_WIDTH
N_EXPERTS = 32
TOP_K = 4
SWIGLU_LIMIT = 7.0
SWIGLU_ALPHA = 1.702
LN_EPS = 1e-5
NEG_INF = -1e30
LANES = 128

MAIN_COLS = 4 * D_MODEL
QKV_COLS = 3 * B_GROUP_WIDTH
MAIN_COL0 = -(-len(B_PATTERNS) * QKV_COLS // D_MODEL) * D_MODEL
GELU_COLS = 2 * D_MODEL

VMEM_LIMIT = 56 * 1024 * 1024

BF16 = jnp.bfloat16
F32 = jnp.float32


def _gelu(x):
    return 0.5 * x * (1.0 + lax.erf(x * np.float32(np.sqrt(0.5))))


def _layer_norm(x, g, b):
    xc = x - jnp.mean(x, axis=-1, keepdims=True)
    var = jnp.mean(xc * xc, axis=-1, keepdims=True)
    return xc * lax.rsqrt(var + LN_EPS) * g + b


N_CHUNK = 256
MAX_ROW_STRIDE = 4


def _inproj_main_kernel(x_ref, w_ref, b_ref, z_ref, *, n_gelu_tiles):
    j = pl.program_id(1)

    def run(act):
        for c in range(z_ref.shape[1] // N_CHUNK):
            cols = slice(c * N_CHUNK, (c + 1) * N_CHUNK)
            acc = jnp.dot(x_ref[...], w_ref[:, cols], preferred_element_type=F32) + b_ref[:, cols]
            z_ref[:, cols] = act(acc).astype(BF16)

    @pl.when(j < n_gelu_tiles)
    def _():
        run(_gelu)

    @pl.when(j >= n_gelu_tiles)
    def _():
        run(lambda a: a)


def _inproj_main(xb, w_in, b_in, layer, tm=1024, tn=1024):
    t = xb.shape[0]
    return pl.pallas_call(
        functools.partial(_inproj_main_kernel, n_gelu_tiles=GELU_COLS // tn),
        grid=(t // tm, MAIN_COLS // tn),
        in_specs=[
            pl.BlockSpec((tm, D_MODEL), lambda i, j: (i, 0)),
            pl.BlockSpec((None, D_MODEL, tn), lambda i, j: (layer, 0, MAIN_COL0 // tn + j)),
            pl.BlockSpec((None, 1, tn), lambda i, j: (layer, 0, MAIN_COL0 // tn + j)),
        ],
        out_specs=pl.BlockSpec((tm, tn), lambda i, j: (i, j)),
        out_shape=jax.ShapeDtypeStruct((t, MAIN_COLS), BF16),
        compiler_params=pltpu.CompilerParams(
            dimension_semantics=("parallel", "arbitrary"), vmem_limit_bytes=VMEM_LIMIT),
        name="inproj_main",
    )(xb, w_in, b_in)


def _inproj_qkv_kernel(x_ref, w_ref, b_ref, o_ref, *scratch, dilation):
    rows = x_ref.shape[0] // dilation
    for c in range(o_ref.shape[-1] // N_CHUNK):
        cols = slice(c * N_CHUNK, (c + 1) * N_CHUNK)
        acc = jnp.dot(x_ref[...], w_ref[:, cols], preferred_element_type=F32) + b_ref[:, cols]
        if dilation == 1:
            o_ref[0, :, cols] = acc.astype(BF16)
            continue
        scr, scr2 = scratch[0], scratch[-1]
        d1 = min(dilation, MAX_ROW_STRIDE)
        d2 = dilation // d1
        for q in range(c * N_CHUNK // LANES, (c + 1) * N_CHUNK // LANES):
            scr[q] = acc[:, q * LANES - c * N_CHUNK:(q + 1) * LANES - c * N_CHUNK]
            for r1 in range(d1):
                part = scr[q, pl.ds(r1, rows * d2, stride=d1), :]
                if d2 == 1:
                    o_ref[r1, :, q * LANES:(q + 1) * LANES] = part.astype(BF16)
                    continue
                scr2[q, r1 * rows * d2:(r1 + 1) * rows * d2, :] = part
                for r2 in range(d2):
                    o_ref[r2 * d1 + r1, :, q * LANES:(q + 1) * LANES] = (
                        scr2[q, pl.ds(r1 * rows * d2 + r2, rows, stride=d2), :].astype(BF16))


def _inproj_qkv(xb, w_in, b_in, layer, group, n_seq, seq, tm=1024, tn=QKV_COLS):
    _, dilation = B_PATTERNS[group]
    tiles_per_seq = seq // tm
    col0 = group * QKV_COLS // tn
    return pl.pallas_call(
        functools.partial(_inproj_qkv_kernel, dilation=dilation),
        grid=(n_seq * tiles_per_seq, QKV_COLS // tn),
        in_specs=[
            pl.BlockSpec((tm, D_MODEL), lambda i, j: (i, 0)),
            pl.BlockSpec((None, D_MODEL, tn), lambda i, j: (layer, 0, col0 + j)),
            pl.BlockSpec((None, 1, tn), lambda i, j: (layer, 0, col0 + j)),
        ],
        out_specs=pl.BlockSpec((None, dilation, tm // dilation, tn),
                               lambda i, j: (i // tiles_per_seq, 0, i % tiles_per_seq, j)),
        out_shape=jax.ShapeDtypeStruct((n_seq, dilation, seq // dilation, QKV_COLS), BF16),
        scratch_shapes=[pltpu.VMEM((tn // LANES, tm, LANES), F32)] * (
            0 if dilation == 1 else 1 if dilation <= MAX_ROW_STRIDE else 2),
        compiler_params=pltpu.CompilerParams(
            dimension_semantics=("parallel", "arbitrary"), vmem_limit_bytes=VMEM_LIMIT),
        name=f"inproj_qkv{group}",
    )(xb, w_in, b_in)


def _attn_kernel(cur_ref, prev_ref, next_ref, bias_ref, o_ref, lse_ref, *scratch, bq):
    for rb in range(cur_ref.shape[0]):
        _attn_rows(cur_ref.at[rb], prev_ref.at[rb], next_ref.at[rb], bias_ref, o_ref.at[rb], lse_ref.at[rb],
                   scratch, bq)


def _attn_rows(cur_ref, prev_ref, next_ref, bias_ref, o_ref, lse_ref, scratch, bq):
    s_scr, p_scr = scratch[:B_HEADS], scratch[B_HEADS:]
    jq = pl.program_id(2)
    nq = pl.num_programs(2)
    n_piece = bq // N_SIDE
    col = lax.broadcasted_iota(jnp.int32, (N_SIDE, 3 * N_SIDE), 1)
    lane = lax.broadcasted_iota(jnp.int32, (N_SIDE, LANES), 1)
    scale = np.float32(B_HEAD_DIM ** -0.5)

    def window(p, part, h):
        c0 = part * B_GROUP_WIDTH + h * B_HEAD_DIM
        cols = slice(c0, c0 + B_HEAD_DIM)
        if n_piece == 1:
            return jnp.concatenate([prev_ref[:, cols], cur_ref[:, cols], next_ref[:, cols]], axis=0)
        if p == 0:
            return jnp.concatenate([prev_ref[:, cols], cur_ref[0:2 * N_SIDE, cols]], axis=0)
        if p == n_piece - 1:
            return jnp.concatenate([cur_ref[bq - 2 * N_SIDE:bq, cols], next_ref[:, cols]], axis=0)
        return cur_ref[(p - 1) * N_SIDE:(p + 2) * N_SIDE, cols]

    def scores(p, h):
        qh = cur_ref[p * N_SIDE:(p + 1) * N_SIDE, h * B_HEAD_DIM:(h + 1) * B_HEAD_DIM]
        s = lax.dot_general(qh, window(p, 1, h), (((1,), (1,)), ((), ())), preferred_element_type=F32) * scale
        if p == 0:
            s = jnp.where(jnp.logical_and(col < N_SIDE, jq == 0), NEG_INF, s)
        if p == n_piece - 1:
            s = jnp.where(jnp.logical_and(col >= 2 * N_SIDE, jq == nq - 1), NEG_INF, s)
        return s + bias_ref[h]

    def softmax(s):
        m = jnp.max(s, axis=-1, keepdims=True)
        e = jnp.exp(s - m)
        den = jnp.sum(e, axis=-1, keepdims=True)
        return (e / den).astype(BF16), m + jnp.log(den)

    def values(p, h, pn):
        o = jnp.dot(pn, window(p, 2, h), preferred_element_type=F32)
        o_ref[p * N_SIDE:(p + 1) * N_SIDE, h * B_HEAD_DIM:(h + 1) * B_HEAD_DIM] = o.astype(BF16)

    lse = {}
    for t in range(B_HEADS + 2):
        if t < B_HEADS:
            for p in range(n_piece):
                s_scr[t][p * N_SIDE:(p + 1) * N_SIDE, :] = scores(p, t)
        if 1 <= t <= B_HEADS:
            for p in range(n_piece):
                rows = slice(p * N_SIDE, (p + 1) * N_SIDE)
                p_scr[t - 1][rows, :], lse[p, t - 1] = softmax(s_scr[t - 1][rows, :])
        if t >= 2:
            for p in range(n_piece):
                values(p, t - 2, p_scr[t - 2][p * N_SIDE:(p + 1) * N_SIDE, :])
    for p in range(n_piece):
        tile = jnp.zeros((N_SIDE, LANES), F32)
        for h in range(B_HEADS):
            tile = jnp.where(lane == h, lse[p, h], tile)
        lse_ref[p * N_SIDE:(p + 1) * N_SIDE, :] = tile


def _attn_bias(group):
    _, dilation = B_PATTERNS[group]
    slopes = np.array([2.0 ** (-8.0 * (group * B_HEADS + h + 1) / N_ATTN_HEADS) for h in range(B_HEADS)],
                      dtype=np.float32)
    a = np.arange(N_SIDE)[:, None]
    c = np.arange(3 * N_SIDE)[None, :]
    rel = c - N_SIDE - a
    alibi = (-(slopes[:, None, None] * (dilation * np.abs(rel))[None])).astype(np.float32)
    return np.where((np.abs(rel) <= N_SIDE)[None], alibi, np.float32(NEG_INF)).astype(np.float32)


def _attention(qkv, group, rows_per_step=1024):
    n_seq, dilation, sub_len, _ = qkv.shape
    bq = min(rows_per_step, sub_len)
    nq = sub_len // bq
    rb = min(dilation, rows_per_step // bq)
    halo_per_bq = bq // N_SIDE
    n_halo = sub_len // N_SIDE
    return pl.pallas_call(
        functools.partial(_attn_kernel, bq=bq),
        grid=(n_seq, dilation // rb, nq),
        in_specs=[pl.BlockSpec((None, rb, bq, QKV_COLS), lambda b, r, j: (b, r, j, 0)),
                  pl.BlockSpec((None, rb, N_SIDE, QKV_COLS),
                               lambda b, r, j: (b, r, jnp.maximum(j * halo_per_bq - 1, 0), 0)),
                  pl.BlockSpec((None, rb, N_SIDE, QKV_COLS),
                               lambda b, r, j: (b, r, jnp.minimum((j + 1) * halo_per_bq, n_halo - 1), 0)),
                  pl.BlockSpec((B_HEADS, N_SIDE, 3 * N_SIDE), lambda b, r, j: (0, 0, 0))],
        out_specs=[pl.BlockSpec((None, rb, bq, B_GROUP_WIDTH), lambda b, r, j: (b, r, j, 0)),
                   pl.BlockSpec((None, rb, bq, LANES), lambda b, r, j: (b, r, j, 0))],
        out_shape=[jax.ShapeDtypeStruct((n_seq, dilation, sub_len, B_GROUP_WIDTH), BF16),
                   jax.ShapeDtypeStruct((n_seq, dilation, sub_len, LANES), F32)],
        scratch_shapes=([pltpu.VMEM((bq, 3 * N_SIDE), F32)] * B_HEADS
                        + [pltpu.VMEM((bq, 3 * N_SIDE), BF16)] * B_HEADS),
        compiler_params=pltpu.CompilerParams(
            dimension_semantics=("parallel", "parallel", "arbitrary"), vmem_limit_bytes=VMEM_LIMIT),
        name=f"attn{group}",
    )(qkv, qkv, qkv, jnp.asarray(_attn_bias(group)))


def _mixer_kernel(u_ref, v_ref, ga_ref, gb_ref, o0_ref, o1_ref, o2_ref, l0_ref, l1_ref, l2_ref, x_ref,
                  lnvg_ref, lnvb_ref, ws_ref, bs_ref, wpa_ref, wpb_ref, wo_ref, ln1g_ref, ln1b_ref,
                  wr_ref, br_ref, x1_ref, x1b_ref, logit_ref, a_scr, b_scr, o_scr, l_scr, *, tm, dn_alpha):
    for g, (o_ref, l_ref) in enumerate(((o0_ref, l0_ref), (o1_ref, l1_ref), (o2_ref, l2_ref))):
        dilation = o_ref.shape[0]
        rows = tm // dilation
        for r in range(dilation):
            idx = pl.ds(r, rows, stride=dilation) if dilation > 1 else slice(None)
            for h in range(B_HEADS):
                o_scr[g, h, idx, :] = o_ref[r, :, h * B_HEAD_DIM:(h + 1) * B_HEAD_DIM].astype(F32)
            l_scr[g, idx, :] = l_ref[r]
    n_groups = len(B_PATTERNS)

    def branches(r0, nr):
        rows = slice(r0, r0 + nr)
        vn = _layer_norm(v_ref[rows, :].astype(F32), lnvg_ref[...], lnvb_ref[...]).astype(BF16)
        for c in range(nr // CHUNK):
            q0 = c * CHUNK
            for g in range(A_GROUPS):
                c0 = g * CHUNK
                mixed = jnp.dot(ws_ref[g], vn[q0:q0 + CHUNK, c0:c0 + CHUNK], preferred_element_type=F32)
                mixed = mixed + bs_ref[g]
                a = u_ref[r0 + q0:r0 + q0 + CHUNK, c0:c0 + CHUNK].astype(F32) * mixed
                a_scr[r0 + q0:r0 + q0 + CHUNK, c0:c0 + CHUNK] = a.astype(BF16)
        for h in range(B_HEADS):
            c0 = h * B_HEAD_DIM
            ls = [l_scr[g, rows, h:h + 1] for g in range(n_groups)]
            mx = jnp.maximum(jnp.maximum(ls[0], ls[1]), ls[2])
            es = [jnp.exp(l - mx) for l in ls]
            den = es[0] + es[1] + es[2]
            acc = jnp.zeros((nr, B_HEAD_DIM), F32)
            for g in range(n_groups):
                acc = acc + (es[g] / den) * o_scr[g, h, rows, :]
            b_scr[rows, c0:c0 + B_HEAD_DIM] = acc.astype(BF16)

    def project(r0, nr):
        rows = slice(r0, r0 + nr)
        pa = jnp.dot(a_scr[rows, :], wpa_ref[...], preferred_element_type=F32)
        pb = jnp.dot(b_scr[rows, :], wpb_ref[...], preferred_element_type=F32)
        merged = (jax.nn.sigmoid(ga_ref[rows, :].astype(F32)) * pa
                  + jax.nn.sigmoid(gb_ref[rows, :].astype(F32)) * pb)
        mix = jnp.dot(merged.astype(BF16), wo_ref[...], preferred_element_type=F32)
        x1 = _layer_norm(dn_alpha * x_ref[rows, :] + mix, ln1g_ref[...], ln1b_ref[...])
        x1_ref[rows, :] = x1
        x1b = x1.astype(BF16)
        x1b_ref[rows, :] = x1b
        logit_ref[rows, :] = jnp.dot(x1b, wr_ref[...], preferred_element_type=F32) + br_ref[...]

    half = tm // 2
    branches(0, half)
    project(0, half)
    branches(half, half)
    project(half, half)


def _mixer(z, attn, x, p, layer, dn_alpha, seq, tm=512):
    t = x.shape[0]
    tiles_per_seq = seq // tm

    def zspec(cb):
        return pl.BlockSpec((tm, D_MODEL), lambda i: (i, cb))

    def row(width):
        return pl.BlockSpec((tm, width), lambda i: (i, 0))

    def split(a):
        _, dilation, _, width = a.shape
        return pl.BlockSpec((None, dilation, tm // dilation, width),
                            lambda i: (i // tiles_per_seq, 0, i % tiles_per_seq, 0))

    def vec(width):
        return pl.BlockSpec((None, 1, width), lambda i: (layer, 0, 0))

    def mat(k, n):
        return pl.BlockSpec((None, k, n), lambda i: (layer, 0, 0))

    (o0, l0), (o1, l1), (o2, l2) = attn
    return pl.pallas_call(
        functools.partial(_mixer_kernel, tm=tm, dn_alpha=np.float32(dn_alpha)),
        grid=(t // tm,),
        in_specs=[zspec(0), zspec(1), zspec(2), zspec(3),
                  split(o0), split(o1), split(o2), split(l0), split(l1), split(l2), row(D_MODEL),
                  vec(D_MODEL), vec(D_MODEL),
                  pl.BlockSpec((None, A_GROUPS, CHUNK, CHUNK), lambda i: (layer, 0, 0, 0)),
                  pl.BlockSpec((None, A_GROUPS, CHUNK, CHUNK), lambda i: (layer, 0, 0, 0)),
                  mat(D_MODEL, D_MODEL), mat(B_GROUP_WIDTH, D_MODEL), mat(D_MODEL, D_MODEL),
                  vec(D_MODEL), vec(D_MODEL), mat(D_MODEL, LANES), vec(LANES)],
        out_specs=[row(D_MODEL), row(D_MODEL), row(LANES)],
        out_shape=[jax.ShapeDtypeStruct((t, D_MODEL), F32), jax.ShapeDtypeStruct((t, D_MODEL), BF16),
                   jax.ShapeDtypeStruct((t, LANES), F32)],
        scratch_shapes=[pltpu.VMEM((tm, D_MODEL), BF16), pltpu.VMEM((tm, B_GROUP_WIDTH), BF16),
                        pltpu.VMEM((len(B_PATTERNS), B_HEADS, tm, B_HEAD_DIM), F32),
                        pltpu.VMEM((len(B_PATTERNS), tm, LANES), F32)],
        compiler_params=pltpu.CompilerParams(
            dimension_semantics=("parallel",), vmem_limit_bytes=VMEM_LIMIT),
        name="mixer",
    )(z, z, z, z, o0, o1, o2, l0, l1, l2, x,
      p["ln_v_g"], p["ln_v_b"], p["w_s"], p["b_s"], p["w_pa"], p["w_pb"], p["w_o"],
      p["ln1_g"], p["ln1_b"], p["w_r"], p["b_r"])


TS = 512
CHUNK_ROWS = 16
REGION = TS * TOP_K + N_EXPERTS * CHUNK_ROWS
P_ROWS = 256


def _chunks_per_expert(cnt):
    return jnp.floor((cnt + (CHUNK_ROWS - 1)) * (1.0 / CHUNK_ROWS))


def _dispatch_kernel(logit_ref, x_ref, upper_ref, lower_ref, xs_ref, pos_ref, gate_ref, cnt_ref):
    l = jnp.transpose(logit_ref[...])[0:N_EXPERTS, :]
    sub = lax.broadcasted_iota(jnp.int32, (N_EXPERTS, TS), 0)
    onehots, vals = [], []
    for _ in range(TOP_K):
        mx = jnp.max(l, axis=0, keepdims=True)
        idx = jnp.min(jnp.where(l == mx, sub, N_EXPERTS), axis=0, keepdims=True)
        oh = sub == idx
        l = jnp.where(oh, -jnp.inf, l)
        onehots.append(oh)
        vals.append(mx)
    es = [jnp.exp(v - vals[0]) for v in vals]
    den = es[0] + es[1] + es[2] + es[3]
    multi = jnp.zeros((N_EXPERTS, TS), F32)
    for oh in onehots:
        multi = multi + oh.astype(F32)
    multi_b = jnp.concatenate([multi, jnp.zeros((LANES - N_EXPERTS, TS), F32)], axis=0).astype(BF16)
    cnt_row = lax.dot_general(jnp.ones((8, TS), BF16), multi_b, (((1,), (1,)), ((), ())),
                              preferred_element_type=F32)
    cnt_ref[...] = cnt_row.astype(jnp.int32)
    cnt_col = jnp.concatenate([jnp.sum(multi, axis=1, keepdims=True),
                               jnp.zeros((LANES - N_EXPERTS, 1), F32)], axis=0)
    n_chunks = jnp.broadcast_to(_chunks_per_expert(cnt_col), (LANES, LANES)).astype(BF16)
    run_off = CHUNK_ROWS * jnp.dot(lower_ref[...], n_chunks, preferred_element_type=F32)[0:N_EXPERTS, 0:1]
    rank = jnp.dot(multi_b, upper_ref[...], preferred_element_type=F32)[0:N_EXPERTS, :]
    slot = rank + run_off
    pos_rows = [jnp.sum(jnp.where(onehots[k], slot, 0.0), axis=0, keepdims=True) for k in range(TOP_K)]
    pad = jnp.full((LANES - TOP_K, TS), -1.0, F32)
    pos_all = jnp.concatenate(pos_rows + [pad], axis=0)
    pos_ref[...] = jnp.transpose(pos_all).astype(jnp.int32)
    gate_ref[...] = jnp.transpose(jnp.concatenate([e / den for e in es] + [pad], axis=0))
    live = pl.program_id(0) < pl.num_programs(0) - 1
    pos_t = jnp.where(live, pos_all[0:8, :], -1.0)
    row = lax.broadcasted_iota(jnp.int32, (P_ROWS, TS), 0).astype(F32).astype(BF16)
    for c in range(REGION // P_ROWS):
        rel = pos_t - float(c * P_ROWS)
        rel = jnp.where(jnp.logical_and(rel >= 0.0, rel < float(P_ROWS)), rel, -1.0).astype(BF16)
        perm = jnp.zeros((P_ROWS, TS), BF16)
        for k in range(TOP_K):
            perm = jnp.where(row == rel[k:k + 1, :], jnp.ones((), BF16), perm)
        xs_ref[c * P_ROWS:(c + 1) * P_ROWS, :] = jnp.dot(
            perm, x_ref[...], preferred_element_type=F32).astype(BF16)


def _dispatch(logits, x1b):
    t = logits.shape[0]
    n_tiles = t // TS
    upper = jnp.asarray(np.triu(np.ones((TS, TS), np.float32), 1), dtype=BF16)
    lower = jnp.asarray(np.tril(np.ones((LANES, LANES), np.float32), -1), dtype=BF16)

    def tile(i):
        return jnp.minimum(i, n_tiles - 1)

    return pl.pallas_call(
        _dispatch_kernel,
        grid=(n_tiles + 1,),
        in_specs=[pl.BlockSpec((TS, LANES), lambda i: (tile(i), 0)),
                  pl.BlockSpec((TS, D_MODEL), lambda i: (tile(i), 0)),
                  pl.BlockSpec((TS, TS), lambda i: (0, 0)),
                  pl.BlockSpec((LANES, LANES), lambda i: (0, 0))],
        out_specs=[pl.BlockSpec((REGION, D_MODEL), lambda i: (i, 0)),
                   pl.BlockSpec((TS, LANES), lambda i: (tile(i), 0)),
                   pl.BlockSpec((TS, LANES), lambda i: (tile(i), 0)),
                   pl.BlockSpec((None, 8, LANES), lambda i: (tile(i), 0, 0))],
        out_shape=[jax.ShapeDtypeStruct(((n_tiles + 1) * REGION, D_MODEL), BF16),
                   jax.ShapeDtypeStruct((t, LANES), jnp.int32), jax.ShapeDtypeStruct((t, LANES), F32),
                   jax.ShapeDtypeStruct((n_tiles, 8, LANES), jnp.int32)],
        compiler_params=pltpu.CompilerParams(
            dimension_semantics=("arbitrary",), vmem_limit_bytes=VMEM_LIMIT),
        name="dispatch",
    )(logits, x1b, upper, lower)


def _chunk_plan(cnt, bm):
    assert REGION >= 3 * bm
    n_tiles = cnt.shape[0]
    ch = bm // CHUNK_ROWS
    n_rows_max = n_tiles * REGION + N_EXPERTS * (bm - CHUNK_ROWS)
    n_blk = -(-n_rows_max // bm)
    run_chunks = (cnt + CHUNK_ROWS - 1) // CHUNK_ROWS
    run_off = (jnp.cumsum(run_chunks, axis=1) - run_chunks) * CHUNK_ROWS
    rows_e = jnp.sum(run_chunks, axis=0) * CHUNK_ROWS
    padded_e = (rows_e + bm - 1) // bm * bm
    pend = jnp.cumsum(padded_e)
    pstart = pend - padded_e
    runs_t = run_chunks.T
    run_start = pstart[:, None] // CHUNK_ROWS + jnp.cumsum(runs_t, axis=1) - runs_t
    run_row = jnp.arange(n_tiles, dtype=jnp.int32)[None, :] * REGION + run_off.T
    blk = jnp.arange(n_blk + 1, dtype=jnp.int32)
    blk_e = jnp.minimum(jnp.sum((pend[None, :] <= blk[:, None] * bm).astype(jnp.int32), axis=1), N_EXPERTS - 1)
    is_e = blk_e[:, None, None] == jnp.arange(N_EXPERTS, dtype=jnp.int32)[None, :, None]
    c = (blk[:, None] * ch + jnp.arange(ch, dtype=jnp.int32)[None, :]).reshape(-1)

    def of_run(v):
        v_b = jnp.sum(jnp.where(is_e, v[None], 0), axis=1)
        step = v_b - jnp.concatenate([jnp.zeros_like(v_b[:, :1]), v_b[:, :-1]], axis=1)
        return jnp.sum(jnp.where(started, step[:, None, :], 0), axis=-1).reshape(-1)

    start_b = jnp.sum(jnp.where(is_e, run_start[None], 0), axis=1)
    started = start_b[:, None, :] <= c.reshape(n_blk + 1, ch)[:, :, None]
    j = c - of_run(run_start)
    valid = j < of_run(runs_t)
    row = of_run(run_row) + j * CHUNK_ROWS
    spare = n_tiles * REGION + (c % ch) * CHUNK_ROWS
    src = jnp.where(valid, row, spare + 2 * bm).astype(jnp.int32).reshape(n_blk + 1, 1, ch)
    dst = jnp.where(valid, row, spare + ((c // ch) % 2) * bm).astype(jnp.int32).reshape(n_blk + 1, 1, ch)
    dst = jnp.concatenate([dst[n_blk:], dst[:n_blk]])
    blk_e = blk_e[:n_blk].astype(jnp.int32)
    n_used = (pend[-1:] // bm).astype(jnp.int32)
    return src, dst, blk_e, n_used


def _ffn_kernel(blk_e_ref, n_used_ref, src0_ref, srcn_ref, dstp_ref, dstc_ref, xs_hbm, wgu_ref, bgu_ref, wd_ref,
                bd_ref, ys_hbm, xbuf, ybuf, wgu_b, wd_b, in_sem, out_sem, *, bm):
    i = pl.program_id(0)
    n_used = n_used_ref[0]
    slot = i % 2
    ch = bm // CHUNK_ROWS

    def gather(tbl_ref, s):
        for j in range(ch):
            row = pl.multiple_of(tbl_ref[0, j], CHUNK_ROWS)
            pltpu.make_async_copy(xs_hbm.at[pl.ds(row, CHUNK_ROWS)],
                                  xbuf.at[s, pl.ds(j * CHUNK_ROWS, CHUNK_ROWS)], in_sem.at[s]).start()

    def scatter(tbl_ref, s):
        for j in range(ch):
            row = pl.multiple_of(tbl_ref[0, j], CHUNK_ROWS)
            pltpu.make_async_copy(ybuf.at[s, pl.ds(j * CHUNK_ROWS, CHUNK_ROWS)],
                                  ys_hbm.at[pl.ds(row, CHUNK_ROWS)], out_sem.at[s]).start()

    def wait_in(s):
        pltpu.make_async_copy(xs_hbm.at[pl.ds(0, bm)], xbuf.at[s], in_sem.at[s]).wait()

    def wait_out(s):
        pltpu.make_async_copy(ybuf.at[s], ys_hbm.at[pl.ds(0, bm)], out_sem.at[s]).wait()

    used = i < n_used
    e_changed = jnp.logical_or(i == 0, blk_e_ref[i] != blk_e_ref[jnp.maximum(i - 1, 0)])

    @pl.when(jnp.logical_and(i == 0, used))
    def _():
        gather(src0_ref, 0)
        ybuf[1] = jnp.zeros((bm, D_MODEL), BF16)

    @pl.when(jnp.logical_and(used, e_changed))
    def _():
        wgu_b[...] = wgu_ref[...].astype(BF16)
        wd_b[...] = wd_ref[...].astype(BF16)

    @pl.when(jnp.logical_and(used, i >= 1))
    def _():
        wait_out(slot)

    @pl.when(used)
    def _():
        wait_in(slot)
        gu = jnp.dot(xbuf[slot], wgu_b[...], preferred_element_type=F32) + bgu_ref[...]
        gather(srcn_ref, 1 - slot)
        scatter(dstp_ref, 1 - slot)
        gate = jnp.minimum(gu[:, :D_MODEL], SWIGLU_LIMIT)
        up = jnp.clip(gu[:, D_MODEL:], -SWIGLU_LIMIT, SWIGLU_LIMIT)
        h = (up + 1.0) * gate * jax.nn.sigmoid(SWIGLU_ALPHA * gate)
        y = jnp.dot(h.astype(BF16), wd_b[...], preferred_element_type=F32) + bd_ref[...]
        ybuf[slot] = y.astype(BF16)

    @pl.when(i == n_used - 1)
    def _():
        scatter(dstc_ref, slot)
        wait_in(1 - slot)
        wait_out(1 - slot)
        wait_out(slot)


def _ffn(xs, src, dst, blk_e, n_used, p, layer, bm):
    n_blk = blk_e.shape[0]
    ch = src.shape[-1]
    smem = functools.partial(pl.BlockSpec, memory_space=pltpu.SMEM)
    grid_spec = pltpu.PrefetchScalarGridSpec(
        num_scalar_prefetch=2,
        grid=(n_blk,),
        in_specs=[smem((None, 1, ch), lambda i, be, nu: (0, 0, 0)),
                  smem((None, 1, ch), lambda i, be, nu: (i + 1, 0, 0)),
                  smem((None, 1, ch), lambda i, be, nu: (i, 0, 0)),
                  smem((None, 1, ch), lambda i, be, nu: (i + 1, 0, 0)),
                  pl.BlockSpec(memory_space=pl.ANY),
                  pl.BlockSpec((None, None, D_MODEL, 2 * D_MODEL), lambda i, be, nu: (layer, be[i], 0, 0)),
                  pl.BlockSpec((None, None, 1, 2 * D_MODEL), lambda i, be, nu: (layer, be[i], 0, 0)),
                  pl.BlockSpec((None, None, D_MODEL, D_MODEL), lambda i, be, nu: (layer, be[i], 0, 0)),
                  pl.BlockSpec((None, None, 1, D_MODEL), lambda i, be, nu: (layer, be[i], 0, 0))],
        out_specs=pl.BlockSpec(memory_space=pl.ANY),
        scratch_shapes=[pltpu.VMEM((2, bm, D_MODEL), BF16), pltpu.VMEM((2, bm, D_MODEL), BF16),
                        pltpu.VMEM((D_MODEL, 2 * D_MODEL), BF16), pltpu.VMEM((D_MODEL, D_MODEL), BF16),
                        pltpu.SemaphoreType.DMA((2,)), pltpu.SemaphoreType.DMA((2,))],
    )
    return pl.pallas_call(
        functools.partial(_ffn_kernel, bm=bm),
        grid_spec=grid_spec,
        out_shape=jax.ShapeDtypeStruct(xs.shape, BF16),
        input_output_aliases={6: 0},
        compiler_params=pltpu.CompilerParams(
            dimension_semantics=("arbitrary",), vmem_limit_bytes=VMEM_LIMIT),
        name="ffn",
    )(blk_e, n_used, src, src, dst, dst, xs, p["w_gu"], p["b_gu"], p["w_down"], p["b_down"])


def _combine_kernel(ys_ref, pos_ref, gate_ref, x_ref, g_ref, b_ref, x2_ref, x2b_ref, g_scr, *, dn_alpha):
    lane = lax.broadcasted_iota(jnp.int32, (TS, LANES), 1)
    tile_of, gate = [], []
    for k in range(TOP_K):
        rel = jnp.broadcast_to(pos_ref[:, k:k + 1], (TS, LANES)) - lane
        hit = jnp.where((rel & (LANES - 1)) == 0, rel >> (LANES.bit_length() - 1), -1)
        tile_of.append(hit.astype(F32).astype(BF16))
        gate.append(jnp.broadcast_to(gate_ref[:, k:k + 1], (TS, LANES)).astype(BF16))
    for c in range(REGION // LANES):
        g = jnp.zeros((TS, LANES), BF16)
        for k in range(TOP_K):
            g = jnp.where(tile_of[k] == float(c), gate[k], g)
        g_scr[:, c * LANES:(c + 1) * LANES] = g
    y = jnp.dot(g_scr[...], ys_ref[...], preferred_element_type=F32)
    x2 = _layer_norm(dn_alpha * x_ref[...] + y, g_ref[...], b_ref[...])
    x2_ref[...] = x2
    x2b_ref[...] = x2.astype(BF16)


def _combine(ys, pos, gates, x1, p, layer, dn_alpha):
    t = x1.shape[0]
    return pl.pallas_call(
        functools.partial(_combine_kernel, dn_alpha=np.float32(dn_alpha)),
        grid=(t // TS,),
        in_specs=[pl.BlockSpec((REGION, D_MODEL), lambda i: (i, 0)),
                  pl.BlockSpec((TS, LANES), lambda i: (i, 0)),
                  pl.BlockSpec((TS, LANES), lambda i: (i, 0)),
                  pl.BlockSpec((TS, D_MODEL), lambda i: (i, 0)),
                  pl.BlockSpec((None, 1, D_MODEL), lambda i: (layer, 0, 0)),
                  pl.BlockSpec((None, 1, D_MODEL), lambda i: (layer, 0, 0))],
        out_specs=[pl.BlockSpec((TS, D_MODEL), lambda i: (i, 0)),
                   pl.BlockSpec((TS, D_MODEL), lambda i: (i, 0))],
        out_shape=[jax.ShapeDtypeStruct((t, D_MODEL), F32), jax.ShapeDtypeStruct((t, D_MODEL), BF16)],
        scratch_shapes=[pltpu.VMEM((TS, REGION), BF16)],
        compiler_params=pltpu.CompilerParams(
            dimension_semantics=("parallel",), vmem_limit_bytes=VMEM_LIMIT),
        name="combine",
    )(ys, pos, gates, x1, p["ln2_g"], p["ln2_b"])


def _prepare_params(w_in, b_in, ln_v_g, ln_v_b, w_s, b_s, w_pa, w_pb, w_o, ln1_g, ln1_b,
                    w_r, b_r, w_gu, b_gu, w_down, b_down, ln2_g, ln2_b):
    depth = w_in.shape[0]
    n_qkv = 3 * len(B_PATTERNS) * B_GROUP_WIDTH
    qkv0, qkv1 = 2 * D_MODEL, 2 * D_MODEL + n_qkv

    def permute_cols(a):
        gap = jnp.zeros(a.shape[:-1] + (MAIN_COL0 - n_qkv,), a.dtype)
        return jnp.concatenate([a[..., qkv0:qkv1], gap, a[..., :qkv0], a[..., qkv1:]], axis=-1)

    def vec(a):
        return a.reshape(depth, 1, a.shape[-1]).astype(F32)

    w_r_pad = jnp.pad(w_r, ((0, 0), (0, 0), (0, LANES - N_EXPERTS))).astype(BF16)
    b_r_pad = jnp.pad(b_r.astype(F32), ((0, 0), (0, LANES - N_EXPERTS)), constant_values=NEG_INF)
    return dict(
        w_in=permute_cols(w_in).astype(BF16), b_in=vec(permute_cols(b_in)),
        ln_v_g=vec(ln_v_g), ln_v_b=vec(ln_v_b),
        w_s=w_s.astype(BF16),
        b_s=jnp.broadcast_to(b_s.astype(F32)[..., None], b_s.shape + (CHUNK,)),
        w_pa=w_pa.astype(BF16), w_pb=w_pb.astype(BF16), w_o=w_o.astype(BF16),
        ln1_g=vec(ln1_g), ln1_b=vec(ln1_b),
        w_r=w_r_pad, b_r=vec(b_r_pad),
        w_gu=w_gu, b_gu=b_gu.astype(F32)[:, :, None, :],
        w_down=w_down, b_down=b_down.astype(F32)[:, :, None, :],
        ln2_g=vec(ln2_g), ln2_b=vec(ln2_b),
    )


def _trunk(x, params, bm=512):
    n_seq, seq, _ = x.shape
    t = n_seq * seq
    depth = params["w_in"].shape[0]
    dn_alpha = (2.0 * depth) ** 0.25
    x = x.reshape(t, D_MODEL)
    xb = x.astype(BF16)
    for layer in range(depth):
        z = _inproj_main(xb, params["w_in"], params["b_in"], layer)
        attn = [_attention(_inproj_qkv(xb, params["w_in"], params["b_in"], layer, g, n_seq, seq), g)
                for g in range(len(B_PATTERNS))]
        x1, x1b, logits = _mixer(z, attn, x, params, layer, dn_alpha, seq)
        xs, pos, gates, cnt = _dispatch(logits, x1b)
        src, dst, blk_e, n_used = _chunk_plan(cnt[:, 0, :N_EXPERTS], bm)
        ys = _ffn(xs, src, dst, blk_e, n_used, params, layer, bm)
        x, xb = _combine(ys, pos, gates, x1, params, layer, dn_alpha)
    return x.reshape(n_seq, seq, D_MODEL)


def kernel(x_prompt, x_sample, w_in, b_in, ln_v_g, ln_v_b, w_s, b_s, w_pa, w_pb, w_o, ln1_g, ln1_b,
           w_r, b_r, w_gu, b_gu, w_down, b_down, ln2_g, ln2_b):
    params = _prepare_params(w_in, b_in, ln_v_g, ln_v_b, w_s, b_s, w_pa, w_pb, w_o, ln1_g, ln1_b,
                             w_r, b_r, w_gu, b_gu, w_down, b_down, ln2_g, ln2_b)
    n_prompt = x_prompt.shape[0]
    y = _trunk(jnp.concatenate([x_prompt, x_sample], axis=0), params)
    return (y[:n_prompt], y[n_prompt:])
```

```python
import functools

import numpy as np
import jax
import jax.numpy as jnp
from jax import lax
from jax.experimental import pallas as pl
from jax.experimental.pallas import tpu as pltpu

D_MODEL = 1024
CHUNK = 128
A_GROUPS = 8
B_PATTERNS = ((128, 1), (512, 4), (2048, 16))
B_HEADS = 4
B_HEAD_DIM = 128
B_GROUP_WIDTH = B_HEADS * B_HEAD_DIM
N_ATTN_HEADS = len(B_PATTERNS) * B_HEADS
N_SIDE = 64
D_IN = 4 * D_MODEL + 3 * len(B_PATTERNS) * B_GROUP_WIDTH
N_EXPERTS = 32
TOP_K = 4
SWIGLU_LIMIT = 7.0
SWIGLU_ALPHA = 1.702
LN_EPS = 1e-5
NEG_INF = -1e30
LANES = 128

MAIN_COLS = 4 * D_MODEL
QKV_COLS = 3 * B_GROUP_WIDTH
MAIN_COL0 = -(-len(B_PATTERNS) * QKV_COLS // D_MODEL) * D_MODEL
GELU_COLS = 2 * D_MODEL

VMEM_LIMIT = 56 * 1024 * 1024

BF16 = jnp.bfloat16
F32 = jnp.float32


def _gelu(x):
    return 0.5 * x * (1.0 + lax.erf(x * np.float32(np.sqrt(0.5))))


def _sigmoid(x):
    return 0.5 * jnp.tanh(0.5 * x) + 0.5


def _layer_norm(x, g, b):
    xc = x - jnp.mean(x, axis=-1, keepdims=True)
    var = jnp.mean(xc * xc, axis=-1, keepdims=True)
    return xc * lax.rsqrt(var + LN_EPS) * g + b


N_CHUNK = 256
MAX_ROW_STRIDE = 4


def _inproj_main_kernel(x_ref, w_ref, b_ref, z_ref, *, n_gelu_tiles):
    j = pl.program_id(1)

    def run(act):
        for c in range(z_ref.shape[1] // N_CHUNK):
            cols = slice(c * N_CHUNK, (c + 1) * N_CHUNK)
            acc = jnp.dot(x_ref[...], w_ref[:, cols], preferred_element_type=F32) + b_ref[:, cols]
            z_ref[:, cols] = act(acc).astype(BF16)

    @pl.when(j < n_gelu_tiles)
    def _():
        run(_gelu)

    @pl.when(j >= n_gelu_tiles)
    def _():
        run(lambda a: a)


def _inproj_main(xb, w_in, b_in, layer, tm=1024, tn=1024):
    t = xb.shape[0]
    return pl.pallas_call(
        functools.partial(_inproj_main_kernel, n_gelu_tiles=GELU_COLS // tn),
        grid=(t // tm, MAIN_COLS // tn),
        in_specs=[
            pl.BlockSpec((tm, D_MODEL), lambda i, j: (i, 0)),
            pl.BlockSpec((None, D_MODEL, tn), lambda i, j: (layer, 0, MAIN_COL0 // tn + j)),
            pl.BlockSpec((None, 1, tn), lambda i, j: (layer, 0, MAIN_COL0 // tn + j)),
        ],
        out_specs=pl.BlockSpec((tm, tn), lambda i, j: (i, j)),
        out_shape=jax.ShapeDtypeStruct((t, MAIN_COLS), BF16),
        compiler_params=pltpu.CompilerParams(
            dimension_semantics=("parallel", "arbitrary"), vmem_limit_bytes=VMEM_LIMIT),
        name="inproj_main",
    )(xb, w_in, b_in)


def _inproj_qkv_kernel(x_ref, w_ref, b_ref, o_ref, *scratch, dilation):
    rows = x_ref.shape[0] // dilation
    for c in range(o_ref.shape[-1] // N_CHUNK):
        cols = slice(c * N_CHUNK, (c + 1) * N_CHUNK)
        acc = jnp.dot(x_ref[...], w_ref[:, cols], preferred_element_type=F32) + b_ref[:, cols]
        if dilation == 1:
            o_ref[0, :, cols] = acc.astype(BF16)
            continue
        scr, scr2 = scratch[0], scratch[-1]
        d1 = min(dilation, MAX_ROW_STRIDE)
        d2 = dilation // d1
        for q in range(c * N_CHUNK // LANES, (c + 1) * N_CHUNK // LANES):
            scr[q] = acc[:, q * LANES - c * N_CHUNK:(q + 1) * LANES - c * N_CHUNK]
            for r1 in range(d1):
                part = scr[q, pl.ds(r1, rows * d2, stride=d1), :]
                if d2 == 1:
                    o_ref[r1, :, q * LANES:(q + 1) * LANES] = part.astype(BF16)
                    continue
                scr2[q, r1 * rows * d2:(r1 + 1) * rows * d2, :] = part
                for r2 in range(d2):
                    o_ref[r2 * d1 + r1, :, q * LANES:(q + 1) * LANES] = (
                        scr2[q, pl.ds(r1 * rows * d2 + r2, rows, stride=d2), :].astype(BF16))


def _inproj_qkv(xb, w_in, b_in, layer, group, n_seq, seq, tm=1024, tn=QKV_COLS):
    _, dilation = B_PATTERNS[group]
    tiles_per_seq = seq // tm
    col0 = group * QKV_COLS // tn
    return pl.pallas_call(
        functools.partial(_inproj_qkv_kernel, dilation=dilation),
        grid=(n_seq * tiles_per_seq, QKV_COLS // tn),
        in_specs=[
            pl.BlockSpec((tm, D_MODEL), lambda i, j: (i, 0)),
            pl.BlockSpec((None, D_MODEL, tn), lambda i, j: (layer, 0, col0 + j)),
            pl.BlockSpec((None, 1, tn), lambda i, j: (layer, 0, col0 + j)),
        ],
        out_specs=pl.BlockSpec((None, dilation, tm // dilation, tn),
                               lambda i, j: (i // tiles_per_seq, 0, i % tiles_per_seq, j)),
        out_shape=jax.ShapeDtypeStruct((n_seq, dilation, seq // dilation, QKV_COLS), BF16),
        scratch_shapes=[pltpu.VMEM((tn // LANES, tm, LANES), F32)] * (
            0 if dilation == 1 else 1 if dilation <= MAX_ROW_STRIDE else 2),
        compiler_params=pltpu.CompilerParams(
            dimension_semantics=("parallel", "arbitrary"), vmem_limit_bytes=VMEM_LIMIT),
        name=f"inproj_qkv{group}",
    )(xb, w_in, b_in)


def _attn_kernel(cur_ref, prev_ref, next_ref, bias_ref, o_ref, lse_ref, *scratch, bq):
    for rb in range(cur_ref.shape[0]):
        _attn_rows(cur_ref.at[rb], prev_ref.at[rb], next_ref.at[rb], bias_ref, o_ref.at[rb], lse_ref.at[rb],
                   scratch, bq)


def _attn_rows(cur_ref, prev_ref, next_ref, bias_ref, o_ref, lse_ref, scratch, bq):
    s_scr, p_scr = scratch[:B_HEADS], scratch[B_HEADS:]
    jq = pl.program_id(2)
    nq = pl.num_programs(2)
    n_piece = bq // N_SIDE
    col = lax.broadcasted_iota(jnp.int32, (N_SIDE, 3 * N_SIDE), 1)
    lane = lax.broadcasted_iota(jnp.int32, (N_SIDE, LANES), 1)
    scale = np.float32(B_HEAD_DIM ** -0.5)

    def window(p, part, h):
        c0 = part * B_GROUP_WIDTH + h * B_HEAD_DIM
        cols = slice(c0, c0 + B_HEAD_DIM)
        if n_piece == 1:
            return jnp.concatenate([prev_ref[:, cols], cur_ref[:, cols], next_ref[:, cols]], axis=0)
        if p == 0:
            return jnp.concatenate([prev_ref[:, cols], cur_ref[0:2 * N_SIDE, cols]], axis=0)
        if p == n_piece - 1:
            return jnp.concatenate([cur_ref[bq - 2 * N_SIDE:bq, cols], next_ref[:, cols]], axis=0)
        return cur_ref[(p - 1) * N_SIDE:(p + 2) * N_SIDE, cols]

    def scores(p, h):
        qh = cur_ref[p * N_SIDE:(p + 1) * N_SIDE, h * B_HEAD_DIM:(h + 1) * B_HEAD_DIM]
        s = lax.dot_general(qh, window(p, 1, h), (((1,), (1,)), ((), ())), preferred_element_type=F32) * scale
        if p == 0:
            s = jnp.where(jnp.logical_and(col < N_SIDE, jq == 0), NEG_INF, s)
        if p == n_piece - 1:
            s = jnp.where(jnp.logical_and(col >= 2 * N_SIDE, jq == nq - 1), NEG_INF, s)
        return s + bias_ref[h]

    def softmax(s):
        m = jnp.max(s, axis=-1, keepdims=True)
        e = jnp.exp(s - m)
        den = jnp.sum(e, axis=-1, keepdims=True)
        return (e / den).astype(BF16), m + jnp.log(den)

    def values(p, h, pn):
        o = jnp.dot(pn, window(p, 2, h), preferred_element_type=F32)
        o_ref[p * N_SIDE:(p + 1) * N_SIDE, h * B_HEAD_DIM:(h + 1) * B_HEAD_DIM] = o.astype(BF16)

    lse = {}
    for t in range(B_HEADS + 2):
        if t < B_HEADS:
            for p in range(n_piece):
                s_scr[t][p * N_SIDE:(p + 1) * N_SIDE, :] = scores(p, t)
        if 1 <= t <= B_HEADS:
            for p in range(n_piece):
                rows = slice(p * N_SIDE, (p + 1) * N_SIDE)
                p_scr[t - 1][rows, :], lse[p, t - 1] = softmax(s_scr[t - 1][rows, :])
        if t >= 2:
            for p in range(n_piece):
                values(p, t - 2, p_scr[t - 2][p * N_SIDE:(p + 1) * N_SIDE, :])
    for p in range(n_piece):
        tile = jnp.zeros((N_SIDE, LANES), F32)
        for h in range(B_HEADS):
            tile = jnp.where(lane == h, lse[p, h], tile)
        lse_ref[p * N_SIDE:(p + 1) * N_SIDE, :] = tile


def _attn_bias(group):
    _, dilation = B_PATTERNS[group]
    slopes = np.array([2.0 ** (-8.0 * (group * B_HEADS + h + 1) / N_ATTN_HEADS) for h in range(B_HEADS)],
                      dtype=np.float32)
    a = np.arange(N_SIDE)[:, None]
    c = np.arange(3 * N_SIDE)[None, :]
    rel = c - N_SIDE - a
    alibi = (-(slopes[:, None, None] * (dilation * np.abs(rel))[None])).astype(np.float32)
    return np.where((np.abs(rel) <= N_SIDE)[None], alibi, np.float32(NEG_INF)).astype(np.float32)


def _attention(qkv, group, rows_per_step=1024):
    n_seq, dilation, sub_len, _ = qkv.shape
    bq = min(rows_per_step, sub_len)
    nq = sub_len // bq
    rb = min(dilation, rows_per_step // bq)
    halo_per_bq = bq // N_SIDE
    n_halo = sub_len // N_SIDE
    return pl.pallas_call(
        functools.partial(_attn_kernel, bq=bq),
        grid=(n_seq, dilation // rb, nq),
        in_specs=[pl.BlockSpec((None, rb, bq, QKV_COLS), lambda b, r, j: (b, r, j, 0)),
                  pl.BlockSpec((None, rb, N_SIDE, QKV_COLS),
                               lambda b, r, j: (b, r, jnp.maximum(j * halo_per_bq - 1, 0), 0)),
                  pl.BlockSpec((None, rb, N_SIDE, QKV_COLS),
                               lambda b, r, j: (b, r, jnp.minimum((j + 1) * halo_per_bq, n_halo - 1), 0)),
                  pl.BlockSpec((B_HEADS, N_SIDE, 3 * N_SIDE), lambda b, r, j: (0, 0, 0))],
        out_specs=[pl.BlockSpec((None, rb, bq, B_GROUP_WIDTH), lambda b, r, j: (b, r, j, 0)),
                   pl.BlockSpec((None, rb, bq, LANES), lambda b, r, j: (b, r, j, 0))],
        out_shape=[jax.ShapeDtypeStruct((n_seq, dilation, sub_len, B_GROUP_WIDTH), BF16),
                   jax.ShapeDtypeStruct((n_seq, dilation, sub_len, LANES), F32)],
        scratch_shapes=([pltpu.VMEM((bq, 3 * N_SIDE), F32)] * B_HEADS
                        + [pltpu.VMEM((bq, 3 * N_SIDE), BF16)] * B_HEADS),
        compiler_params=pltpu.CompilerParams(
            dimension_semantics=("parallel", "parallel", "arbitrary"), vmem_limit_bytes=VMEM_LIMIT),
        name=f"attn{group}",
    )(qkv, qkv, qkv, jnp.asarray(_attn_bias(group)))


def _mixer_kernel(u_ref, v_ref, ga_ref, gb_ref, o0_ref, o1_ref, o2_ref, l0_ref, l1_ref, l2_ref, x_ref,
                  lnvg_ref, lnvb_ref, ws_ref, bs_ref, wpa_ref, wpb_ref, wo_ref, ln1g_ref, ln1b_ref,
                  wr_ref, br_ref, x1_ref, x1b_ref, logit_ref, a_scr, b_scr, o_scr, l_scr, t_scr, *, tm, dn_alpha):
    for g, (o_ref, l_ref) in enumerate(((o0_ref, l0_ref), (o1_ref, l1_ref), (o2_ref, l2_ref))):
        dilation = o_ref.shape[0]
        rows = tm // dilation
        d1 = min(dilation, MAX_ROW_STRIDE)
        d2 = dilation // d1

        def pieces(r):
            return [o_ref[r, :, h * B_HEAD_DIM:(h + 1) * B_HEAD_DIM].astype(F32) for h in range(B_HEADS)] + [l_ref[r]]

        def place(idx, vals):
            for h in range(B_HEADS):
                o_scr[g, h, idx, :] = vals[h]
            l_scr[g, idx, :] = vals[B_HEADS]

        for r1 in range(d1):
            idx = pl.ds(r1, rows * d2, stride=d1) if d1 > 1 else slice(None)
            if d2 == 1:
                place(idx, pieces(r1))
                continue
            for r2 in range(d2):
                for j, val in enumerate(pieces(r2 * d1 + r1)):
                    t_scr[r1, j, pl.ds(r2, rows, stride=d2), :] = val
            place(idx, [t_scr[r1, j] for j in range(B_HEADS + 1)])
    n_groups = len(B_PATTERNS)

    def branches(r0, nr):
        rows = slice(r0, r0 + nr)
        vn = _layer_norm(v_ref[rows, :].astype(F32), lnvg_ref[...], lnvb_ref[...]).astype(BF16)
        for g in range(A_GROUPS):
            c0 = g * CHUNK
            chunks = jnp.concatenate([vn[q0:q0 + CHUNK, c0:c0 + CHUNK] for q0 in range(0, nr, CHUNK)], axis=1)
            mixed_all = jnp.dot(ws_ref[g], chunks, preferred_element_type=F32)
            for c in range(nr // CHUNK):
                q0 = c * CHUNK
                mixed = mixed_all[:, q0:q0 + CHUNK] + bs_ref[g]
                a = u_ref[r0 + q0:r0 + q0 + CHUNK, c0:c0 + CHUNK].astype(F32) * mixed
                a_scr[r0 + q0:r0 + q0 + CHUNK, c0:c0 + CHUNK] = a.astype(BF16)
        ls = [l_scr[g, rows, :] for g in range(n_groups)]
        mx = jnp.maximum(jnp.maximum(ls[0], ls[1]), ls[2])
        es = [jnp.exp(l - mx) for l in ls]
        den = es[0] + es[1] + es[2]
        wts = [e / den for e in es]
        for h in range(B_HEADS):
            c0 = h * B_HEAD_DIM
            acc = jnp.zeros((nr, B_HEAD_DIM), F32)
            for g in range(n_groups):
                acc = acc + wts[g][:, h:h + 1] * o_scr[g, h, rows, :]
            b_scr[rows, c0:c0 + B_HEAD_DIM] = acc.astype(BF16)

    def project(r0, nr):
        rows = slice(r0, r0 + nr)
        pa = jnp.dot(a_scr[rows, :], wpa_ref[...], preferred_element_type=F32)
        pb = jnp.dot(b_scr[rows, :], wpb_ref[...], preferred_element_type=F32)
        merged = _sigmoid(ga_ref[rows, :].astype(F32)) * pa + _sigmoid(gb_ref[rows, :].astype(F32)) * pb
        mix = jnp.dot(merged.astype(BF16), wo_ref[...], preferred_element_type=F32)
        x1 = _layer_norm(dn_alpha * x_ref[rows, :] + mix, ln1g_ref[...], ln1b_ref[...])
        x1_ref[rows, :] = x1
        x1b = x1.astype(BF16)
        x1b_ref[rows, :] = x1b
        logit_ref[rows, :] = jnp.dot(x1b, wr_ref[...], preferred_element_type=F32) + br_ref[...]

    half = tm // 2
    branches(0, half)
    project(0, half)
    branches(half, half)
    project(half, half)


def _mixer(z, attn, x, p, layer, dn_alpha, seq, tm=512):
    t = x.shape[0]
    tiles_per_seq = seq // tm

    def zspec(cb):
        return pl.BlockSpec((tm, D_MODEL), lambda i: (i, cb))

    def row(width):
        return pl.BlockSpec((tm, width), lambda i: (i, 0))

    def split(a):
        _, dilation, _, width = a.shape
        return pl.BlockSpec((None, dilation, tm // dilation, width),
                            lambda i: (i // tiles_per_seq, 0, i % tiles_per_seq, 0))

    def vec(width):
        return pl.BlockSpec((None, 1, width), lambda i: (layer, 0, 0))

    def mat(k, n):
        return pl.BlockSpec((None, k, n), lambda i: (layer, 0, 0))

    (o0, l0), (o1, l1), (o2, l2) = attn
    return pl.pallas_call(
        functools.partial(_mixer_kernel, tm=tm, dn_alpha=np.float32(dn_alpha)),
        grid=(t // tm,),
        in_specs=[zspec(0), zspec(1), zspec(2), zspec(3),
                  split(o0), split(o1), split(o2), split(l0), split(l1), split(l2), row(D_MODEL),
                  vec(D_MODEL), vec(D_MODEL),
                  pl.BlockSpec((None, A_GROUPS, CHUNK, CHUNK), lambda i: (layer, 0, 0, 0)),
                  pl.BlockSpec((None, A_GROUPS, CHUNK, CHUNK), lambda i: (layer, 0, 0, 0)),
                  mat(D_MODEL, D_MODEL), mat(B_GROUP_WIDTH, D_MODEL), mat(D_MODEL, D_MODEL),
                  vec(D_MODEL), vec(D_MODEL), mat(D_MODEL, LANES), vec(LANES)],
        out_specs=[row(D_MODEL), row(D_MODEL), row(LANES)],
        out_shape=[jax.ShapeDtypeStruct((t, D_MODEL), F32), jax.ShapeDtypeStruct((t, D_MODEL), BF16),
                   jax.ShapeDtypeStruct((t, LANES), F32)],
        scratch_shapes=[pltpu.VMEM((tm, D_MODEL), BF16), pltpu.VMEM((tm, B_GROUP_WIDTH), BF16),
                        pltpu.VMEM((len(B_PATTERNS), B_HEADS, tm, B_HEAD_DIM), F32),
                        pltpu.VMEM((len(B_PATTERNS), tm, LANES), F32),
                        pltpu.VMEM((MAX_ROW_STRIDE, B_HEADS + 1, tm // MAX_ROW_STRIDE, LANES), F32)],
        compiler_params=pltpu.CompilerParams(
            dimension_semantics=("parallel",), vmem_limit_bytes=VMEM_LIMIT),
        name="mixer",
    )(z, z, z, z, o0, o1, o2, l0, l1, l2, x,
      p["ln_v_g"], p["ln_v_b"], p["w_s"], p["b_s"], p["w_pa"], p["w_pb"], p["w_o"],
      p["ln1_g"], p["ln1_b"], p["w_r"], p["b_r"])


TS = 512
CHUNK_ROWS = 16
REGION = TS * TOP_K + N_EXPERTS * CHUNK_ROWS
P_ROWS = 256


def _chunks_per_expert(cnt):
    return jnp.floor((cnt + (CHUNK_ROWS - 1)) * (1.0 / CHUNK_ROWS))


def _dispatch_kernel(logit_ref, x_ref, upper_ref, lower_ref, xs_ref, pos_ref, gate_ref, cnt_ref):
    l = jnp.transpose(logit_ref[...])[0:N_EXPERTS, :]
    sub = lax.broadcasted_iota(jnp.int32, (N_EXPERTS, TS), 0)
    onehots, vals = [], []
    for _ in range(TOP_K):
        mx = jnp.max(l, axis=0, keepdims=True)
        idx = jnp.min(jnp.where(l == mx, sub, N_EXPERTS), axis=0, keepdims=True)
        oh = sub == idx
        l = jnp.where(oh, -jnp.inf, l)
        onehots.append(oh)
        vals.append(mx)
    es = [jnp.exp(v - vals[0]) for v in vals]
    den = es[0] + es[1] + es[2] + es[3]
    multi = jnp.zeros((N_EXPERTS, TS), F32)
    for oh in onehots:
        multi = multi + oh.astype(F32)
    multi_b = jnp.concatenate([multi, jnp.zeros((LANES - N_EXPERTS, TS), F32)], axis=0).astype(BF16)
    cnt_row = lax.dot_general(jnp.ones((8, TS), BF16), multi_b, (((1,), (1,)), ((), ())),
                              preferred_element_type=F32)
    cnt_ref[...] = cnt_row.astype(jnp.int32)
    cnt_col = jnp.concatenate([jnp.sum(multi, axis=1, keepdims=True),
                               jnp.zeros((LANES - N_EXPERTS, 1), F32)], axis=0)
    n_chunks = jnp.broadcast_to(_chunks_per_expert(cnt_col), (LANES, LANES)).astype(BF16)
    run_off = CHUNK_ROWS * jnp.dot(lower_ref[...], n_chunks, preferred_element_type=F32)[0:N_EXPERTS, 0:1]
    rank = jnp.dot(multi_b, upper_ref[...], preferred_element_type=F32)[0:N_EXPERTS, :]
    slot = rank + run_off
    pos_rows = [jnp.sum(jnp.where(onehots[k], slot, 0.0), axis=0, keepdims=True) for k in range(TOP_K)]
    pad = jnp.full((LANES - TOP_K, TS), -1.0, F32)
    pos_all = jnp.concatenate(pos_rows + [pad], axis=0)
    pos_ref[...] = jnp.transpose(pos_all).astype(jnp.int32)
    gate_ref[...] = jnp.transpose(jnp.concatenate([e / den for e in es] + [pad], axis=0))
    live = pl.program_id(0) < pl.num_programs(0) - 1
    pos_t = jnp.where(live, pos_all[0:8, :], -1.0)
    row = lax.broadcasted_iota(jnp.int32, (P_ROWS, TS), 0).astype(F32).astype(BF16)
    for c in range(REGION // P_ROWS):
        rel = pos_t - float(c * P_ROWS)
        rel = jnp.where(jnp.logical_and(rel >= 0.0, rel < float(P_ROWS)), rel, -1.0).astype(BF16)
        perm = jnp.zeros((P_ROWS, TS), BF16)
        for k in range(TOP_K):
            perm = jnp.where(row == rel[k:k + 1, :], jnp.ones((), BF16), perm)
        xs_ref[c * P_ROWS:(c + 1) * P_ROWS, :] = jnp.dot(
            perm, x_ref[...], preferred_element_type=F32).astype(BF16)


def _dispatch(logits, x1b):
    t = logits.shape[0]
    n_tiles = t // TS
    upper = jnp.asarray(np.triu(np.ones((TS, TS), np.float32), 1), dtype=BF16)
    lower = jnp.asarray(np.tril(np.ones((LANES, LANES), np.float32), -1), dtype=BF16)

    def tile(i):
        return jnp.minimum(i, n_tiles - 1)

    return pl.pallas_call(
        _dispatch_kernel,
        grid=(n_tiles + 1,),
        in_specs=[pl.BlockSpec((TS, LANES), lambda i: (tile(i), 0)),
                  pl.BlockSpec((TS, D_MODEL), lambda i: (tile(i), 0)),
                  pl.BlockSpec((TS, TS), lambda i: (0, 0)),
                  pl.BlockSpec((LANES, LANES), lambda i: (0, 0))],
        out_specs=[pl.BlockSpec((REGION, D_MODEL), lambda i: (i, 0)),
                   pl.BlockSpec((TS, LANES), lambda i: (tile(i), 0)),
                   pl.BlockSpec((TS, LANES), lambda i: (tile(i), 0)),
                   pl.BlockSpec((None, 8, LANES), lambda i: (tile(i), 0, 0))],
        out_shape=[jax.ShapeDtypeStruct(((n_tiles + 1) * REGION, D_MODEL), BF16),
                   jax.ShapeDtypeStruct((t, LANES), jnp.int32), jax.ShapeDtypeStruct((t, LANES), F32),
                   jax.ShapeDtypeStruct((n_tiles, 8, LANES), jnp.int32)],
        compiler_params=pltpu.CompilerParams(
            dimension_semantics=("arbitrary",), vmem_limit_bytes=VMEM_LIMIT),
        name="dispatch",
    )(logits, x1b, upper, lower)


def _chunk_plan(cnt, bm):
    assert REGION >= 3 * bm
    n_tiles = cnt.shape[0]
    ch = bm // CHUNK_ROWS
    n_rows_max = n_tiles * REGION + N_EXPERTS * (bm - CHUNK_ROWS)
    n_blk = -(-n_rows_max // bm)
    run_chunks = (cnt + CHUNK_ROWS - 1) // CHUNK_ROWS
    run_off = (jnp.cumsum(run_chunks, axis=1) - run_chunks) * CHUNK_ROWS
    rows_e = jnp.sum(run_chunks, axis=0) * CHUNK_ROWS
    padded_e = (rows_e + bm - 1) // bm * bm
    pend = jnp.cumsum(padded_e)
    pstart = pend - padded_e
    runs_t = run_chunks.T
    run_start = pstart[:, None] // CHUNK_ROWS + jnp.cumsum(runs_t, axis=1) - runs_t
    run_row = jnp.arange(n_tiles, dtype=jnp.int32)[None, :] * REGION + run_off.T
    blk = jnp.arange(n_blk + 1, dtype=jnp.int32)
    blk_e = jnp.minimum(jnp.sum((pend[None, :] <= blk[:, None] * bm).astype(jnp.int32), axis=1), N_EXPERTS - 1)
    is_e = blk_e[:, None, None] == jnp.arange(N_EXPERTS, dtype=jnp.int32)[None, :, None]
    c = (blk[:, None] * ch + jnp.arange(ch, dtype=jnp.int32)[None, :]).reshape(-1)

    def of_run(v):
        v_b = jnp.sum(jnp.where(is_e, v[None], 0), axis=1)
        step = v_b - jnp.concatenate([jnp.zeros_like(v_b[:, :1]), v_b[:, :-1]], axis=1)
        return jnp.sum(jnp.where(started, step[:, None, :], 0), axis=-1).reshape(-1)

    start_b = jnp.sum(jnp.where(is_e, run_start[None], 0), axis=1)
    started = start_b[:, None, :] <= c.reshape(n_blk + 1, ch)[:, :, None]
    j = c - of_run(run_start)
    valid = j < of_run(runs_t)
    row = of_run(run_row) + j * CHUNK_ROWS
    spare = n_tiles * REGION + (c % ch) * CHUNK_ROWS
    src = jnp.where(valid, row, spare + 2 * bm).astype(jnp.int32).reshape(n_blk + 1, 1, ch)
    dst = jnp.where(valid, row, spare + ((c // ch) % 2) * bm).astype(jnp.int32).reshape(n_blk + 1, 1, ch)
    dst = jnp.concatenate([dst[n_blk:], dst[:n_blk]])
    blk_e = blk_e[:n_blk].astype(jnp.int32)
    n_used = (pend[-1:] // bm).astype(jnp.int32)
    plan = jnp.concatenate([src[1:], dst[:n_blk], dst[1:]], axis=-1)
    return src[:1], plan, blk_e, n_used


def _ffn_kernel(blk_e_ref, n_used_ref, src0_ref, plan_ref, xs_hbm, wgu_ref, bgu_ref, wd_ref,
                bd_ref, ys_hbm, xbuf, ybuf, wgu_b, wd_b, in_sem, out_sem, *, bm):
    i = pl.program_id(0)
    n_used = n_used_ref[0]
    slot = i % 2
    ch = bm // CHUNK_ROWS

    def gather(tbl_ref, col0, s):
        for j in range(ch):
            row = pl.multiple_of(tbl_ref[0, col0 + j], CHUNK_ROWS)
            pltpu.make_async_copy(xs_hbm.at[pl.ds(row, CHUNK_ROWS)],
                                  xbuf.at[s, pl.ds(j * CHUNK_ROWS, CHUNK_ROWS)], in_sem.at[s]).start()

    def scatter(tbl_ref, col0, s):
        for j in range(ch):
            row = pl.multiple_of(tbl_ref[0, col0 + j], CHUNK_ROWS)
            pltpu.make_async_copy(ybuf.at[s, pl.ds(j * CHUNK_ROWS, CHUNK_ROWS)],
                                  ys_hbm.at[pl.ds(row, CHUNK_ROWS)], out_sem.at[s]).start()

    def wait_in(s):
        pltpu.make_async_copy(xs_hbm.at[pl.ds(0, bm)], xbuf.at[s], in_sem.at[s]).wait()

    def wait_out(s):
        pltpu.make_async_copy(ybuf.at[s], ys_hbm.at[pl.ds(0, bm)], out_sem.at[s]).wait()

    used = i < n_used
    e_changed = jnp.logical_or(i == 0, blk_e_ref[i] != blk_e_ref[jnp.maximum(i - 1, 0)])

    @pl.when(jnp.logical_and(i == 0, used))
    def _():
        gather(src0_ref, 0, 0)
        ybuf[1] = jnp.zeros((bm, D_MODEL), BF16)

    @pl.when(jnp.logical_and(used, e_changed))
    def _():
        wgu_b[...] = wgu_ref[...].astype(BF16)
        wd_b[...] = wd_ref[...].astype(BF16)

    @pl.when(jnp.logical_and(used, i >= 1))
    def _():
        wait_out(slot)

    @pl.when(used)
    def _():
        wait_in(slot)
        gu = jnp.dot(xbuf[slot], wgu_b[...], preferred_element_type=F32) + bgu_ref[...]
        gather(plan_ref, 0, 1 - slot)
        scatter(plan_ref, ch, 1 - slot)
        gate = jnp.minimum(gu[:, :D_MODEL], SWIGLU_LIMIT)
        up = jnp.clip(gu[:, D_MODEL:], -SWIGLU_LIMIT, SWIGLU_LIMIT)
        h = (up + 1.0) * gate * _sigmoid(SWIGLU_ALPHA * gate)
        y = jnp.dot(h.astype(BF16), wd_b[...], preferred_element_type=F32) + bd_ref[...]
        ybuf[slot] = y.astype(BF16)

    @pl.when(i == n_used - 1)
    def _():
        scatter(plan_ref, 2 * ch, slot)
        wait_in(1 - slot)
        wait_out(1 - slot)
        wait_out(slot)


def _ffn(xs, src0, plan, blk_e, n_used, p, layer, bm):
    n_blk = blk_e.shape[0]
    ch = src0.shape[-1]
    smem = functools.partial(pl.BlockSpec, memory_space=pltpu.SMEM)
    grid_spec = pltpu.PrefetchScalarGridSpec(
        num_scalar_prefetch=2,
        grid=(n_blk,),
        in_specs=[smem((None, 1, ch), lambda i, be, nu: (0, 0, 0)),
                  smem((None, 1, 3 * ch), lambda i, be, nu: (i, 0, 0)),
                  pl.BlockSpec(memory_space=pl.ANY),
                  pl.BlockSpec((None, None, D_MODEL, 2 * D_MODEL), lambda i, be, nu: (layer, be[i], 0, 0)),
                  pl.BlockSpec((None, None, 1, 2 * D_MODEL), lambda i, be, nu: (layer, be[i], 0, 0)),
                  pl.BlockSpec((None, None, D_MODEL, D_MODEL), lambda i, be, nu: (layer, be[i], 0, 0)),
                  pl.BlockSpec((None, None, 1, D_MODEL), lambda i, be, nu: (layer, be[i], 0, 0))],
        out_specs=pl.BlockSpec(memory_space=pl.ANY),
        scratch_shapes=[pltpu.VMEM((2, bm, D_MODEL), BF16), pltpu.VMEM((2, bm, D_MODEL), BF16),
                        pltpu.VMEM((D_MODEL, 2 * D_MODEL), BF16), pltpu.VMEM((D_MODEL, D_MODEL), BF16),
                        pltpu.SemaphoreType.DMA((2,)), pltpu.SemaphoreType.DMA((2,))],
    )
    return pl.pallas_call(
        functools.partial(_ffn_kernel, bm=bm),
        grid_spec=grid_spec,
        out_shape=jax.ShapeDtypeStruct(xs.shape, BF16),
        input_output_aliases={4: 0},
        compiler_params=pltpu.CompilerParams(
            dimension_semantics=("arbitrary",), vmem_limit_bytes=VMEM_LIMIT),
        name="ffn",
    )(blk_e, n_used, src0, plan, xs, p["w_gu"], p["b_gu"], p["w_down"], p["b_down"])


G_ROWS = 128


def _combine_kernel(ys_ref, pos_ref, gate_ref, x_ref, g_ref, b_ref, x2_ref, x2b_ref, g_scr, *, dn_alpha):
    nr = G_ROWS
    lane = lax.broadcasted_iota(jnp.int32, (nr, LANES), 1)
    for r0 in range(0, TS, nr):
        rows = slice(r0, r0 + nr)
        tile_of, gate = [], []
        for k in range(TOP_K):
            rel = jnp.broadcast_to(pos_ref[rows, k:k + 1], (nr, LANES)) - lane
            hit = jnp.where((rel & (LANES - 1)) == 0, rel >> (LANES.bit_length() - 1), -1)
            tile_of.append(hit.astype(F32).astype(BF16))
            gate.append(jnp.broadcast_to(gate_ref[rows, k:k + 1], (nr, LANES)).astype(BF16))
        for c in range(REGION // LANES):
            g = jnp.zeros((nr, LANES), BF16)
            for k in range(TOP_K):
                g = jnp.where(tile_of[k] == float(c), gate[k], g)
            g_scr[rows, c * LANES:(c + 1) * LANES] = g
        y = jnp.dot(g_scr[rows, :], ys_ref[...], preferred_element_type=F32)
        x2 = _layer_norm(dn_alpha * x_ref[rows, :] + y, g_ref[...], b_ref[...])
        x2_ref[rows, :] = x2
        x2b_ref[rows, :] = x2.astype(BF16)


def _combine(ys, pos, gates, x1, p, layer, dn_alpha):
    t = x1.shape[0]
    return pl.pallas_call(
        functools.partial(_combine_kernel, dn_alpha=np.float32(dn_alpha)),
        grid=(t // TS,),
        in_specs=[pl.BlockSpec((REGION, D_MODEL), lambda i: (i, 0)),
                  pl.BlockSpec((TS, LANES), lambda i: (i, 0)),
                  pl.BlockSpec((TS, LANES), lambda i: (i, 0)),
                  pl.BlockSpec((TS, D_MODEL), lambda i: (i, 0)),
                  pl.BlockSpec((None, 1, D_MODEL), lambda i: (layer, 0, 0)),
                  pl.BlockSpec((None, 1, D_MODEL), lambda i: (layer, 0, 0))],
        out_specs=[pl.BlockSpec((TS, D_MODEL), lambda i: (i, 0)),
                   pl.BlockSpec((TS, D_MODEL), lambda i: (i, 0))],
        out_shape=[jax.ShapeDtypeStruct((t, D_MODEL), F32), jax.ShapeDtypeStruct((t, D_MODEL), BF16)],
        scratch_shapes=[pltpu.VMEM((TS, REGION), BF16)],
        compiler_params=pltpu.CompilerParams(
            dimension_semantics=("parallel",), vmem_limit_bytes=VMEM_LIMIT),
        name="combine",
    )(ys, pos, gates, x1, p["ln2_g"], p["ln2_b"])


def _prepare_params(w_in, b_in, ln_v_g, ln_v_b, w_s, b_s, w_pa, w_pb, w_o, ln1_g, ln1_b,
                    w_r, b_r, w_gu, b_gu, w_down, b_down, ln2_g, ln2_b):
    depth = w_in.shape[0]
    n_qkv = 3 * len(B_PATTERNS) * B_GROUP_WIDTH
    qkv0, qkv1 = 2 * D_MODEL, 2 * D_MODEL + n_qkv

    def permute_cols(a):
        gap = jnp.zeros(a.shape[:-1] + (MAIN_COL0 - n_qkv,), a.dtype)
        return jnp.concatenate([a[..., qkv0:qkv1], gap, a[..., :qkv0], a[..., qkv1:]], axis=-1)

    def vec(a):
        return a.reshape(depth, 1, a.shape[-1]).astype(F32)

    w_r_pad = jnp.pad(w_r, ((0, 0), (0, 0), (0, LANES - N_EXPERTS))).astype(BF16)
    b_r_pad = jnp.pad(b_r.astype(F32), ((0, 0), (0, LANES - N_EXPERTS)), constant_values=NEG_INF)
    return dict(
        w_in=permute_cols(w_in).astype(BF16), b_in=vec(permute_cols(b_in)),
        ln_v_g=vec(ln_v_g), ln_v_b=vec(ln_v_b),
        w_s=w_s.astype(BF16),
        b_s=jnp.broadcast_to(b_s.astype(F32)[..., None], b_s.shape + (CHUNK,)),
        w_pa=w_pa.astype(BF16), w_pb=w_pb.astype(BF16), w_o=w_o.astype(BF16),
        ln1_g=vec(ln1_g), ln1_b=vec(ln1_b),
        w_r=w_r_pad, b_r=vec(b_r_pad),
        w_gu=w_gu, b_gu=b_gu.astype(F32)[:, :, None, :],
        w_down=w_down, b_down=b_down.astype(F32)[:, :, None, :],
        ln2_g=vec(ln2_g), ln2_b=vec(ln2_b),
    )


def _trunk(x, params, bm=512):
    n_seq, seq, _ = x.shape
    t = n_seq * seq
    depth = params["w_in"].shape[0]
    dn_alpha = (2.0 * depth) ** 0.25
    x = x.reshape(t, D_MODEL)
    xb = x.astype(BF16)
    for layer in range(depth):
        z = _inproj_main(xb, params["w_in"], params["b_in"], layer)
        attn = [_attention(_inproj_qkv(xb, params["w_in"], params["b_in"], layer, g, n_seq, seq), g)
                for g in range(len(B_PATTERNS))]
        x1, x1b, logits = _mixer(z, attn, x, params, layer, dn_alpha, seq)
        xs, pos, gates, cnt = _dispatch(logits, x1b)
        src0, plan, blk_e, n_used = _chunk_plan(cnt[:, 0, :N_EXPERTS], bm)
        ys = _ffn(xs, src0, plan, blk_e, n_used, params, layer, bm)
        x, xb = _combine(ys, pos, gates, x1, params, layer, dn_alpha)
    return x.reshape(n_seq, seq, D_MODEL)


def kernel(x_prompt, x_sample, w_in, b_in, ln_v_g, ln_v_b, w_s, b_s, w_pa, w_pb, w_o, ln1_g, ln1_b,
           w_r, b_r, w_gu, b_gu, w_down, b_down, ln2_g, ln2_b):
    params = _prepare_params(w_in, b_in, ln_v_g, ln_v_b, w_s, b_s, w_pa, w_pb, w_o, ln1_g, ln1_b,
                             w_r, b_r, w_gu, b_gu, w_down, b_down, ln2_g, ln2_b)
    n_prompt = x_prompt.shape[0]
    y = _trunk(jnp.concatenate([x_prompt, x_sample], axis=0), params)
    return (y[:n_prompt], y[n_prompt:])
```

```python
import functools

import numpy as np
import jax
import jax.numpy as jnp
from jax import lax
from jax.experimental import pallas as pl
from jax.experimental.pallas import tpu as pltpu

D_MODEL = 1024
CHUNK = 128
A_GROUPS = 8
B_PATTERNS = ((128, 1), (512, 4), (2048, 16))
B_HEADS = 4
B_HEAD_DIM = 128
B_GROUP_WIDTH = B_HEADS * B_HEAD_DIM
N_ATTN_HEADS = len(B_PATTERNS) * B_HEADS
N_SIDE = 64
D_IN = 4 * D_MODEL + 3 * len(B_PATTERNS) * B_GROUP_WIDTH
N_EXPERTS = 32
TOP_K = 4
SWIGLU_LIMIT = 7.0
SWIGLU_ALPHA = 1.702
LN_EPS = 1e-5
NEG_INF = -1e30
LANES = 128

MAIN_COLS = 4 * D_MODEL
QKV_COLS = 3 * B_GROUP_WIDTH
MAIN_COL0 = -(-len(B_PATTERNS) * QKV_COLS // D_MODEL) * D_MODEL
GELU_COLS = 2 * D_MODEL

VMEM_LIMIT = 56 * 1024 * 1024

BF16 = jnp.bfloat16
F32 = jnp.float32


def _gelu(x):
    return 0.5 * x * (1.0 + lax.erf(x * np.float32(np.sqrt(0.5))))


def _sigmoid(x):
    return 0.5 * jnp.tanh(0.5 * x) + 0.5


def _layer_norm(x, g, b):
    xc = x - jnp.mean(x, axis=-1, keepdims=True)
    var = jnp.mean(xc * xc, axis=-1, keepdims=True)
    return xc * lax.rsqrt(var + LN_EPS) * g + b


N_CHUNK = 256
MAX_ROW_STRIDE = 4


def _inproj_main_kernel(x_ref, w_ref, b_ref, z_ref, *, n_gelu_tiles):
    j = pl.program_id(1)

    def run(act):
        for c in range(z_ref.shape[1] // N_CHUNK):
            cols = slice(c * N_CHUNK, (c + 1) * N_CHUNK)
            acc = jnp.dot(x_ref[...], w_ref[:, cols], preferred_element_type=F32) + b_ref[:, cols]
            z_ref[:, cols] = act(acc).astype(BF16)

    @pl.when(j < n_gelu_tiles)
    def _():
        run(_gelu)

    @pl.when(j >= n_gelu_tiles)
    def _():
        run(lambda a: a)


def _inproj_main(xb, w_in, b_in, layer, tm=1024, tn=1024):
    t = xb.shape[0]
    return pl.pallas_call(
        functools.partial(_inproj_main_kernel, n_gelu_tiles=GELU_COLS // tn),
        grid=(t // tm, MAIN_COLS // tn),
        in_specs=[
            pl.BlockSpec((tm, D_MODEL), lambda i, j: (i, 0)),
            pl.BlockSpec((None, D_MODEL, tn), lambda i, j: (layer, 0, MAIN_COL0 // tn + j)),
            pl.BlockSpec((None, 1, tn), lambda i, j: (layer, 0, MAIN_COL0 // tn + j)),
        ],
        out_specs=pl.BlockSpec((tm, tn), lambda i, j: (i, j)),
        out_shape=jax.ShapeDtypeStruct((t, MAIN_COLS), BF16),
        compiler_params=pltpu.CompilerParams(
            dimension_semantics=("parallel", "arbitrary"), vmem_limit_bytes=VMEM_LIMIT),
        name="inproj_main",
    )(xb, w_in, b_in)


def _inproj_qkv_kernel(x_ref, w_ref, b_ref, o_ref, *scratch, dilation):
    rows = x_ref.shape[0] // dilation
    for c in range(o_ref.shape[-1] // N_CHUNK):
        cols = slice(c * N_CHUNK, (c + 1) * N_CHUNK)
        acc = jnp.dot(x_ref[...], w_ref[:, cols], preferred_element_type=F32) + b_ref[:, cols]
        if dilation == 1:
            o_ref[0, :, cols] = acc.astype(BF16)
            continue
        scr, scr2 = scratch[0], scratch[-1]
        d1 = min(dilation, MAX_ROW_STRIDE)
        d2 = dilation // d1
        for q in range(c * N_CHUNK // LANES, (c + 1) * N_CHUNK // LANES):
            scr[q] = acc[:, q * LANES - c * N_CHUNK:(q + 1) * LANES - c * N_CHUNK]
            for r1 in range(d1):
                part = scr[q, pl.ds(r1, rows * d2, stride=d1), :]
                if d2 == 1:
                    o_ref[r1, :, q * LANES:(q + 1) * LANES] = part.astype(BF16)
                    continue
                scr2[q, r1 * rows * d2:(r1 + 1) * rows * d2, :] = part
                for r2 in range(d2):
                    o_ref[r2 * d1 + r1, :, q * LANES:(q + 1) * LANES] = (
                        scr2[q, pl.ds(r1 * rows * d2 + r2, rows, stride=d2), :].astype(BF16))


def _inproj_qkv(xb, w_in, b_in, layer, group, n_seq, seq, tm=1024, tn=QKV_COLS):
    _, dilation = B_PATTERNS[group]
    tiles_per_seq = seq // tm
    col0 = group * QKV_COLS // tn
    return pl.pallas_call(
        functools.partial(_inproj_qkv_kernel, dilation=dilation),
        grid=(n_seq * tiles_per_seq, QKV_COLS // tn),
        in_specs=[
            pl.BlockSpec((tm, D_MODEL), lambda i, j: (i, 0)),
            pl.BlockSpec((None, D_MODEL, tn), lambda i, j: (layer, 0, col0 + j)),
            pl.BlockSpec((None, 1, tn), lambda i, j: (layer, 0, col0 + j)),
        ],
        out_specs=pl.BlockSpec((None, dilation, tm // dilation, tn),
                               lambda i, j: (i // tiles_per_seq, 0, i % tiles_per_seq, j)),
        out_shape=jax.ShapeDtypeStruct((n_seq, dilation, seq // dilation, QKV_COLS), BF16),
        scratch_shapes=[pltpu.VMEM((tn // LANES, tm, LANES), F32)] * (
            0 if dilation == 1 else 1 if dilation <= MAX_ROW_STRIDE else 2),
        compiler_params=pltpu.CompilerParams(
            dimension_semantics=("parallel", "arbitrary"), vmem_limit_bytes=VMEM_LIMIT),
        name=f"inproj_qkv{group}",
    )(xb, w_in, b_in)


def _attn_kernel(cur_ref, prev_ref, next_ref, bias_ref, o_ref, lse_ref, *scratch, bq):
    for rb in range(cur_ref.shape[0]):
        _attn_rows(cur_ref.at[rb], prev_ref.at[rb], next_ref.at[rb], bias_ref, o_ref.at[rb], lse_ref.at[rb],
                   scratch, bq)


def _attn_rows(cur_ref, prev_ref, next_ref, bias_ref, o_ref, lse_ref, scratch, bq):
    s_scr, p_scr = scratch[:B_HEADS], scratch[B_HEADS:]
    jq = pl.program_id(2)
    nq = pl.num_programs(2)
    n_piece = bq // N_SIDE
    col = lax.broadcasted_iota(jnp.int32, (N_SIDE, 3 * N_SIDE), 1)
    lane = lax.broadcasted_iota(jnp.int32, (N_SIDE, LANES), 1)
    scale = np.float32(B_HEAD_DIM ** -0.5)

    def window(p, part, h):
        c0 = part * B_GROUP_WIDTH + h * B_HEAD_DIM
        cols = slice(c0, c0 + B_HEAD_DIM)
        if n_piece == 1:
            return jnp.concatenate([prev_ref[:, cols], cur_ref[:, cols], next_ref[:, cols]], axis=0)
        if p == 0:
            return jnp.concatenate([prev_ref[:, cols], cur_ref[0:2 * N_SIDE, cols]], axis=0)
        if p == n_piece - 1:
            return jnp.concatenate([cur_ref[bq - 2 * N_SIDE:bq, cols], next_ref[:, cols]], axis=0)
        return cur_ref[(p - 1) * N_SIDE:(p + 2) * N_SIDE, cols]

    def scores(p, h):
        qh = cur_ref[p * N_SIDE:(p + 1) * N_SIDE, h * B_HEAD_DIM:(h + 1) * B_HEAD_DIM]
        s = lax.dot_general(qh, window(p, 1, h), (((1,), (1,)), ((), ())), preferred_element_type=F32) * scale
        if p == 0:
            s = jnp.where(jnp.logical_and(col < N_SIDE, jq == 0), NEG_INF, s)
        if p == n_piece - 1:
            s = jnp.where(jnp.logical_and(col >= 2 * N_SIDE, jq == nq - 1), NEG_INF, s)
        return s + bias_ref[h]

    def softmax(s):
        m = jnp.max(s, axis=-1, keepdims=True)
        e = jnp.exp(s - m)
        den = jnp.sum(e, axis=-1, keepdims=True)
        return (e / den).astype(BF16), m + jnp.log(den)

    def values(p, h, pn):
        o = jnp.dot(pn, window(p, 2, h), preferred_element_type=F32)
        o_ref[p * N_SIDE:(p + 1) * N_SIDE, h * B_HEAD_DIM:(h + 1) * B_HEAD_DIM] = o.astype(BF16)

    lse = {}
    for t in range(B_HEADS + 2):
        if t < B_HEADS:
            for p in range(n_piece):
                s_scr[t][p * N_SIDE:(p + 1) * N_SIDE, :] = scores(p, t)
        if 1 <= t <= B_HEADS:
            for p in range(n_piece):
                rows = slice(p * N_SIDE, (p + 1) * N_SIDE)
                p_scr[t - 1][rows, :], lse[p, t - 1] = softmax(s_scr[t - 1][rows, :])
        if t >= 2:
            for p in range(n_piece):
                values(p, t - 2, p_scr[t - 2][p * N_SIDE:(p + 1) * N_SIDE, :])
    for p in range(n_piece):
        tile = jnp.zeros((N_SIDE, LANES), F32)
        for h in range(B_HEADS):
            tile = jnp.where(lane == h, lse[p, h], tile)
        lse_ref[p * N_SIDE:(p + 1) * N_SIDE, :] = tile


def _attn_bias(group):
    _, dilation = B_PATTERNS[group]
    slopes = np.array([2.0 ** (-8.0 * (group * B_HEADS + h + 1) / N_ATTN_HEADS) for h in range(B_HEADS)],
                      dtype=np.float32)
    a = np.arange(N_SIDE)[:, None]
    c = np.arange(3 * N_SIDE)[None, :]
    rel = c - N_SIDE - a
    alibi = (-(slopes[:, None, None] * (dilation * np.abs(rel))[None])).astype(np.float32)
    return np.where((np.abs(rel) <= N_SIDE)[None], alibi, np.float32(NEG_INF)).astype(np.float32)


def _attention(qkv, group, rows_per_step=1024):
    n_seq, dilation, sub_len, _ = qkv.shape
    bq = min(rows_per_step, sub_len)
    nq = sub_len // bq
    rb = min(dilation, rows_per_step // bq)
    halo_per_bq = bq // N_SIDE
    n_halo = sub_len // N_SIDE
    return pl.pallas_call(
        functools.partial(_attn_kernel, bq=bq),
        grid=(n_seq, dilation // rb, nq),
        in_specs=[pl.BlockSpec((None, rb, bq, QKV_COLS), lambda b, r, j: (b, r, j, 0)),
                  pl.BlockSpec((None, rb, N_SIDE, QKV_COLS),
                               lambda b, r, j: (b, r, jnp.maximum(j * halo_per_bq - 1, 0), 0)),
                  pl.BlockSpec((None, rb, N_SIDE, QKV_COLS),
                               lambda b, r, j: (b, r, jnp.minimum((j + 1) * halo_per_bq, n_halo - 1), 0)),
                  pl.BlockSpec((B_HEADS, N_SIDE, 3 * N_SIDE), lambda b, r, j: (0, 0, 0))],
        out_specs=[pl.BlockSpec((None, rb, bq, B_GROUP_WIDTH), lambda b, r, j: (b, r, j, 0)),
                   pl.BlockSpec((None, rb, bq, LANES), lambda b, r, j: (b, r, j, 0))],
        out_shape=[jax.ShapeDtypeStruct((n_seq, dilation, sub_len, B_GROUP_WIDTH), BF16),
                   jax.ShapeDtypeStruct((n_seq, dilation, sub_len, LANES), F32)],
        scratch_shapes=([pltpu.VMEM((bq, 3 * N_SIDE), F32)] * B_HEADS
                        + [pltpu.VMEM((bq, 3 * N_SIDE), BF16)] * B_HEADS),
        compiler_params=pltpu.CompilerParams(
            dimension_semantics=("parallel", "parallel", "arbitrary"), vmem_limit_bytes=VMEM_LIMIT),
        name=f"attn{group}",
    )(qkv, qkv, qkv, jnp.asarray(_attn_bias(group)))


def _mixer_kernel(u_ref, v_ref, ga_ref, gb_ref, o0_ref, o1_ref, o2_ref, l0_ref, l1_ref, l2_ref, x_ref,
                  lnvg_ref, lnvb_ref, ws_ref, bs_ref, wpa_ref, wpb_ref, wo_ref, ln1g_ref, ln1b_ref,
                  wr_ref, br_ref, x1_ref, x1b_ref, logit_ref, a_scr, b_scr, o_scr, l_scr, t_scr, *, tm, dn_alpha):
    for g, (o_ref, l_ref) in enumerate(((o0_ref, l0_ref), (o1_ref, l1_ref), (o2_ref, l2_ref))):
        dilation = o_ref.shape[0]
        rows = tm // dilation
        d1 = min(dilation, MAX_ROW_STRIDE)
        d2 = dilation // d1

        def pieces(r):
            return [o_ref[r, :, h * B_HEAD_DIM:(h + 1) * B_HEAD_DIM].astype(F32) for h in range(B_HEADS)] + [l_ref[r]]

        def place(idx, vals):
            for h in range(B_HEADS):
                o_scr[g, h, idx, :] = vals[h]
            l_scr[g, idx, :] = vals[B_HEADS]

        for r1 in range(d1):
            idx = pl.ds(r1, rows * d2, stride=d1) if d1 > 1 else slice(None)
            if d2 == 1:
                place(idx, pieces(r1))
                continue
            for r2 in range(d2):
                for j, val in enumerate(pieces(r2 * d1 + r1)):
                    t_scr[r1, j, pl.ds(r2, rows, stride=d2), :] = val
            place(idx, [t_scr[r1, j] for j in range(B_HEADS + 1)])
    n_groups = len(B_PATTERNS)

    def branches(r0, nr):
        rows = slice(r0, r0 + nr)
        vn = _layer_norm(v_ref[rows, :].astype(F32), lnvg_ref[...], lnvb_ref[...]).astype(BF16)
        for g in range(A_GROUPS):
            c0 = g * CHUNK
            chunks = jnp.concatenate([vn[q0:q0 + CHUNK, c0:c0 + CHUNK] for q0 in range(0, nr, CHUNK)], axis=1)
            mixed_all = jnp.dot(ws_ref[g], chunks, preferred_element_type=F32)
            for c in range(nr // CHUNK):
                q0 = c * CHUNK
                mixed = mixed_all[:, q0:q0 + CHUNK] + bs_ref[g]
                a = u_ref[r0 + q0:r0 + q0 + CHUNK, c0:c0 + CHUNK].astype(F32) * mixed
                a_scr[r0 + q0:r0 + q0 + CHUNK, c0:c0 + CHUNK] = a.astype(BF16)
        ls = [l_scr[g, rows, :] for g in range(n_groups)]
        mx = jnp.maximum(jnp.maximum(ls[0], ls[1]), ls[2])
        es = [jnp.exp(l - mx) for l in ls]
        den = es[0] + es[1] + es[2]
        wts = [e / den for e in es]
        for h in range(B_HEADS):
            c0 = h * B_HEAD_DIM
            acc = jnp.zeros((nr, B_HEAD_DIM), F32)
            for g in range(n_groups):
                acc = acc + wts[g][:, h:h + 1] * o_scr[g, h, rows, :]
            b_scr[rows, c0:c0 + B_HEAD_DIM] = acc.astype(BF16)

    def project(r0, nr):
        rows = slice(r0, r0 + nr)
        pa = jnp.dot(a_scr[rows, :], wpa_ref[...], preferred_element_type=F32)
        pb = jnp.dot(b_scr[rows, :], wpb_ref[...], preferred_element_type=F32)
        merged = _sigmoid(ga_ref[rows, :].astype(F32)) * pa + _sigmoid(gb_ref[rows, :].astype(F32)) * pb
        mix = jnp.dot(merged.astype(BF16), wo_ref[...], preferred_element_type=F32)
        x1 = _layer_norm(dn_alpha * x_ref[rows, :] + mix, ln1g_ref[...], ln1b_ref[...])
        x1_ref[rows, :] = x1
        x1b = x1.astype(BF16)
        x1b_ref[rows, :] = x1b
        logit_ref[rows, :] = jnp.dot(x1b, wr_ref[...], preferred_element_type=F32) + br_ref[...]

    half = tm // 2
    branches(0, half)
    project(0, half)
    branches(half, half)
    project(half, half)


def _mixer(z, attn, x, p, layer, dn_alpha, seq, tm=512):
    t = x.shape[0]
    tiles_per_seq = seq // tm

    def zspec(cb):
        return pl.BlockSpec((tm, D_MODEL), lambda i: (i, cb))

    def row(width):
        return pl.BlockSpec((tm, width), lambda i: (i, 0))

    def split(a):
        _, dilation, _, width = a.shape
        return pl.BlockSpec((None, dilation, tm // dilation, width),
                            lambda i: (i // tiles_per_seq, 0, i % tiles_per_seq, 0))

    def vec(width):
        return pl.BlockSpec((None, 1, width), lambda i: (layer, 0, 0))

    def mat(k, n):
        return pl.BlockSpec((None, k, n), lambda i: (layer, 0, 0))

    (o0, l0), (o1, l1), (o2, l2) = attn
    return pl.pallas_call(
        functools.partial(_mixer_kernel, tm=tm, dn_alpha=np.float32(dn_alpha)),
        grid=(t // tm,),
        in_specs=[zspec(0), zspec(1), zspec(2), zspec(3),
                  split(o0), split(o1), split(o2), split(l0), split(l1), split(l2), row(D_MODEL),
                  vec(D_MODEL), vec(D_MODEL),
                  pl.BlockSpec((None, A_GROUPS, CHUNK, CHUNK), lambda i: (layer, 0, 0, 0)),
                  pl.BlockSpec((None, A_GROUPS, CHUNK, CHUNK), lambda i: (layer, 0, 0, 0)),
                  mat(D_MODEL, D_MODEL), mat(B_GROUP_WIDTH, D_MODEL), mat(D_MODEL, D_MODEL),
                  vec(D_MODEL), vec(D_MODEL), mat(D_MODEL, LANES), vec(LANES)],
        out_specs=[row(D_MODEL), row(D_MODEL), row(LANES)],
        out_shape=[jax.ShapeDtypeStruct((t, D_MODEL), F32), jax.ShapeDtypeStruct((t, D_MODEL), BF16),
                   jax.ShapeDtypeStruct((t, LANES), F32)],
        scratch_shapes=[pltpu.VMEM((tm, D_MODEL), BF16), pltpu.VMEM((tm, B_GROUP_WIDTH), BF16),
                        pltpu.VMEM((len(B_PATTERNS), B_HEADS, tm, B_HEAD_DIM), F32),
                        pltpu.VMEM((len(B_PATTERNS), tm, LANES), F32),
                        pltpu.VMEM((MAX_ROW_STRIDE, B_HEADS + 1, tm // MAX_ROW_STRIDE, LANES), F32)],
        compiler_params=pltpu.CompilerParams(
            dimension_semantics=("parallel",), vmem_limit_bytes=VMEM_LIMIT),
        name="mixer",
    )(z, z, z, z, o0, o1, o2, l0, l1, l2, x,
      p["ln_v_g"], p["ln_v_b"], p["w_s"], p["b_s"], p["w_pa"], p["w_pb"], p["w_o"],
      p["ln1_g"], p["ln1_b"], p["w_r"], p["b_r"])


TS = 512
CHUNK_ROWS = 8
REGION = TS * TOP_K + N_EXPERTS * CHUNK_ROWS
P_ROWS = 256


def _chunks_per_expert(cnt):
    return jnp.floor((cnt + (CHUNK_ROWS - 1)) * (1.0 / CHUNK_ROWS))


def _dispatch_kernel(logit_ref, x_ref, upper_ref, lower_ref, xs_ref, pos_ref, gate_ref, cnt_ref):
    l = jnp.transpose(logit_ref[...])[0:N_EXPERTS, :]
    sub = lax.broadcasted_iota(jnp.int32, (N_EXPERTS, TS), 0)
    onehots, vals = [], []
    for _ in range(TOP_K):
        mx = jnp.max(l, axis=0, keepdims=True)
        idx = jnp.min(jnp.where(l == mx, sub, N_EXPERTS), axis=0, keepdims=True)
        oh = sub == idx
        l = jnp.where(oh, -jnp.inf, l)
        onehots.append(oh)
        vals.append(mx)
    es = [jnp.exp(v - vals[0]) for v in vals]
    den = es[0] + es[1] + es[2] + es[3]
    multi = jnp.zeros((N_EXPERTS, TS), F32)
    for oh in onehots:
        multi = multi + oh.astype(F32)
    multi_b = jnp.concatenate([multi, jnp.zeros((LANES - N_EXPERTS, TS), F32)], axis=0).astype(BF16)
    cnt_row = lax.dot_general(jnp.ones((8, TS), BF16), multi_b, (((1,), (1,)), ((), ())),
                              preferred_element_type=F32)
    cnt_ref[...] = cnt_row.astype(jnp.int32)
    cnt_col = jnp.concatenate([jnp.sum(multi, axis=1, keepdims=True),
                               jnp.zeros((LANES - N_EXPERTS, 1), F32)], axis=0)
    n_chunks = jnp.broadcast_to(_chunks_per_expert(cnt_col), (LANES, LANES)).astype(BF16)
    run_off = CHUNK_ROWS * jnp.dot(lower_ref[...], n_chunks, preferred_element_type=F32)[0:N_EXPERTS, 0:1]
    rank = jnp.dot(multi_b, upper_ref[...], preferred_element_type=F32)[0:N_EXPERTS, :]
    slot = rank + run_off
    pos_rows = [jnp.sum(jnp.where(onehots[k], slot, 0.0), axis=0, keepdims=True) for k in range(TOP_K)]
    pad = jnp.full((LANES - TOP_K, TS), -1.0, F32)
    pos_all = jnp.concatenate(pos_rows + [pad], axis=0)
    pos_ref[...] = jnp.transpose(pos_all).astype(jnp.int32)
    gate_ref[...] = jnp.transpose(jnp.concatenate([e / den for e in es] + [pad], axis=0))
    live = pl.program_id(0) < pl.num_programs(0) - 1
    pos_t = jnp.where(live, pos_all[0:8, :], -1.0)
    row = lax.broadcasted_iota(jnp.int32, (P_ROWS, TS), 0).astype(F32).astype(BF16)
    for c in range(REGION // P_ROWS):
        rel = pos_t - float(c * P_ROWS)
        rel = jnp.where(jnp.logical_and(rel >= 0.0, rel < float(P_ROWS)), rel, -1.0).astype(BF16)
        perm = jnp.zeros((P_ROWS, TS), BF16)
        for k in range(TOP_K):
            perm = jnp.where(row == rel[k:k + 1, :], jnp.ones((), BF16), perm)
        xs_ref[c * P_ROWS:(c + 1) * P_ROWS, :] = jnp.dot(
            perm, x_ref[...], preferred_element_type=F32).astype(BF16)


def _dispatch(logits, x1b):
    t = logits.shape[0]
    n_tiles = t // TS
    upper = jnp.asarray(np.triu(np.ones((TS, TS), np.float32), 1), dtype=BF16)
    lower = jnp.asarray(np.tril(np.ones((LANES, LANES), np.float32), -1), dtype=BF16)

    def tile(i):
        return jnp.minimum(i, n_tiles - 1)

    return pl.pallas_call(
        _dispatch_kernel,
        grid=(n_tiles + 1,),
        in_specs=[pl.BlockSpec((TS, LANES), lambda i: (tile(i), 0)),
                  pl.BlockSpec((TS, D_MODEL), lambda i: (tile(i), 0)),
                  pl.BlockSpec((TS, TS), lambda i: (0, 0)),
                  pl.BlockSpec((LANES, LANES), lambda i: (0, 0))],
        out_specs=[pl.BlockSpec((REGION, D_MODEL), lambda i: (i, 0)),
                   pl.BlockSpec((TS, LANES), lambda i: (tile(i), 0)),
                   pl.BlockSpec((TS, LANES), lambda i: (tile(i), 0)),
                   pl.BlockSpec((None, 8, LANES), lambda i: (tile(i), 0, 0))],
        out_shape=[jax.ShapeDtypeStruct(((n_tiles + 1) * REGION, D_MODEL), BF16),
                   jax.ShapeDtypeStruct((t, LANES), jnp.int32), jax.ShapeDtypeStruct((t, LANES), F32),
                   jax.ShapeDtypeStruct((n_tiles, 8, LANES), jnp.int32)],
        compiler_params=pltpu.CompilerParams(
            dimension_semantics=("arbitrary",), vmem_limit_bytes=VMEM_LIMIT),
        name="dispatch",
    )(logits, x1b, upper, lower)


def _chunk_plan(cnt, bm):
    assert REGION >= 3 * bm
    n_tiles = cnt.shape[0]
    ch = bm // CHUNK_ROWS
    n_rows_max = n_tiles * REGION + N_EXPERTS * (bm - CHUNK_ROWS)
    n_blk = -(-n_rows_max // bm)
    run_chunks = (cnt + CHUNK_ROWS - 1) // CHUNK_ROWS
    run_off = (jnp.cumsum(run_chunks, axis=1) - run_chunks) * CHUNK_ROWS
    rows_e = jnp.sum(run_chunks, axis=0) * CHUNK_ROWS
    padded_e = (rows_e + bm - 1) // bm * bm
    pend = jnp.cumsum(padded_e)
    pstart = pend - padded_e
    runs_t = run_chunks.T
    run_start = pstart[:, None] // CHUNK_ROWS + jnp.cumsum(runs_t, axis=1) - runs_t
    run_row = jnp.arange(n_tiles, dtype=jnp.int32)[None, :] * REGION + run_off.T
    blk = jnp.arange(n_blk + 1, dtype=jnp.int32)
    blk_e = jnp.minimum(jnp.sum((pend[None, :] <= blk[:, None] * bm).astype(jnp.int32), axis=1), N_EXPERTS - 1)
    is_e = blk_e[:, None, None] == jnp.arange(N_EXPERTS, dtype=jnp.int32)[None, :, None]
    c = (blk[:, None] * ch + jnp.arange(ch, dtype=jnp.int32)[None, :]).reshape(-1)

    def of_run(v):
        v_b = jnp.sum(jnp.where(is_e, v[None], 0), axis=1)
        step = v_b - jnp.concatenate([jnp.zeros_like(v_b[:, :1]), v_b[:, :-1]], axis=1)
        return jnp.sum(jnp.where(started, step[:, None, :], 0), axis=-1).reshape(-1)

    start_b = jnp.sum(jnp.where(is_e, run_start[None], 0), axis=1)
    started = start_b[:, None, :] <= c.reshape(n_blk + 1, ch)[:, :, None]
    j = c - of_run(run_start)
    valid = j < of_run(runs_t)
    row = of_run(run_row) + j * CHUNK_ROWS
    spare = n_tiles * REGION + (c % ch) * CHUNK_ROWS
    src = jnp.where(valid, row, spare + 2 * bm).astype(jnp.int32).reshape(n_blk + 1, 1, ch)
    dst = jnp.where(valid, row, spare + ((c // ch) % 2) * bm).astype(jnp.int32).reshape(n_blk + 1, 1, ch)
    dst = jnp.concatenate([dst[n_blk:], dst[:n_blk]])
    blk_e = blk_e[:n_blk].astype(jnp.int32)
    n_used = (pend[-1:] // bm).astype(jnp.int32)
    plan = jnp.concatenate([src[1:], dst[:n_blk], dst[1:]], axis=-1)
    return src[:1], plan, blk_e, n_used


def _ffn_kernel(blk_e_ref, n_used_ref, src0_ref, plan_ref, xs_hbm, wgu_ref, bgu_ref, wd_ref,
                bd_ref, ys_hbm, xbuf, ybuf, wgu_b, wd_b, in_sem, out_sem, *, bm):
    i = pl.program_id(0)
    n_used = n_used_ref[0]
    slot = i % 2
    ch = bm // CHUNK_ROWS

    def gather(tbl_ref, col0, s):
        for j in range(ch):
            row = pl.multiple_of(tbl_ref[0, col0 + j], CHUNK_ROWS)
            pltpu.make_async_copy(xs_hbm.at[pl.ds(row, CHUNK_ROWS)],
                                  xbuf.at[s, pl.ds(j * CHUNK_ROWS, CHUNK_ROWS)], in_sem.at[s]).start()

    def scatter(tbl_ref, col0, s):
        for j in range(ch):
            row = pl.multiple_of(tbl_ref[0, col0 + j], CHUNK_ROWS)
            pltpu.make_async_copy(ybuf.at[s, pl.ds(j * CHUNK_ROWS, CHUNK_ROWS)],
                                  ys_hbm.at[pl.ds(row, CHUNK_ROWS)], out_sem.at[s]).start()

    def wait_in(s):
        pltpu.make_async_copy(xs_hbm.at[pl.ds(0, bm)], xbuf.at[s], in_sem.at[s]).wait()

    def wait_out(s):
        pltpu.make_async_copy(ybuf.at[s], ys_hbm.at[pl.ds(0, bm)], out_sem.at[s]).wait()

    used = i < n_used
    e_changed = jnp.logical_or(i == 0, blk_e_ref[i] != blk_e_ref[jnp.maximum(i - 1, 0)])

    @pl.when(jnp.logical_and(i == 0, used))
    def _():
        gather(src0_ref, 0, 0)
        ybuf[1] = jnp.zeros((bm, D_MODEL), BF16)

    @pl.when(jnp.logical_and(used, e_changed))
    def _():
        wgu_b[...] = wgu_ref[...].astype(BF16)
        wd_b[...] = wd_ref[...].astype(BF16)

    @pl.when(jnp.logical_and(used, i >= 1))
    def _():
        wait_out(slot)

    @pl.when(used)
    def _():
        wait_in(slot)
        gu = jnp.dot(xbuf[slot], wgu_b[...], preferred_element_type=F32) + bgu_ref[...]
        gather(plan_ref, 0, 1 - slot)
        scatter(plan_ref, ch, 1 - slot)
        gate = jnp.minimum(gu[:, :D_MODEL], SWIGLU_LIMIT)
        up = jnp.clip(gu[:, D_MODEL:], -SWIGLU_LIMIT, SWIGLU_LIMIT)
        h = (up + 1.0) * gate * _sigmoid(SWIGLU_ALPHA * gate)
        y = jnp.dot(h.astype(BF16), wd_b[...], preferred_element_type=F32) + bd_ref[...]
        ybuf[slot] = y.astype(BF16)

    @pl.when(i == n_used - 1)
    def _():
        scatter(plan_ref, 2 * ch, slot)
        wait_in(1 - slot)
        wait_out(1 - slot)
        wait_out(slot)


def _ffn(xs, src0, plan, blk_e, n_used, p, layer, bm):
    n_blk = blk_e.shape[0]
    ch = src0.shape[-1]
    smem = functools.partial(pl.BlockSpec, memory_space=pltpu.SMEM)
    grid_spec = pltpu.PrefetchScalarGridSpec(
        num_scalar_prefetch=2,
        grid=(n_blk,),
        in_specs=[smem((None, 1, ch), lambda i, be, nu: (0, 0, 0)),
                  smem((None, 1, 3 * ch), lambda i, be, nu: (i, 0, 0)),
                  pl.BlockSpec(memory_space=pl.ANY),
                  pl.BlockSpec((None, None, D_MODEL, 2 * D_MODEL), lambda i, be, nu: (layer, be[i], 0, 0)),
                  pl.BlockSpec((None, None, 1, 2 * D_MODEL), lambda i, be, nu: (layer, be[i], 0, 0)),
                  pl.BlockSpec((None, None, D_MODEL, D_MODEL), lambda i, be, nu: (layer, be[i], 0, 0)),
                  pl.BlockSpec((None, None, 1, D_MODEL), lambda i, be, nu: (layer, be[i], 0, 0))],
        out_specs=pl.BlockSpec(memory_space=pl.ANY),
        scratch_shapes=[pltpu.VMEM((2, bm, D_MODEL), BF16), pltpu.VMEM((2, bm, D_MODEL), BF16),
                        pltpu.VMEM((D_MODEL, 2 * D_MODEL), BF16), pltpu.VMEM((D_MODEL, D_MODEL), BF16),
                        pltpu.SemaphoreType.DMA((2,)), pltpu.SemaphoreType.DMA((2,))],
    )
    return pl.pallas_call(
        functools.partial(_ffn_kernel, bm=bm),
        grid_spec=grid_spec,
        out_shape=jax.ShapeDtypeStruct(xs.shape, BF16),
        input_output_aliases={4: 0},
        compiler_params=pltpu.CompilerParams(
            dimension_semantics=("arbitrary",), vmem_limit_bytes=VMEM_LIMIT),
        name="ffn",
    )(blk_e, n_used, src0, plan, xs, p["w_gu"], p["b_gu"], p["w_down"], p["b_down"])


G_ROWS = 128


def _combine_kernel(ys_ref, pos_ref, gate_ref, x_ref, g_ref, b_ref, x2_ref, x2b_ref, g_scr, *, dn_alpha):
    nr = G_ROWS
    lane = lax.broadcasted_iota(jnp.int32, (nr, LANES), 1)
    for r0 in range(0, TS, nr):
        rows = slice(r0, r0 + nr)
        tile_of, gate = [], []
        for k in range(TOP_K):
            rel = jnp.broadcast_to(pos_ref[rows, k:k + 1], (nr, LANES)) - lane
            hit = jnp.where((rel & (LANES - 1)) == 0, rel >> (LANES.bit_length() - 1), -1)
            tile_of.append(hit.astype(F32).astype(BF16))
            gate.append(jnp.broadcast_to(gate_ref[rows, k:k + 1], (nr, LANES)).astype(BF16))
        for c in range(REGION // LANES):
            g = jnp.zeros((nr, LANES), BF16)
            for k in range(TOP_K):
                g = jnp.where(tile_of[k] == float(c), gate[k], g)
            g_scr[rows, c * LANES:(c + 1) * LANES] = g
        y = jnp.dot(g_scr[rows, :], ys_ref[...], preferred_element_type=F32)
        x2 = _layer_norm(dn_alpha * x_ref[rows, :] + y, g_ref[...], b_ref[...])
        x2_ref[rows, :] = x2
        x2b_ref[rows, :] = x2.astype(BF16)


def _combine(ys, pos, gates, x1, p, layer, dn_alpha):
    t = x1.shape[0]
    return pl.pallas_call(
        functools.partial(_combine_kernel, dn_alpha=np.float32(dn_alpha)),
        grid=(t // TS,),
        in_specs=[pl.BlockSpec((REGION, D_MODEL), lambda i: (i, 0)),
                  pl.BlockSpec((TS, LANES), lambda i: (i, 0)),
                  pl.BlockSpec((TS, LANES), lambda i: (i, 0)),
                  pl.BlockSpec((TS, D_MODEL), lambda i: (i, 0)),
                  pl.BlockSpec((None, 1, D_MODEL), lambda i: (layer, 0, 0)),
                  pl.BlockSpec((None, 1, D_MODEL), lambda i: (layer, 0, 0))],
        out_specs=[pl.BlockSpec((TS, D_MODEL), lambda i: (i, 0)),
                   pl.BlockSpec((TS, D_MODEL), lambda i: (i, 0))],
        out_shape=[jax.ShapeDtypeStruct((t, D_MODEL), F32), jax.ShapeDtypeStruct((t, D_MODEL), BF16)],
        scratch_shapes=[pltpu.VMEM((TS, REGION), BF16)],
        compiler_params=pltpu.CompilerParams(
            dimension_semantics=("parallel",), vmem_limit_bytes=VMEM_LIMIT),
        name="combine",
    )(ys, pos, gates, x1, p["ln2_g"], p["ln2_b"])


def _prepare_params(w_in, b_in, ln_v_g, ln_v_b, w_s, b_s, w_pa, w_pb, w_o, ln1_g, ln1_b,
                    w_r, b_r, w_gu, b_gu, w_down, b_down, ln2_g, ln2_b):
    depth = w_in.shape[0]
    n_qkv = 3 * len(B_PATTERNS) * B_GROUP_WIDTH
    qkv0, qkv1 = 2 * D_MODEL, 2 * D_MODEL + n_qkv

    def permute_cols(a):
        gap = jnp.zeros(a.shape[:-1] + (MAIN_COL0 - n_qkv,), a.dtype)
        return jnp.concatenate([a[..., qkv0:qkv1], gap, a[..., :qkv0], a[..., qkv1:]], axis=-1)

    def vec(a):
        return a.reshape(depth, 1, a.shape[-1]).astype(F32)

    w_r_pad = jnp.pad(w_r, ((0, 0), (0, 0), (0, LANES - N_EXPERTS))).astype(BF16)
    b_r_pad = jnp.pad(b_r.astype(F32), ((0, 0), (0, LANES - N_EXPERTS)), constant_values=NEG_INF)
    return dict(
        w_in=permute_cols(w_in).astype(BF16), b_in=vec(permute_cols(b_in)),
        ln_v_g=vec(ln_v_g), ln_v_b=vec(ln_v_b),
        w_s=w_s.astype(BF16),
        b_s=jnp.broadcast_to(b_s.astype(F32)[..., None], b_s.shape + (CHUNK,)),
        w_pa=w_pa.astype(BF16), w_pb=w_pb.astype(BF16), w_o=w_o.astype(BF16),
        ln1_g=vec(ln1_g), ln1_b=vec(ln1_b),
        w_r=w_r_pad, b_r=vec(b_r_pad),
        w_gu=w_gu, b_gu=b_gu.astype(F32)[:, :, None, :],
        w_down=w_down, b_down=b_down.astype(F32)[:, :, None, :],
        ln2_g=vec(ln2_g), ln2_b=vec(ln2_b),
    )


def _trunk(x, params, bm=512):
    n_seq, seq, _ = x.shape
    t = n_seq * seq
    depth = params["w_in"].shape[0]
    dn_alpha = (2.0 * depth) ** 0.25
    x = x.reshape(t, D_MODEL)
    xb = x.astype(BF16)
    for layer in range(depth):
        z = _inproj_main(xb, params["w_in"], params["b_in"], layer)
        attn = [_attention(_inproj_qkv(xb, params["w_in"], params["b_in"], layer, g, n_seq, seq), g)
                for g in range(len(B_PATTERNS))]
        x1, x1b, logits = _mixer(z, attn, x, params, layer, dn_alpha, seq)
        xs, pos, gates, cnt = _dispatch(logits, x1b)
        src0, plan, blk_e, n_used = _chunk_plan(cnt[:, 0, :N_EXPERTS], bm)
        ys = _ffn(xs, src0, plan, blk_e, n_used, params, layer, bm)
        x, xb = _combine(ys, pos, gates, x1, params, layer, dn_alpha)
    return x.reshape(n_seq, seq, D_MODEL)


def kernel(x_prompt, x_sample, w_in, b_in, ln_v_g, ln_v_b, w_s, b_s, w_pa, w_pb, w_o, ln1_g, ln1_b,
           w_r, b_r, w_gu, b_gu, w_down, b_down, ln2_g, ln2_b):
    params = _prepare_params(w_in, b_in, ln_v_g, ln_v_b, w_s, b_s, w_pa, w_pb, w_o, ln1_g, ln1_b,
                             w_r, b_r, w_gu, b_gu, w_down, b_down, ln2_g, ln2_b)
    n_prompt = x_prompt.shape[0]
    y = _trunk(jnp.concatenate([x_prompt, x_sample], axis=0), params)
    return (y[:n_prompt], y[n_prompt:])
```

```python
import functools

import numpy as np
import jax
import jax.numpy as jnp
from jax import lax
from jax.experimental import pallas as pl
from jax.experimental.pallas import tpu as pltpu

D_MODEL = 1024
CHUNK = 128
A_GROUPS = 8
B_PATTERNS = ((128, 1), (512, 4), (2048, 16))
B_HEADS = 4
B_HEAD_DIM = 128
B_GROUP_WIDTH = B_HEADS * B_HEAD_DIM
N_ATTN_HEADS = len(B_PATTERNS) * B_HEADS
N_SIDE = 64
D_IN = 4 * D_MODEL + 3 * len(B_PATTERNS) * B_GROUP_WIDTH
N_EXPERTS = 32
TOP_K = 4
SWIGLU_LIMIT = 7.0
SWIGLU_ALPHA = 1.702
LN_EPS = 1e-5
NEG_INF = -1e30
LANES = 128

MAIN_COLS = 4 * D_MODEL
QKV_COLS = 3 * B_GROUP_WIDTH
MAIN_COL0 = -(-len(B_PATTERNS) * QKV_COLS // D_MODEL) * D_MODEL
GELU_COLS = 2 * D_MODEL

VMEM_LIMIT = 56 * 1024 * 1024

BF16 = jnp.bfloat16
F32 = jnp.float32


def _gelu(x):
    return 0.5 * x * (1.0 + lax.erf(x * np.float32(np.sqrt(0.5))))


def _sigmoid(x):
    return 0.5 * jnp.tanh(0.5 * x) + 0.5


def _layer_norm(x, g, b):
    xc = x - jnp.mean(x, axis=-1, keepdims=True)
    var = jnp.mean(xc * xc, axis=-1, keepdims=True)
    return xc * lax.rsqrt(var + LN_EPS) * g + b


N_CHUNK = 256
MAX_ROW_STRIDE = 4


def _inproj_main_kernel(x_ref, w_ref, b_ref, z_ref, *, n_gelu_tiles):
    j = pl.program_id(1)

    def run(act):
        for c in range(z_ref.shape[1] // N_CHUNK):
            cols = slice(c * N_CHUNK, (c + 1) * N_CHUNK)
            acc = jnp.dot(x_ref[...], w_ref[:, cols], preferred_element_type=F32) + b_ref[:, cols]
            z_ref[:, cols] = act(acc).astype(BF16)

    @pl.when(j < n_gelu_tiles)
    def _():
        run(_gelu)

    @pl.when(j >= n_gelu_tiles)
    def _():
        run(lambda a: a)


def _inproj_main(xb, w_in, b_in, layer, tm=2048, tn=1024):
    t = xb.shape[0]
    return pl.pallas_call(
        functools.partial(_inproj_main_kernel, n_gelu_tiles=GELU_COLS // tn),
        grid=(t // tm, MAIN_COLS // tn),
        in_specs=[
            pl.BlockSpec((tm, D_MODEL), lambda i, j: (i, 0)),
            pl.BlockSpec((None, D_MODEL, tn), lambda i, j: (layer, 0, MAIN_COL0 // tn + j)),
            pl.BlockSpec((None, 1, tn), lambda i, j: (layer, 0, MAIN_COL0 // tn + j)),
        ],
        out_specs=pl.BlockSpec((tm, tn), lambda i, j: (i, j)),
        out_shape=jax.ShapeDtypeStruct((t, MAIN_COLS), BF16),
        compiler_params=pltpu.CompilerParams(
            dimension_semantics=("parallel", "arbitrary"), vmem_limit_bytes=VMEM_LIMIT),
        name="inproj_main",
    )(xb, w_in, b_in)


def _inproj_qkv_kernel(x_ref, w_ref, b_ref, o_ref, *scratch, dilation):
    rows = x_ref.shape[0] // dilation
    for c in range(o_ref.shape[-1] // N_CHUNK):
        cols = slice(c * N_CHUNK, (c + 1) * N_CHUNK)
        acc = jnp.dot(x_ref[...], w_ref[:, cols], preferred_element_type=F32) + b_ref[:, cols]
        if dilation == 1:
            o_ref[0, :, cols] = acc.astype(BF16)
            continue
        scr, scr2 = scratch[0], scratch[-1]
        d1 = min(dilation, MAX_ROW_STRIDE)
        d2 = dilation // d1
        for q in range(c * N_CHUNK // LANES, (c + 1) * N_CHUNK // LANES):
            scr[q] = acc[:, q * LANES - c * N_CHUNK:(q + 1) * LANES - c * N_CHUNK]
            for r1 in range(d1):
                part = scr[q, pl.ds(r1, rows * d2, stride=d1), :]
                if d2 == 1:
                    o_ref[r1, :, q * LANES:(q + 1) * LANES] = part.astype(BF16)
                    continue
                scr2[q, r1 * rows * d2:(r1 + 1) * rows * d2, :] = part
                for r2 in range(d2):
                    o_ref[r2 * d1 + r1, :, q * LANES:(q + 1) * LANES] = (
                        scr2[q, pl.ds(r1 * rows * d2 + r2, rows, stride=d2), :].astype(BF16))


def _inproj_qkv(xb, w_in, b_in, layer, group, n_seq, seq, tn=QKV_COLS):
    _, dilation = B_PATTERNS[group]
    tm = 2048 if dilation <= MAX_ROW_STRIDE else 1024
    tiles_per_seq = seq // tm
    col0 = group * QKV_COLS // tn
    return pl.pallas_call(
        functools.partial(_inproj_qkv_kernel, dilation=dilation),
        grid=(n_seq * tiles_per_seq, QKV_COLS // tn),
        in_specs=[
            pl.BlockSpec((tm, D_MODEL), lambda i, j: (i, 0)),
            pl.BlockSpec((None, D_MODEL, tn), lambda i, j: (layer, 0, col0 + j)),
            pl.BlockSpec((None, 1, tn), lambda i, j: (layer, 0, col0 + j)),
        ],
        out_specs=pl.BlockSpec((None, dilation, tm // dilation, tn),
                               lambda i, j: (i // tiles_per_seq, 0, i % tiles_per_seq, j)),
        out_shape=jax.ShapeDtypeStruct((n_seq, dilation, seq // dilation, QKV_COLS), BF16),
        scratch_shapes=[pltpu.VMEM((tn // LANES, tm, LANES), F32)] * (
            0 if dilation == 1 else 1 if dilation <= MAX_ROW_STRIDE else 2),
        compiler_params=pltpu.CompilerParams(
            dimension_semantics=("parallel", "arbitrary"), vmem_limit_bytes=VMEM_LIMIT),
        name=f"inproj_qkv{group}",
    )(xb, w_in, b_in)


def _attn_kernel(cur_ref, prev_ref, next_ref, bias_ref, o_ref, lse_ref, *scratch, bq):
    for rb in range(cur_ref.shape[0]):
        _attn_rows(cur_ref.at[rb], prev_ref.at[rb], next_ref.at[rb], bias_ref, o_ref.at[rb], lse_ref.at[rb],
                   scratch, bq)


def _attn_rows(cur_ref, prev_ref, next_ref, bias_ref, o_ref, lse_ref, scratch, bq):
    s_scr, p_scr = scratch[:B_HEADS], scratch[B_HEADS:]
    jq = pl.program_id(2)
    nq = pl.num_programs(2)
    n_piece = bq // N_SIDE
    col = lax.broadcasted_iota(jnp.int32, (N_SIDE, 3 * N_SIDE), 1)
    lane = lax.broadcasted_iota(jnp.int32, (N_SIDE, LANES), 1)
    scale = np.float32(B_HEAD_DIM ** -0.5)

    def window(p, part, h):
        c0 = part * B_GROUP_WIDTH + h * B_HEAD_DIM
        cols = slice(c0, c0 + B_HEAD_DIM)
        if n_piece == 1:
            return jnp.concatenate([prev_ref[:, cols], cur_ref[:, cols], next_ref[:, cols]], axis=0)
        if p == 0:
            return jnp.concatenate([prev_ref[:, cols], cur_ref[0:2 * N_SIDE, cols]], axis=0)
        if p == n_piece - 1:
            return jnp.concatenate([cur_ref[bq - 2 * N_SIDE:bq, cols], next_ref[:, cols]], axis=0)
        return cur_ref[(p - 1) * N_SIDE:(p + 2) * N_SIDE, cols]

    def scores(p, h):
        qh = cur_ref[p * N_SIDE:(p + 1) * N_SIDE, h * B_HEAD_DIM:(h + 1) * B_HEAD_DIM]
        s = lax.dot_general(qh, window(p, 1, h), (((1,), (1,)), ((), ())), preferred_element_type=F32) * scale
        if p == 0:
            s = jnp.where(jnp.logical_and(col < N_SIDE, jq == 0), NEG_INF, s)
        if p == n_piece - 1:
            s = jnp.where(jnp.logical_and(col >= 2 * N_SIDE, jq == nq - 1), NEG_INF, s)
        return s + bias_ref[h]

    def softmax(s):
        m = jnp.max(s, axis=-1, keepdims=True)
        e = jnp.exp(s - m)
        den = jnp.sum(e, axis=-1, keepdims=True)
        return (e / den).astype(BF16), m + jnp.log(den)

    def values(p, h, pn):
        o = jnp.dot(pn, window(p, 2, h), preferred_element_type=F32)
        o_ref[p * N_SIDE:(p + 1) * N_SIDE, h * B_HEAD_DIM:(h + 1) * B_HEAD_DIM] = o.astype(BF16)

    lse = {}
    for t in range(B_HEADS + 2):
        if t < B_HEADS:
            for p in range(n_piece):
                s_scr[t][p * N_SIDE:(p + 1) * N_SIDE, :] = scores(p, t)
        if 1 <= t <= B_HEADS:
            for p in range(n_piece):
                rows = slice(p * N_SIDE, (p + 1) * N_SIDE)
                p_scr[t - 1][rows, :], lse[p, t - 1] = softmax(s_scr[t - 1][rows, :])
        if t >= 2:
            for p in range(n_piece):
                values(p, t - 2, p_scr[t - 2][p * N_SIDE:(p + 1) * N_SIDE, :])
    for p in range(n_piece):
        tile = jnp.zeros((N_SIDE, LANES), F32)
        for h in range(B_HEADS):
            tile = jnp.where(lane == h, lse[p, h], tile)
        lse_ref[p * N_SIDE:(p + 1) * N_SIDE, :] = tile


def _attn_bias(group):
    _, dilation = B_PATTERNS[group]
    slopes = np.array([2.0 ** (-8.0 * (group * B_HEADS + h + 1) / N_ATTN_HEADS) for h in range(B_HEADS)],
                      dtype=np.float32)
    a = np.arange(N_SIDE)[:, None]
    c = np.arange(3 * N_SIDE)[None, :]
    rel = c - N_SIDE - a
    alibi = (-(slopes[:, None, None] * (dilation * np.abs(rel))[None])).astype(np.float32)
    return np.where((np.abs(rel) <= N_SIDE)[None], alibi, np.float32(NEG_INF)).astype(np.float32)


def _attention(qkv, group, rows_per_step=1024):
    n_seq, dilation, sub_len, _ = qkv.shape
    bq = min(rows_per_step, sub_len)
    nq = sub_len // bq
    rb = min(dilation, rows_per_step // bq)
    halo_per_bq = bq // N_SIDE
    n_halo = sub_len // N_SIDE
    return pl.pallas_call(
        functools.partial(_attn_kernel, bq=bq),
        grid=(n_seq, dilation // rb, nq),
        in_specs=[pl.BlockSpec((None, rb, bq, QKV_COLS), lambda b, r, j: (b, r, j, 0)),
                  pl.BlockSpec((None, rb, N_SIDE, QKV_COLS),
                               lambda b, r, j: (b, r, jnp.maximum(j * halo_per_bq - 1, 0), 0)),
                  pl.BlockSpec((None, rb, N_SIDE, QKV_COLS),
                               lambda b, r, j: (b, r, jnp.minimum((j + 1) * halo_per_bq, n_halo - 1), 0)),
                  pl.BlockSpec((B_HEADS, N_SIDE, 3 * N_SIDE), lambda b, r, j: (0, 0, 0))],
        out_specs=[pl.BlockSpec((None, rb, bq, B_GROUP_WIDTH), lambda b, r, j: (b, r, j, 0)),
                   pl.BlockSpec((None, rb, bq, LANES), lambda b, r, j: (b, r, j, 0))],
        out_shape=[jax.ShapeDtypeStruct((n_seq, dilation, sub_len, B_GROUP_WIDTH), BF16),
                   jax.ShapeDtypeStruct((n_seq, dilation, sub_len, LANES), F32)],
        scratch_shapes=([pltpu.VMEM((bq, 3 * N_SIDE), F32)] * B_HEADS
                        + [pltpu.VMEM((bq, 3 * N_SIDE), BF16)] * B_HEADS),
        compiler_params=pltpu.CompilerParams(
            dimension_semantics=("parallel", "parallel", "arbitrary"), vmem_limit_bytes=VMEM_LIMIT),
        name=f"attn{group}",
    )(qkv, qkv, qkv, jnp.asarray(_attn_bias(group)))


def _mixer_kernel(u_ref, v_ref, ga_ref, gb_ref, o0_ref, o1_ref, o2_ref, l0_ref, l1_ref, l2_ref, x_ref,
                  lnvg_ref, lnvb_ref, ws_ref, bs_ref, wpa_ref, wpb_ref, wo_ref, ln1g_ref, ln1b_ref,
                  wr_ref, br_ref, x1_ref, x1b_ref, logit_ref, a_scr, b_scr, o_scr, l_scr, t_scr, *, tm, dn_alpha):
    for g, (o_ref, l_ref) in enumerate(((o0_ref, l0_ref), (o1_ref, l1_ref), (o2_ref, l2_ref))):
        dilation = o_ref.shape[0]
        rows = tm // dilation
        d1 = min(dilation, MAX_ROW_STRIDE)
        d2 = dilation // d1

        def pieces(r):
            return [o_ref[r, :, h * B_HEAD_DIM:(h + 1) * B_HEAD_DIM].astype(F32) for h in range(B_HEADS)] + [l_ref[r]]

        def place(idx, vals):
            for h in range(B_HEADS):
                o_scr[g, h, idx, :] = vals[h]
            l_scr[g, idx, :] = vals[B_HEADS]

        for r1 in range(d1):
            idx = pl.ds(r1, rows * d2, stride=d1) if d1 > 1 else slice(None)
            if d2 == 1:
                place(idx, pieces(r1))
                continue
            for r2 in range(d2):
                for j, val in enumerate(pieces(r2 * d1 + r1)):
                    t_scr[r1, j, pl.ds(r2, rows, stride=d2), :] = val
            place(idx, [t_scr[r1, j] for j in range(B_HEADS + 1)])
    n_groups = len(B_PATTERNS)

    def branches(r0, nr):
        rows = slice(r0, r0 + nr)
        vn = _layer_norm(v_ref[rows, :].astype(F32), lnvg_ref[...], lnvb_ref[...]).astype(BF16)
        for g in range(A_GROUPS):
            c0 = g * CHUNK
            chunks = jnp.concatenate([vn[q0:q0 + CHUNK, c0:c0 + CHUNK] for q0 in range(0, nr, CHUNK)], axis=1)
            mixed_all = jnp.dot(ws_ref[g], chunks, preferred_element_type=F32)
            for c in range(nr // CHUNK):
                q0 = c * CHUNK
                mixed = mixed_all[:, q0:q0 + CHUNK] + bs_ref[g]
                a = u_ref[r0 + q0:r0 + q0 + CHUNK, c0:c0 + CHUNK].astype(F32) * mixed
                a_scr[r0 + q0:r0 + q0 + CHUNK, c0:c0 + CHUNK] = a.astype(BF16)
        ls = [l_scr[g, rows, :] for g in range(n_groups)]
        mx = jnp.maximum(jnp.maximum(ls[0], ls[1]), ls[2])
        es = [jnp.exp(l - mx) for l in ls]
        den = es[0] + es[1] + es[2]
        wts = [e / den for e in es]
        for h in range(B_HEADS):
            c0 = h * B_HEAD_DIM
            acc = jnp.zeros((nr, B_HEAD_DIM), F32)
            for g in range(n_groups):
                acc = acc + wts[g][:, h:h + 1] * o_scr[g, h, rows, :]
            b_scr[rows, c0:c0 + B_HEAD_DIM] = acc.astype(BF16)

    def project(r0, nr):
        rows = slice(r0, r0 + nr)
        pa = jnp.dot(a_scr[rows, :], wpa_ref[...], preferred_element_type=F32)
        pb = jnp.dot(b_scr[rows, :], wpb_ref[...], preferred_element_type=F32)
        merged = _sigmoid(ga_ref[rows, :].astype(F32)) * pa + _sigmoid(gb_ref[rows, :].astype(F32)) * pb
        mix = jnp.dot(merged.astype(BF16), wo_ref[...], preferred_element_type=F32)
        x1 = _layer_norm(dn_alpha * x_ref[rows, :] + mix, ln1g_ref[...], ln1b_ref[...])
        x1_ref[rows, :] = x1
        x1b = x1.astype(BF16)
        x1b_ref[rows, :] = x1b
        logit_ref[rows, :] = jnp.dot(x1b, wr_ref[...], preferred_element_type=F32) + br_ref[...]

    half = tm // 2
    branches(0, half)
    project(0, half)
    branches(half, half)
    project(half, half)


def _mixer(z, attn, x, p, layer, dn_alpha, seq, tm=512):
    t = x.shape[0]
    tiles_per_seq = seq // tm

    def zspec(cb):
        return pl.BlockSpec((tm, D_MODEL), lambda i: (i, cb))

    def row(width):
        return pl.BlockSpec((tm, width), lambda i: (i, 0))

    def split(a):
        _, dilation, _, width = a.shape
        return pl.BlockSpec((None, dilation, tm // dilation, width),
                            lambda i: (i // tiles_per_seq, 0, i % tiles_per_seq, 0))

    def vec(width):
        return pl.BlockSpec((None, 1, width), lambda i: (layer, 0, 0))

    def mat(k, n):
        return pl.BlockSpec((None, k, n), lambda i: (layer, 0, 0))

    (o0, l0), (o1, l1), (o2, l2) = attn
    return pl.pallas_call(
        functools.partial(_mixer_kernel, tm=tm, dn_alpha=np.float32(dn_alpha)),
        grid=(t // tm,),
        in_specs=[zspec(0), zspec(1), zspec(2), zspec(3),
                  split(o0), split(o1), split(o2), split(l0), split(l1), split(l2), row(D_MODEL),
                  vec(D_MODEL), vec(D_MODEL),
                  pl.BlockSpec((None, A_GROUPS, CHUNK, CHUNK), lambda i: (layer, 0, 0, 0)),
                  pl.BlockSpec((None, A_GROUPS, CHUNK, CHUNK), lambda i: (layer, 0, 0, 0)),
                  mat(D_MODEL, D_MODEL), mat(B_GROUP_WIDTH, D_MODEL), mat(D_MODEL, D_MODEL),
                  vec(D_MODEL), vec(D_MODEL), mat(D_MODEL, LANES), vec(LANES)],
        out_specs=[row(D_MODEL), row(D_MODEL), row(LANES)],
        out_shape=[jax.ShapeDtypeStruct((t, D_MODEL), F32), jax.ShapeDtypeStruct((t, D_MODEL), BF16),
                   jax.ShapeDtypeStruct((t, LANES), F32)],
        scratch_shapes=[pltpu.VMEM((tm, D_MODEL), BF16), pltpu.VMEM((tm, B_GROUP_WIDTH), BF16),
                        pltpu.VMEM((len(B_PATTERNS), B_HEADS, tm, B_HEAD_DIM), F32),
                        pltpu.VMEM((len(B_PATTERNS), tm, LANES), F32),
                        pltpu.VMEM((MAX_ROW_STRIDE, B_HEADS + 1, tm // MAX_ROW_STRIDE, LANES), F32)],
        compiler_params=pltpu.CompilerParams(
            dimension_semantics=("parallel",), vmem_limit_bytes=VMEM_LIMIT),
        name="mixer",
    )(z, z, z, z, o0, o1, o2, l0, l1, l2, x,
      p["ln_v_g"], p["ln_v_b"], p["w_s"], p["b_s"], p["w_pa"], p["w_pb"], p["w_o"],
      p["ln1_g"], p["ln1_b"], p["w_r"], p["b_r"])


TS = 512
CHUNK_ROWS = 8
REGION = TS * TOP_K + N_EXPERTS * CHUNK_ROWS
P_ROWS = 256


def _chunks_per_expert(cnt):
    return jnp.floor((cnt + (CHUNK_ROWS - 1)) * (1.0 / CHUNK_ROWS))


def _dispatch_kernel(logit_ref, x_ref, upper_ref, lower_ref, xs_ref, pos_ref, gate_ref, cnt_ref):
    l = jnp.transpose(logit_ref[...])[0:N_EXPERTS, :]
    sub = lax.broadcasted_iota(jnp.int32, (N_EXPERTS, TS), 0)
    onehots, vals = [], []
    for _ in range(TOP_K):
        mx = jnp.max(l, axis=0, keepdims=True)
        idx = jnp.min(jnp.where(l == mx, sub, N_EXPERTS), axis=0, keepdims=True)
        oh = sub == idx
        l = jnp.where(oh, -jnp.inf, l)
        onehots.append(oh)
        vals.append(mx)
    es = [jnp.exp(v - vals[0]) for v in vals]
    den = es[0] + es[1] + es[2] + es[3]
    multi = jnp.zeros((N_EXPERTS, TS), F32)
    for oh in onehots:
        multi = multi + oh.astype(F32)
    multi_b = jnp.concatenate([multi, jnp.zeros((LANES - N_EXPERTS, TS), F32)], axis=0).astype(BF16)
    cnt_row = lax.dot_general(jnp.ones((8, TS), BF16), multi_b, (((1,), (1,)), ((), ())),
                              preferred_element_type=F32)
    cnt_ref[...] = cnt_row.astype(jnp.int32)
    cnt_col = jnp.concatenate([jnp.sum(multi, axis=1, keepdims=True),
                               jnp.zeros((LANES - N_EXPERTS, 1), F32)], axis=0)
    n_chunks = jnp.broadcast_to(_chunks_per_expert(cnt_col), (LANES, LANES)).astype(BF16)
    run_off = CHUNK_ROWS * jnp.dot(lower_ref[...], n_chunks, preferred_element_type=F32)[0:N_EXPERTS, 0:1]
    rank = jnp.dot(multi_b, upper_ref[...], preferred_element_type=F32)[0:N_EXPERTS, :]
    slot = rank + run_off
    pos_rows = [jnp.sum(jnp.where(onehots[k], slot, 0.0), axis=0, keepdims=True) for k in range(TOP_K)]
    pad = jnp.full((LANES - TOP_K, TS), -1.0, F32)
    pos_all = jnp.concatenate(pos_rows + [pad], axis=0)
    pos_ref[...] = jnp.transpose(pos_all).astype(jnp.int32)
    gate_ref[...] = jnp.transpose(jnp.concatenate([e / den for e in es] + [pad], axis=0))
    live = pl.program_id(0) < pl.num_programs(0) - 1
    pos_t = jnp.where(live, pos_all[0:8, :], -1.0)
    row = lax.broadcasted_iota(jnp.int32, (P_ROWS, TS), 0).astype(F32).astype(BF16)
    for c in range(REGION // P_ROWS):
        rel = pos_t - float(c * P_ROWS)
        rel = jnp.where(jnp.logical_and(rel >= 0.0, rel < float(P_ROWS)), rel, -1.0).astype(BF16)
        perm = jnp.zeros((P_ROWS, TS), BF16)
        for k in range(TOP_K):
            perm = jnp.where(row == rel[k:k + 1, :], jnp.ones((), BF16), perm)
        xs_ref[c * P_ROWS:(c + 1) * P_ROWS, :] = jnp.dot(
            perm, x_ref[...], preferred_element_type=F32).astype(BF16)


def _dispatch(logits, x1b):
    t = logits.shape[0]
    n_tiles = t // TS
    upper = jnp.asarray(np.triu(np.ones((TS, TS), np.float32), 1), dtype=BF16)
    lower = jnp.asarray(np.tril(np.ones((LANES, LANES), np.float32), -1), dtype=BF16)

    def tile(i):
        return jnp.minimum(i, n_tiles - 1)

    return pl.pallas_call(
        _dispatch_kernel,
        grid=(n_tiles + 1,),
        in_specs=[pl.BlockSpec((TS, LANES), lambda i: (tile(i), 0)),
                  pl.BlockSpec((TS, D_MODEL), lambda i: (tile(i), 0)),
                  pl.BlockSpec((TS, TS), lambda i: (0, 0)),
                  pl.BlockSpec((LANES, LANES), lambda i: (0, 0))],
        out_specs=[pl.BlockSpec((REGION, D_MODEL), lambda i: (i, 0)),
                   pl.BlockSpec((TS, LANES), lambda i: (tile(i), 0)),
                   pl.BlockSpec((TS, LANES), lambda i: (tile(i), 0)),
                   pl.BlockSpec((None, 8, LANES), lambda i: (tile(i), 0, 0))],
        out_shape=[jax.ShapeDtypeStruct(((n_tiles + 1) * REGION, D_MODEL), BF16),
                   jax.ShapeDtypeStruct((t, LANES), jnp.int32), jax.ShapeDtypeStruct((t, LANES), F32),
                   jax.ShapeDtypeStruct((n_tiles, 8, LANES), jnp.int32)],
        compiler_params=pltpu.CompilerParams(
            dimension_semantics=("arbitrary",), vmem_limit_bytes=VMEM_LIMIT),
        name="dispatch",
    )(logits, x1b, upper, lower)


def _chunk_plan(cnt, bm):
    assert REGION >= 3 * bm
    n_tiles = cnt.shape[0]
    ch = bm // CHUNK_ROWS
    n_rows_max = n_tiles * REGION + N_EXPERTS * (bm - CHUNK_ROWS)
    n_blk = -(-n_rows_max // bm)
    run_chunks = (cnt + CHUNK_ROWS - 1) // CHUNK_ROWS
    run_off = (jnp.cumsum(run_chunks, axis=1) - run_chunks) * CHUNK_ROWS
    rows_e = jnp.sum(run_chunks, axis=0) * CHUNK_ROWS
    padded_e = (rows_e + bm - 1) // bm * bm
    pend = jnp.cumsum(padded_e)
    pstart = pend - padded_e
    runs_t = run_chunks.T
    run_start = pstart[:, None] // CHUNK_ROWS + jnp.cumsum(runs_t, axis=1) - runs_t
    run_row = jnp.arange(n_tiles, dtype=jnp.int32)[None, :] * REGION + run_off.T
    blk = jnp.arange(n_blk + 1, dtype=jnp.int32)
    blk_e = jnp.minimum(jnp.sum((pend[None, :] <= blk[:, None] * bm).astype(jnp.int32), axis=1), N_EXPERTS - 1)
    is_e = blk_e[:, None, None] == jnp.arange(N_EXPERTS, dtype=jnp.int32)[None, :, None]
    c = (blk[:, None] * ch + jnp.arange(ch, dtype=jnp.int32)[None, :]).reshape(-1)

    def of_run(v):
        v_b = jnp.sum(jnp.where(is_e, v[None], 0), axis=1)
        step = v_b - jnp.concatenate([jnp.zeros_like(v_b[:, :1]), v_b[:, :-1]], axis=1)
        return jnp.sum(jnp.where(started, step[:, None, :], 0), axis=-1).reshape(-1)

    start_b = jnp.sum(jnp.where(is_e, run_start[None], 0), axis=1)
    started = start_b[:, None, :] <= c.reshape(n_blk + 1, ch)[:, :, None]
    j = c - of_run(run_start)
    valid = j < of_run(runs_t)
    row = of_run(run_row) + j * CHUNK_ROWS
    spare = n_tiles * REGION + (c % ch) * CHUNK_ROWS
    src = jnp.where(valid, row, spare + 2 * bm).astype(jnp.int32).reshape(n_blk + 1, 1, ch)
    dst = jnp.where(valid, row, spare + ((c // ch) % 2) * bm).astype(jnp.int32).reshape(n_blk + 1, 1, ch)
    dst = jnp.concatenate([dst[n_blk:], dst[:n_blk]])
    blk_e = blk_e[:n_blk].astype(jnp.int32)
    n_used = (pend[-1:] // bm).astype(jnp.int32)
    plan = jnp.concatenate([src[1:], dst[:n_blk], dst[1:]], axis=-1)
    return src[:1], plan, blk_e, n_used


def _ffn_kernel(blk_e_ref, n_used_ref, src0_ref, plan_ref, xs_hbm, wgu_ref, bgu_ref, wd_ref,
                bd_ref, ys_hbm, xbuf, ybuf, wgu_b, wd_b, in_sem, out_sem, *, bm):
    i = pl.program_id(0)
    n_used = n_used_ref[0]
    slot = i % 2
    ch = bm // CHUNK_ROWS

    def gather(tbl_ref, col0, s):
        for j in range(ch):
            row = pl.multiple_of(tbl_ref[0, col0 + j], CHUNK_ROWS)
            pltpu.make_async_copy(xs_hbm.at[pl.ds(row, CHUNK_ROWS)],
                                  xbuf.at[s, pl.ds(j * CHUNK_ROWS, CHUNK_ROWS)], in_sem.at[s]).start()

    def scatter(tbl_ref, col0, s):
        for j in range(ch):
            row = pl.multiple_of(tbl_ref[0, col0 + j], CHUNK_ROWS)
            pltpu.make_async_copy(ybuf.at[s, pl.ds(j * CHUNK_ROWS, CHUNK_ROWS)],
                                  ys_hbm.at[pl.ds(row, CHUNK_ROWS)], out_sem.at[s]).start()

    def wait_in(s):
        pltpu.make_async_copy(xs_hbm.at[pl.ds(0, bm)], xbuf.at[s], in_sem.at[s]).wait()

    def wait_out(s):
        pltpu.make_async_copy(ybuf.at[s], ys_hbm.at[pl.ds(0, bm)], out_sem.at[s]).wait()

    used = i < n_used
    e_changed = jnp.logical_or(i == 0, blk_e_ref[i] != blk_e_ref[jnp.maximum(i - 1, 0)])

    @pl.when(jnp.logical_and(i == 0, used))
    def _():
        gather(src0_ref, 0, 0)
        ybuf[1] = jnp.zeros((bm, D_MODEL), BF16)

    @pl.when(jnp.logical_and(used, e_changed))
    def _():
        wgu_b[...] = wgu_ref[...].astype(BF16)
        wd_b[...] = wd_ref[...].astype(BF16)

    @pl.when(jnp.logical_and(used, i >= 1))
    def _():
        wait_out(slot)

    @pl.when(used)
    def _():
        wait_in(slot)
        gu = jnp.dot(xbuf[slot], wgu_b[...], preferred_element_type=F32) + bgu_ref[...]
        gather(plan_ref, 0, 1 - slot)
        scatter(plan_ref, ch, 1 - slot)
        gate = jnp.minimum(gu[:, :D_MODEL], SWIGLU_LIMIT)
        up = jnp.clip(gu[:, D_MODEL:], -SWIGLU_LIMIT, SWIGLU_LIMIT)
        h = (up + 1.0) * gate * _sigmoid(SWIGLU_ALPHA * gate)
        y = jnp.dot(h.astype(BF16), wd_b[...], preferred_element_type=F32) + bd_ref[...]
        ybuf[slot] = y.astype(BF16)

    @pl.when(i == n_used - 1)
    def _():
        scatter(plan_ref, 2 * ch, slot)
        wait_in(1 - slot)
        wait_out(1 - slot)
        wait_out(slot)


def _ffn(xs, src0, plan, blk_e, n_used, p, layer, bm):
    n_blk = blk_e.shape[0]
    ch = src0.shape[-1]
    smem = functools.partial(pl.BlockSpec, memory_space=pltpu.SMEM)
    grid_spec = pltpu.PrefetchScalarGridSpec(
        num_scalar_prefetch=2,
        grid=(n_blk,),
        in_specs=[smem((None, 1, ch), lambda i, be, nu: (0, 0, 0)),
                  smem((None, 1, 3 * ch), lambda i, be, nu: (i, 0, 0)),
                  pl.BlockSpec(memory_space=pl.ANY),
                  pl.BlockSpec((None, None, D_MODEL, 2 * D_MODEL), lambda i, be, nu: (layer, be[i], 0, 0)),
                  pl.BlockSpec((None, None, 1, 2 * D_MODEL), lambda i, be, nu: (layer, be[i], 0, 0)),
                  pl.BlockSpec((None, None, D_MODEL, D_MODEL), lambda i, be, nu: (layer, be[i], 0, 0)),
                  pl.BlockSpec((None, None, 1, D_MODEL), lambda i, be, nu: (layer, be[i], 0, 0))],
        out_specs=pl.BlockSpec(memory_space=pl.ANY),
        scratch_shapes=[pltpu.VMEM((2, bm, D_MODEL), BF16), pltpu.VMEM((2, bm, D_MODEL), BF16),
                        pltpu.VMEM((D_MODEL, 2 * D_MODEL), BF16), pltpu.VMEM((D_MODEL, D_MODEL), BF16),
                        pltpu.SemaphoreType.DMA((2,)), pltpu.SemaphoreType.DMA((2,))],
    )
    return pl.pallas_call(
        functools.partial(_ffn_kernel, bm=bm),
        grid_spec=grid_spec,
        out_shape=jax.ShapeDtypeStruct(xs.shape, BF16),
        input_output_aliases={4: 0},
        compiler_params=pltpu.CompilerParams(
            dimension_semantics=("arbitrary",), vmem_limit_bytes=VMEM_LIMIT),
        name="ffn",
    )(blk_e, n_used, src0, plan, xs, p["w_gu"], p["b_gu"], p["w_down"], p["b_down"])


G_ROWS = 128


def _combine_kernel(ys_ref, pos_ref, gate_ref, x_ref, g_ref, b_ref, out0_ref, out1_ref, g_scr, *x_scr,
                    dn_alpha, first_tiles):
    nr = G_ROWS
    lane = lax.broadcasted_iota(jnp.int32, (nr, LANES), 1)
    for r0 in range(0, TS, nr):
        rows = slice(r0, r0 + nr)
        tile_of, gate = [], []
        for k in range(TOP_K):
            rel = jnp.broadcast_to(pos_ref[rows, k:k + 1], (nr, LANES)) - lane
            hit = jnp.where((rel & (LANES - 1)) == 0, rel >> (LANES.bit_length() - 1), -1)
            tile_of.append(hit.astype(F32).astype(BF16))
            gate.append(jnp.broadcast_to(gate_ref[rows, k:k + 1], (nr, LANES)).astype(BF16))
        for c in range(REGION // LANES):
            g = jnp.zeros((nr, LANES), BF16)
            for k in range(TOP_K):
                g = jnp.where(tile_of[k] == float(c), gate[k], g)
            g_scr[rows, c * LANES:(c + 1) * LANES] = g
        y = jnp.dot(g_scr[rows, :], ys_ref[...], preferred_element_type=F32)
        x2 = _layer_norm(dn_alpha * x_ref[rows, :] + y, g_ref[...], b_ref[...])
        if first_tiles is None:
            out0_ref[rows, :] = x2
            out1_ref[rows, :] = x2.astype(BF16)
        else:
            x_scr[0][rows, :] = x2
    if first_tiles is not None:
        @pl.when(pl.program_id(0) < first_tiles)
        def _():
            out0_ref[...] = x_scr[0][...]

        @pl.when(pl.program_id(0) >= first_tiles)
        def _():
            out1_ref[...] = x_scr[0][...]


def _combine(ys, pos, gates, x1, p, layer, dn_alpha, first_tokens=None):
    t = x1.shape[0]
    if first_tokens is None:
        first_tiles = None
        out_specs = [pl.BlockSpec((TS, D_MODEL), lambda i: (i, 0)), pl.BlockSpec((TS, D_MODEL), lambda i: (i, 0))]
        out_shape = [jax.ShapeDtypeStruct((t, D_MODEL), F32), jax.ShapeDtypeStruct((t, D_MODEL), BF16)]
        scratch = []
    else:
        first_tiles = first_tokens // TS
        out_specs = [pl.BlockSpec((TS, D_MODEL), lambda i: (jnp.minimum(i, first_tiles - 1), 0)),
                     pl.BlockSpec((TS, D_MODEL), lambda i: (jnp.maximum(i - first_tiles, 0), 0))]
        out_shape = [jax.ShapeDtypeStruct((first_tokens, D_MODEL), F32),
                     jax.ShapeDtypeStruct((t - first_tokens, D_MODEL), F32)]
        scratch = [pltpu.VMEM((TS, D_MODEL), F32)]
    return pl.pallas_call(
        functools.partial(_combine_kernel, dn_alpha=np.float32(dn_alpha), first_tiles=first_tiles),
        grid=(t // TS,),
        in_specs=[pl.BlockSpec((REGION, D_MODEL), lambda i: (i, 0)),
                  pl.BlockSpec((TS, LANES), lambda i: (i, 0)),
                  pl.BlockSpec((TS, LANES), lambda i: (i, 0)),
                  pl.BlockSpec((TS, D_MODEL), lambda i: (i, 0)),
                  pl.BlockSpec((None, 1, D_MODEL), lambda i: (layer, 0, 0)),
                  pl.BlockSpec((None, 1, D_MODEL), lambda i: (layer, 0, 0))],
        out_specs=out_specs,
        out_shape=out_shape,
        scratch_shapes=[pltpu.VMEM((TS, REGION), BF16)] + scratch,
        compiler_params=pltpu.CompilerParams(
            dimension_semantics=("arbitrary",), vmem_limit_bytes=VMEM_LIMIT),
        name="combine",
    )(ys, pos, gates, x1, p["ln2_g"], p["ln2_b"])


def _prepare_params(w_in, b_in, ln_v_g, ln_v_b, w_s, b_s, w_pa, w_pb, w_o, ln1_g, ln1_b,
                    w_r, b_r, w_gu, b_gu, w_down, b_down, ln2_g, ln2_b):
    depth = w_in.shape[0]
    n_qkv = 3 * len(B_PATTERNS) * B_GROUP_WIDTH
    qkv0, qkv1 = 2 * D_MODEL, 2 * D_MODEL + n_qkv

    def permute_cols(a):
        gap = jnp.zeros(a.shape[:-1] + (MAIN_COL0 - n_qkv,), a.dtype)
        return jnp.concatenate([a[..., qkv0:qkv1], gap, a[..., :qkv0], a[..., qkv1:]], axis=-1)

    def vec(a):
        return a.reshape(depth, 1, a.shape[-1]).astype(F32)

    w_r_pad = jnp.pad(w_r, ((0, 0), (0, 0), (0, LANES - N_EXPERTS))).astype(BF16)
    b_r_pad = jnp.pad(b_r.astype(F32), ((0, 0), (0, LANES - N_EXPERTS)), constant_values=NEG_INF)
    return dict(
        w_in=permute_cols(w_in).astype(BF16), b_in=vec(permute_cols(b_in)),
        ln_v_g=vec(ln_v_g), ln_v_b=vec(ln_v_b),
        w_s=w_s.astype(BF16),
        b_s=jnp.broadcast_to(b_s.astype(F32)[..., None], b_s.shape + (CHUNK,)),
        w_pa=w_pa.astype(BF16), w_pb=w_pb.astype(BF16), w_o=w_o.astype(BF16),
        ln1_g=vec(ln1_g), ln1_b=vec(ln1_b),
        w_r=w_r_pad, b_r=vec(b_r_pad),
        w_gu=w_gu, b_gu=b_gu.astype(F32)[:, :, None, :],
        w_down=w_down, b_down=b_down.astype(F32)[:, :, None, :],
        ln2_g=vec(ln2_g), ln2_b=vec(ln2_b),
    )


def _trunk(x, params, n_first, bm=512):
    n_seq, seq, _ = x.shape
    t = n_seq * seq
    depth = params["w_in"].shape[0]
    dn_alpha = (2.0 * depth) ** 0.25
    x = x.reshape(t, D_MODEL)
    xb = x.astype(BF16)
    for layer in range(depth):
        z = _inproj_main(xb, params["w_in"], params["b_in"], layer)
        attn = [_attention(_inproj_qkv(xb, params["w_in"], params["b_in"], layer, g, n_seq, seq), g)
                for g in range(len(B_PATTERNS))]
        x1, x1b, logits = _mixer(z, attn, x, params, layer, dn_alpha, seq)
        xs, pos, gates, cnt = _dispatch(logits, x1b)
        src0, plan, blk_e, n_used = _chunk_plan(cnt[:, 0, :N_EXPERTS], bm)
        ys = _ffn(xs, src0, plan, blk_e, n_used, params, layer, bm)
        last = layer == depth - 1
        x, xb = _combine(ys, pos, gates, x1, params, layer, dn_alpha, n_first * seq if last else None)
    return x.reshape(n_first, seq, D_MODEL), xb.reshape(n_seq - n_first, seq, D_MODEL)


def kernel(x_prompt, x_sample, w_in, b_in, ln_v_g, ln_v_b, w_s, b_s, w_pa, w_pb, w_o, ln1_g, ln1_b,
           w_r, b_r, w_gu, b_gu, w_down, b_down, ln2_g, ln2_b):
    params = _prepare_params(w_in, b_in, ln_v_g, ln_v_b, w_s, b_s, w_pa, w_pb, w_o, ln1_g, ln1_b,
                             w_r, b_r, w_gu, b_gu, w_down, b_down, ln2_g, ln2_b)
    return _trunk(jnp.concatenate([x_prompt, x_sample], axis=0), params, x_prompt.shape[0])
```

```python
import functools

import numpy as np
import jax
import jax.numpy as jnp
from jax import lax
from jax.experimental import pallas as pl
from jax.experimental.pallas import tpu as pltpu

D_MODEL = 1024
CHUNK = 128
A_GROUPS = 8
B_PATTERNS = ((128, 1), (512, 4), (2048, 16))
B_HEADS = 4
B_HEAD_DIM = 128
B_GROUP_WIDTH = B_HEADS * B_HEAD_DIM
N_ATTN_HEADS = len(B_PATTERNS) * B_HEADS
N_SIDE = 64
D_IN = 4 * D_MODEL + 3 * len(B_PATTERNS) * B_GROUP_WIDTH
N_EXPERTS = 32
TOP_K = 4
SWIGLU_LIMIT = 7.0
SWIGLU_ALPHA = 1.702
LN_EPS = 1e-5
NEG_INF = -1e30
LANES = 128

MAIN_COLS = 4 * D_MODEL
QKV_COLS = 3 * B_GROUP_WIDTH
MAIN_COL0 = -(-len(B_PATTERNS) * QKV_COLS // D_MODEL) * D_MODEL
GELU_COLS = 2 * D_MODEL

VMEM_LIMIT = 56 * 1024 * 1024

BF16 = jnp.bfloat16
F32 = jnp.float32


def _gelu(x):
    return 0.5 * x * (1.0 + lax.erf(x * np.float32(np.sqrt(0.5))))


def _sigmoid(x):
    return 0.5 * jnp.tanh(0.5 * x) + 0.5


def _layer_norm(x, g, b):
    xc = x - jnp.mean(x, axis=-1, keepdims=True)
    var = jnp.mean(xc * xc, axis=-1, keepdims=True)
    return xc * lax.rsqrt(var + LN_EPS) * g + b


N_CHUNK = 256
MAX_ROW_STRIDE = 4


def _inproj_main_kernel(x_ref, w_ref, b_ref, z_ref, *, n_gelu_tiles):
    j = pl.program_id(1)

    def run(act):
        for c in range(z_ref.shape[1] // N_CHUNK):
            cols = slice(c * N_CHUNK, (c + 1) * N_CHUNK)
            acc = jnp.dot(x_ref[...], w_ref[:, cols], preferred_element_type=F32) + b_ref[:, cols]
            z_ref[:, cols] = act(acc).astype(BF16)

    @pl.when(j < n_gelu_tiles)
    def _():
        run(_gelu)

    @pl.when(j >= n_gelu_tiles)
    def _():
        run(lambda a: a)


def _inproj_main(xb, w_in, b_in, layer, tm=2048, tn=1024):
    t = xb.shape[0]
    return pl.pallas_call(
        functools.partial(_inproj_main_kernel, n_gelu_tiles=GELU_COLS // tn),
        grid=(t // tm, MAIN_COLS // tn),
        in_specs=[
            pl.BlockSpec((tm, D_MODEL), lambda i, j: (i, 0)),
            pl.BlockSpec((None, D_MODEL, tn), lambda i, j: (layer, 0, MAIN_COL0 // tn + j)),
            pl.BlockSpec((None, 1, tn), lambda i, j: (layer, 0, MAIN_COL0 // tn + j)),
        ],
        out_specs=pl.BlockSpec((tm, tn), lambda i, j: (i, j)),
        out_shape=jax.ShapeDtypeStruct((t, MAIN_COLS), BF16),
        compiler_params=pltpu.CompilerParams(
            dimension_semantics=("parallel", "arbitrary"), vmem_limit_bytes=VMEM_LIMIT),
        name="inproj_main",
    )(xb, w_in, b_in)


def _inproj_qkv_kernel(x_ref, w_ref, b_ref, o_ref, *scratch, dilation):
    rows = x_ref.shape[0] // dilation
    for c in range(o_ref.shape[-1] // N_CHUNK):
        cols = slice(c * N_CHUNK, (c + 1) * N_CHUNK)
        acc = jnp.dot(x_ref[...], w_ref[:, cols], preferred_element_type=F32) + b_ref[:, cols]
        if dilation == 1:
            o_ref[0, :, cols] = acc.astype(BF16)
            continue
        scr, scr2 = scratch[0], scratch[-1]
        d1 = min(dilation, MAX_ROW_STRIDE)
        d2 = dilation // d1
        for q in range(c * N_CHUNK // LANES, (c + 1) * N_CHUNK // LANES):
            scr[q] = acc[:, q * LANES - c * N_CHUNK:(q + 1) * LANES - c * N_CHUNK]
            for r1 in range(d1):
                part = scr[q, pl.ds(r1, rows * d2, stride=d1), :]
                if d2 == 1:
                    o_ref[r1, :, q * LANES:(q + 1) * LANES] = part.astype(BF16)
                    continue
                scr2[q, r1 * rows * d2:(r1 + 1) * rows * d2, :] = part
                for r2 in range(d2):
                    o_ref[r2 * d1 + r1, :, q * LANES:(q + 1) * LANES] = (
                        scr2[q, pl.ds(r1 * rows * d2 + r2, rows, stride=d2), :].astype(BF16))


def _inproj_qkv(xb, w_in, b_in, layer, group, n_seq, seq, tn=QKV_COLS):
    _, dilation = B_PATTERNS[group]
    tm = 2048 if dilation <= MAX_ROW_STRIDE else 1024
    tiles_per_seq = seq // tm
    col0 = group * QKV_COLS // tn
    return pl.pallas_call(
        functools.partial(_inproj_qkv_kernel, dilation=dilation),
        grid=(n_seq * tiles_per_seq, QKV_COLS // tn),
        in_specs=[
            pl.BlockSpec((tm, D_MODEL), lambda i, j: (i, 0)),
            pl.BlockSpec((None, D_MODEL, tn), lambda i, j: (layer, 0, col0 + j)),
            pl.BlockSpec((None, 1, tn), lambda i, j: (layer, 0, col0 + j)),
        ],
        out_specs=pl.BlockSpec((None, dilation, tm // dilation, tn),
                               lambda i, j: (i // tiles_per_seq, 0, i % tiles_per_seq, j)),
        out_shape=jax.ShapeDtypeStruct((n_seq, dilation, seq // dilation, QKV_COLS), BF16),
        scratch_shapes=[pltpu.VMEM((tn // LANES, tm, LANES), F32)] * (
            0 if dilation == 1 else 1 if dilation <= MAX_ROW_STRIDE else 2),
        compiler_params=pltpu.CompilerParams(
            dimension_semantics=("parallel", "arbitrary"), vmem_limit_bytes=VMEM_LIMIT),
        name=f"inproj_qkv{group}",
    )(xb, w_in, b_in)


def _attn_kernel(cur_ref, prev_ref, next_ref, bias_ref, o_ref, lse_ref, *scratch, bq):
    for rb in range(cur_ref.shape[0]):
        _attn_rows(cur_ref.at[rb], prev_ref.at[rb], next_ref.at[rb], bias_ref, o_ref.at[rb], lse_ref.at[rb],
                   scratch, bq)


def _attn_rows(cur_ref, prev_ref, next_ref, bias_ref, o_ref, lse_ref, scratch, bq):
    s_scr, p_scr = scratch[:B_HEADS], scratch[B_HEADS:]
    jq = pl.program_id(2)
    nq = pl.num_programs(2)
    n_piece = bq // N_SIDE
    col = lax.broadcasted_iota(jnp.int32, (N_SIDE, 3 * N_SIDE), 1)
    lane = lax.broadcasted_iota(jnp.int32, (N_SIDE, LANES), 1)
    scale = np.float32(B_HEAD_DIM ** -0.5)

    def window(p, part, h):
        c0 = part * B_GROUP_WIDTH + h * B_HEAD_DIM
        cols = slice(c0, c0 + B_HEAD_DIM)
        if n_piece == 1:
            return jnp.concatenate([prev_ref[:, cols], cur_ref[:, cols], next_ref[:, cols]], axis=0)
        if p == 0:
            return jnp.concatenate([prev_ref[:, cols], cur_ref[0:2 * N_SIDE, cols]], axis=0)
        if p == n_piece - 1:
            return jnp.concatenate([cur_ref[bq - 2 * N_SIDE:bq, cols], next_ref[:, cols]], axis=0)
        return cur_ref[(p - 1) * N_SIDE:(p + 2) * N_SIDE, cols]

    def scores(p, h):
        qh = cur_ref[p * N_SIDE:(p + 1) * N_SIDE, h * B_HEAD_DIM:(h + 1) * B_HEAD_DIM]
        s = lax.dot_general(qh, window(p, 1, h), (((1,), (1,)), ((), ())), preferred_element_type=F32) * scale
        if p == 0:
            s = jnp.where(jnp.logical_and(col < N_SIDE, jq == 0), NEG_INF, s)
        if p == n_piece - 1:
            s = jnp.where(jnp.logical_and(col >= 2 * N_SIDE, jq == nq - 1), NEG_INF, s)
        return s + bias_ref[h]

    def softmax(s):
        m = jnp.max(s, axis=-1, keepdims=True)
        e = jnp.exp(s - m)
        den = jnp.sum(e, axis=-1, keepdims=True)
        return (e / den).astype(BF16), m + jnp.log(den)

    def values(p, h, pn):
        o = jnp.dot(pn, window(p, 2, h), preferred_element_type=F32)
        o_ref[p * N_SIDE:(p + 1) * N_SIDE, h * B_HEAD_DIM:(h + 1) * B_HEAD_DIM] = o.astype(BF16)

    lse = {}
    for t in range(B_HEADS + 2):
        if t < B_HEADS:
            for p in range(n_piece):
                s_scr[t][p * N_SIDE:(p + 1) * N_SIDE, :] = scores(p, t)
        if 1 <= t <= B_HEADS:
            for p in range(n_piece):
                rows = slice(p * N_SIDE, (p + 1) * N_SIDE)
                p_scr[t - 1][rows, :], lse[p, t - 1] = softmax(s_scr[t - 1][rows, :])
        if t >= 2:
            for p in range(n_piece):
                values(p, t - 2, p_scr[t - 2][p * N_SIDE:(p + 1) * N_SIDE, :])
    for p in range(n_piece):
        tile = jnp.zeros((N_SIDE, LANES), F32)
        for h in range(B_HEADS):
            tile = jnp.where(lane == h, lse[p, h], tile)
        lse_ref[p * N_SIDE:(p + 1) * N_SIDE, :] = tile


def _attn_bias(group):
    _, dilation = B_PATTERNS[group]
    slopes = np.array([2.0 ** (-8.0 * (group * B_HEADS + h + 1) / N_ATTN_HEADS) for h in range(B_HEADS)],
                      dtype=np.float32)
    a = np.arange(N_SIDE)[:, None]
    c = np.arange(3 * N_SIDE)[None, :]
    rel = c - N_SIDE - a
    alibi = (-(slopes[:, None, None] * (dilation * np.abs(rel))[None])).astype(np.float32)
    return np.where((np.abs(rel) <= N_SIDE)[None], alibi, np.float32(NEG_INF)).astype(np.float32)


def _attention(qkv, group, rows_per_step=1024):
    n_seq, dilation, sub_len, _ = qkv.shape
    bq = min(rows_per_step, sub_len)
    nq = sub_len // bq
    rb = min(dilation, rows_per_step // bq)
    halo_per_bq = bq // N_SIDE
    n_halo = sub_len // N_SIDE
    return pl.pallas_call(
        functools.partial(_attn_kernel, bq=bq),
        grid=(n_seq, dilation // rb, nq),
        in_specs=[pl.BlockSpec((None, rb, bq, QKV_COLS), lambda b, r, j: (b, r, j, 0)),
                  pl.BlockSpec((None, rb, N_SIDE, QKV_COLS),
                               lambda b, r, j: (b, r, jnp.maximum(j * halo_per_bq - 1, 0), 0)),
                  pl.BlockSpec((None, rb, N_SIDE, QKV_COLS),
                               lambda b, r, j: (b, r, jnp.minimum((j + 1) * halo_per_bq, n_halo - 1), 0)),
                  pl.BlockSpec((B_HEADS, N_SIDE, 3 * N_SIDE), lambda b, r, j: (0, 0, 0))],
        out_specs=[pl.BlockSpec((None, rb, bq, B_GROUP_WIDTH), lambda b, r, j: (b, r, j, 0)),
                   pl.BlockSpec((None, rb, bq, LANES), lambda b, r, j: (b, r, j, 0))],
        out_shape=[jax.ShapeDtypeStruct((n_seq, dilation, sub_len, B_GROUP_WIDTH), BF16),
                   jax.ShapeDtypeStruct((n_seq, dilation, sub_len, LANES), F32)],
        scratch_shapes=([pltpu.VMEM((bq, 3 * N_SIDE), F32)] * B_HEADS
                        + [pltpu.VMEM((bq, 3 * N_SIDE), BF16)] * B_HEADS),
        compiler_params=pltpu.CompilerParams(
            dimension_semantics=("parallel", "parallel", "arbitrary"), vmem_limit_bytes=VMEM_LIMIT),
        name=f"attn{group}",
    )(qkv, qkv, qkv, jnp.asarray(_attn_bias(group)))


def _mixer_kernel(u_ref, v_ref, ga_ref, gb_ref, o0_ref, o1_ref, o2_ref, l0_ref, l1_ref, l2_ref, x_ref, xrest_ref,
                  lnvg_ref, lnvb_ref, ws_ref, bs_ref, wpa_ref, wpb_ref, wo_ref, ln1g_ref, ln1b_ref,
                  wr_ref, br_ref, x1_ref, x1b_ref, logit_ref, a_scr, b_scr, o_scr, l_scr, t_scr,
                  *, tm, dn_alpha, first_tiles):
    for g, (o_ref, l_ref) in enumerate(((o0_ref, l0_ref), (o1_ref, l1_ref), (o2_ref, l2_ref))):
        dilation = o_ref.shape[0]
        rows = tm // dilation
        d1 = min(dilation, MAX_ROW_STRIDE)
        d2 = dilation // d1

        def pieces(r):
            return [o_ref[r, :, h * B_HEAD_DIM:(h + 1) * B_HEAD_DIM].astype(F32) for h in range(B_HEADS)] + [l_ref[r]]

        def place(idx, vals):
            for h in range(B_HEADS):
                o_scr[g, h, idx, :] = vals[h]
            l_scr[g, idx, :] = vals[B_HEADS]

        for r1 in range(d1):
            idx = pl.ds(r1, rows * d2, stride=d1) if d1 > 1 else slice(None)
            if d2 == 1:
                place(idx, pieces(r1))
                continue
            for r2 in range(d2):
                for j, val in enumerate(pieces(r2 * d1 + r1)):
                    t_scr[r1, j, pl.ds(r2, rows, stride=d2), :] = val
            place(idx, [t_scr[r1, j] for j in range(B_HEADS + 1)])
    n_groups = len(B_PATTERNS)

    def branches(r0, nr):
        rows = slice(r0, r0 + nr)
        vn = _layer_norm(v_ref[rows, :].astype(F32), lnvg_ref[...], lnvb_ref[...]).astype(BF16)
        for g in range(A_GROUPS):
            c0 = g * CHUNK
            chunks = jnp.concatenate([vn[q0:q0 + CHUNK, c0:c0 + CHUNK] for q0 in range(0, nr, CHUNK)], axis=1)
            mixed_all = jnp.dot(ws_ref[g], chunks, preferred_element_type=F32)
            for c in range(nr // CHUNK):
                q0 = c * CHUNK
                mixed = mixed_all[:, q0:q0 + CHUNK] + bs_ref[g]
                a = u_ref[r0 + q0:r0 + q0 + CHUNK, c0:c0 + CHUNK].astype(F32) * mixed
                a_scr[r0 + q0:r0 + q0 + CHUNK, c0:c0 + CHUNK] = a.astype(BF16)
        ls = [l_scr[g, rows, :] for g in range(n_groups)]
        mx = jnp.maximum(jnp.maximum(ls[0], ls[1]), ls[2])
        es = [jnp.exp(l - mx) for l in ls]
        den = es[0] + es[1] + es[2]
        wts = [e / den for e in es]
        for h in range(B_HEADS):
            c0 = h * B_HEAD_DIM
            acc = jnp.zeros((nr, B_HEAD_DIM), F32)
            for g in range(n_groups):
                acc = acc + wts[g][:, h:h + 1] * o_scr[g, h, rows, :]
            b_scr[rows, c0:c0 + B_HEAD_DIM] = acc.astype(BF16)

    def project(r0, nr):
        rows = slice(r0, r0 + nr)
        pa = jnp.dot(a_scr[rows, :], wpa_ref[...], preferred_element_type=F32)
        pb = jnp.dot(b_scr[rows, :], wpb_ref[...], preferred_element_type=F32)
        merged = _sigmoid(ga_ref[rows, :].astype(F32)) * pa + _sigmoid(gb_ref[rows, :].astype(F32)) * pb
        mix = jnp.dot(merged.astype(BF16), wo_ref[...], preferred_element_type=F32)
        x = x_ref[rows, :]
        if first_tiles is not None:
            x = jnp.where(pl.program_id(0) < first_tiles, x, xrest_ref[rows, :])
        x1 = _layer_norm(dn_alpha * x + mix, ln1g_ref[...], ln1b_ref[...])
        x1_ref[rows, :] = x1
        x1b = x1.astype(BF16)
        x1b_ref[rows, :] = x1b
        logit_ref[rows, :] = jnp.dot(x1b, wr_ref[...], preferred_element_type=F32) + br_ref[...]

    half = tm // 2
    branches(0, half)
    project(0, half)
    branches(half, half)
    project(half, half)


def _mixer(z, attn, x, p, layer, dn_alpha, seq, tm=512):
    xa, xb = x if isinstance(x, tuple) else (x, x)
    first_tiles = xa.shape[0] // tm
    t = z.shape[0]
    tiles_per_seq = seq // tm
    if isinstance(x, tuple):
        x_specs = [pl.BlockSpec((tm, D_MODEL), lambda i: (jnp.minimum(i, first_tiles - 1), 0)),
                   pl.BlockSpec((tm, D_MODEL), lambda i: (jnp.maximum(i - first_tiles, 0), 0))]
    else:
        x_specs = [pl.BlockSpec((tm, D_MODEL), lambda i: (i, 0)), pl.BlockSpec((tm, D_MODEL), lambda i: (0, 0))]

    def zspec(cb):
        return pl.BlockSpec((tm, D_MODEL), lambda i: (i, cb))

    def row(width):
        return pl.BlockSpec((tm, width), lambda i: (i, 0))

    def split(a):
        _, dilation, _, width = a.shape
        return pl.BlockSpec((None, dilation, tm // dilation, width),
                            lambda i: (i // tiles_per_seq, 0, i % tiles_per_seq, 0))

    def vec(width):
        return pl.BlockSpec((None, 1, width), lambda i: (layer, 0, 0))

    def mat(k, n):
        return pl.BlockSpec((None, k, n), lambda i: (layer, 0, 0))

    (o0, l0), (o1, l1), (o2, l2) = attn
    return pl.pallas_call(
        functools.partial(_mixer_kernel, tm=tm, dn_alpha=np.float32(dn_alpha),
                          first_tiles=first_tiles if isinstance(x, tuple) else None),
        grid=(t // tm,),
        in_specs=[zspec(0), zspec(1), zspec(2), zspec(3),
                  split(o0), split(o1), split(o2), split(l0), split(l1), split(l2), *x_specs,
                  vec(D_MODEL), vec(D_MODEL),
                  pl.BlockSpec((None, A_GROUPS, CHUNK, CHUNK), lambda i: (layer, 0, 0, 0)),
                  pl.BlockSpec((None, A_GROUPS, CHUNK, CHUNK), lambda i: (layer, 0, 0, 0)),
                  mat(D_MODEL, D_MODEL), mat(B_GROUP_WIDTH, D_MODEL), mat(D_MODEL, D_MODEL),
                  vec(D_MODEL), vec(D_MODEL), mat(D_MODEL, LANES), vec(LANES)],
        out_specs=[row(D_MODEL), row(D_MODEL), row(LANES)],
        out_shape=[jax.ShapeDtypeStruct((t, D_MODEL), F32), jax.ShapeDtypeStruct((t, D_MODEL), BF16),
                   jax.ShapeDtypeStruct((t, LANES), F32)],
        scratch_shapes=[pltpu.VMEM((tm, D_MODEL), BF16), pltpu.VMEM((tm, B_GROUP_WIDTH), BF16),
                        pltpu.VMEM((len(B_PATTERNS), B_HEADS, tm, B_HEAD_DIM), F32),
                        pltpu.VMEM((len(B_PATTERNS), tm, LANES), F32),
                        pltpu.VMEM((MAX_ROW_STRIDE, B_HEADS + 1, tm // MAX_ROW_STRIDE, LANES), F32)],
        compiler_params=pltpu.CompilerParams(
            dimension_semantics=("parallel",), vmem_limit_bytes=VMEM_LIMIT),
        name="mixer",
    )(z, z, z, z, o0, o1, o2, l0, l1, l2, xa, xb,
      p["ln_v_g"], p["ln_v_b"], p["w_s"], p["b_s"], p["w_pa"], p["w_pb"], p["w_o"],
      p["ln1_g"], p["ln1_b"], p["w_r"], p["b_r"])


TS = 512
CHUNK_ROWS = 8
REGION = TS * TOP_K + N_EXPERTS * CHUNK_ROWS
P_ROWS = 256


def _chunks_per_expert(cnt):
    return jnp.floor((cnt + (CHUNK_ROWS - 1)) * (1.0 / CHUNK_ROWS))


def _dispatch_kernel(logit_ref, x_ref, upper_ref, lower_ref, xs_ref, pos_ref, gate_ref, cnt_ref):
    l = jnp.transpose(logit_ref[...])[0:N_EXPERTS, :]
    sub = lax.broadcasted_iota(jnp.int32, (N_EXPERTS, TS), 0)
    onehots, vals = [], []
    for _ in range(TOP_K):
        mx = jnp.max(l, axis=0, keepdims=True)
        idx = jnp.min(jnp.where(l == mx, sub, N_EXPERTS), axis=0, keepdims=True)
        oh = sub == idx
        l = jnp.where(oh, -jnp.inf, l)
        onehots.append(oh)
        vals.append(mx)
    es = [jnp.exp(v - vals[0]) for v in vals]
    den = es[0] + es[1] + es[2] + es[3]
    multi = jnp.zeros((N_EXPERTS, TS), F32)
    for oh in onehots:
        multi = multi + oh.astype(F32)
    multi_b = jnp.concatenate([multi, jnp.zeros((LANES - N_EXPERTS, TS), F32)], axis=0).astype(BF16)
    cnt_row = lax.dot_general(jnp.ones((8, TS), BF16), multi_b, (((1,), (1,)), ((), ())),
                              preferred_element_type=F32)
    cnt_ref[...] = cnt_row.astype(jnp.int32)
    cnt_col = jnp.concatenate([jnp.sum(multi, axis=1, keepdims=True),
                               jnp.zeros((LANES - N_EXPERTS, 1), F32)], axis=0)
    n_chunks = jnp.broadcast_to(_chunks_per_expert(cnt_col), (LANES, LANES)).astype(BF16)
    run_off = CHUNK_ROWS * jnp.dot(lower_ref[...], n_chunks, preferred_element_type=F32)[0:N_EXPERTS, 0:1]
    rank = jnp.dot(multi_b, upper_ref[...], preferred_element_type=F32)[0:N_EXPERTS, :]
    slot = rank + run_off
    pos_rows = [jnp.sum(jnp.where(onehots[k], slot, 0.0), axis=0, keepdims=True) for k in range(TOP_K)]
    pad = jnp.full((LANES - TOP_K, TS), -1.0, F32)
    pos_all = jnp.concatenate(pos_rows + [pad], axis=0)
    pos_ref[...] = jnp.transpose(pos_all).astype(jnp.int32)
    gate_ref[...] = jnp.transpose(jnp.concatenate([e / den for e in es] + [pad], axis=0))
    live = pl.program_id(0) < pl.num_programs(0) - 1
    pos_t = jnp.where(live, pos_all[0:8, :], -1.0)
    row = lax.broadcasted_iota(jnp.int32, (P_ROWS, TS), 0).astype(F32).astype(BF16)
    for c in range(REGION // P_ROWS):
        rel = pos_t - float(c * P_ROWS)
        rel = jnp.where(jnp.logical_and(rel >= 0.0, rel < float(P_ROWS)), rel, -1.0).astype(BF16)
        perm = jnp.zeros((P_ROWS, TS), BF16)
        for k in range(TOP_K):
            perm = jnp.where(row == rel[k:k + 1, :], jnp.ones((), BF16), perm)
        xs_ref[c * P_ROWS:(c + 1) * P_ROWS, :] = jnp.dot(
            perm, x_ref[...], preferred_element_type=F32).astype(BF16)


def _dispatch(logits, x1b):
    t = logits.shape[0]
    n_tiles = t // TS
    upper = jnp.asarray(np.triu(np.ones((TS, TS), np.float32), 1), dtype=BF16)
    lower = jnp.asarray(np.tril(np.ones((LANES, LANES), np.float32), -1), dtype=BF16)

    def tile(i):
        return jnp.minimum(i, n_tiles - 1)

    return pl.pallas_call(
        _dispatch_kernel,
        grid=(n_tiles + 1,),
        in_specs=[pl.BlockSpec((TS, LANES), lambda i: (tile(i), 0)),
                  pl.BlockSpec((TS, D_MODEL), lambda i: (tile(i), 0)),
                  pl.BlockSpec((TS, TS), lambda i: (0, 0)),
                  pl.BlockSpec((LANES, LANES), lambda i: (0, 0))],
        out_specs=[pl.BlockSpec((REGION, D_MODEL), lambda i: (i, 0)),
                   pl.BlockSpec((TS, LANES), lambda i: (tile(i), 0)),
                   pl.BlockSpec((TS, LANES), lambda i: (tile(i), 0)),
                   pl.BlockSpec((None, 8, LANES), lambda i: (tile(i), 0, 0))],
        out_shape=[jax.ShapeDtypeStruct(((n_tiles + 1) * REGION, D_MODEL), BF16),
                   jax.ShapeDtypeStruct((t, LANES), jnp.int32), jax.ShapeDtypeStruct((t, LANES), F32),
                   jax.ShapeDtypeStruct((n_tiles, 8, LANES), jnp.int32)],
        compiler_params=pltpu.CompilerParams(
            dimension_semantics=("arbitrary",), vmem_limit_bytes=VMEM_LIMIT),
        name="dispatch",
    )(logits, x1b, upper, lower)


def _chunk_plan(cnt, bm):
    assert REGION >= 3 * bm
    n_tiles = cnt.shape[0]
    ch = bm // CHUNK_ROWS
    n_rows_max = n_tiles * REGION + N_EXPERTS * (bm - CHUNK_ROWS)
    n_blk = -(-n_rows_max // bm)
    run_chunks = (cnt + CHUNK_ROWS - 1) // CHUNK_ROWS
    run_off = (jnp.cumsum(run_chunks, axis=1) - run_chunks) * CHUNK_ROWS
    rows_e = jnp.sum(run_chunks, axis=0) * CHUNK_ROWS
    padded_e = (rows_e + bm - 1) // bm * bm
    pend = jnp.cumsum(padded_e)
    pstart = pend - padded_e
    runs_t = run_chunks.T
    run_start = pstart[:, None] // CHUNK_ROWS + jnp.cumsum(runs_t, axis=1) - runs_t
    run_row = jnp.arange(n_tiles, dtype=jnp.int32)[None, :] * REGION + run_off.T
    blk = jnp.arange(n_blk + 1, dtype=jnp.int32)
    blk_e = jnp.minimum(jnp.sum((pend[None, :] <= blk[:, None] * bm).astype(jnp.int32), axis=1), N_EXPERTS - 1)
    is_e = blk_e[:, None, None] == jnp.arange(N_EXPERTS, dtype=jnp.int32)[None, :, None]
    c = (blk[:, None] * ch + jnp.arange(ch, dtype=jnp.int32)[None, :]).reshape(-1)

    def of_run(v):
        v_b = jnp.sum(jnp.where(is_e, v[None], 0), axis=1)
        step = v_b - jnp.concatenate([jnp.zeros_like(v_b[:, :1]), v_b[:, :-1]], axis=1)
        return jnp.sum(jnp.where(started, step[:, None, :], 0), axis=-1).reshape(-1)

    start_b = jnp.sum(jnp.where(is_e, run_start[None], 0), axis=1)
    started = start_b[:, None, :] <= c.reshape(n_blk + 1, ch)[:, :, None]
    j = c - of_run(run_start)
    valid = j < of_run(runs_t)
    row = of_run(run_row) + j * CHUNK_ROWS
    spare = n_tiles * REGION + (c % ch) * CHUNK_ROWS
    src = jnp.where(valid, row, spare + 2 * bm).astype(jnp.int32).reshape(n_blk + 1, 1, ch)
    dst = jnp.where(valid, row, spare + ((c // ch) % 2) * bm).astype(jnp.int32).reshape(n_blk + 1, 1, ch)
    dst = jnp.concatenate([dst[n_blk:], dst[:n_blk]])
    blk_e = blk_e[:n_blk].astype(jnp.int32)
    n_used = (pend[-1:] // bm).astype(jnp.int32)
    plan = jnp.concatenate([src[1:], dst[:n_blk], dst[1:]], axis=-1)
    return src[:1], plan, blk_e, n_used


def _ffn_kernel(blk_e_ref, n_used_ref, src0_ref, plan_ref, xs_hbm, wgu_ref, bgu_ref, wd_ref,
                bd_ref, ys_hbm, xbuf, ybuf, wgu_b, wd_b, in_sem, out_sem, *, bm):
    i = pl.program_id(0)
    n_used = n_used_ref[0]
    slot = i % 2
    ch = bm // CHUNK_ROWS

    def gather(tbl_ref, col0, s):
        for j in range(ch):
            row = pl.multiple_of(tbl_ref[0, col0 + j], CHUNK_ROWS)
            pltpu.make_async_copy(xs_hbm.at[pl.ds(row, CHUNK_ROWS)],
                                  xbuf.at[s, pl.ds(j * CHUNK_ROWS, CHUNK_ROWS)], in_sem.at[s]).start()

    def scatter(tbl_ref, col0, s):
        for j in range(ch):
            row = pl.multiple_of(tbl_ref[0, col0 + j], CHUNK_ROWS)
            pltpu.make_async_copy(ybuf.at[s, pl.ds(j * CHUNK_ROWS, CHUNK_ROWS)],
                                  ys_hbm.at[pl.ds(row, CHUNK_ROWS)], out_sem.at[s]).start()

    def wait_in(s):
        pltpu.make_async_copy(xs_hbm.at[pl.ds(0, bm)], xbuf.at[s], in_sem.at[s]).wait()

    def wait_out(s):
        pltpu.make_async_copy(ybuf.at[s], ys_hbm.at[pl.ds(0, bm)], out_sem.at[s]).wait()

    used = i < n_used
    e_changed = jnp.logical_or(i == 0, blk_e_ref[i] != blk_e_ref[jnp.maximum(i - 1, 0)])

    @pl.when(jnp.logical_and(i == 0, used))
    def _():
        gather(src0_ref, 0, 0)
        ybuf[1] = jnp.zeros((bm, D_MODEL), BF16)

    @pl.when(jnp.logical_and(used, e_changed))
    def _():
        wgu_b[...] = wgu_ref[...].astype(BF16)
        wd_b[...] = wd_ref[...].astype(BF16)

    @pl.when(jnp.logical_and(used, i >= 1))
    def _():
        wait_out(slot)

    @pl.when(used)
    def _():
        wait_in(slot)
        gu = jnp.dot(xbuf[slot], wgu_b[...], preferred_element_type=F32) + bgu_ref[...]
        gather(plan_ref, 0, 1 - slot)
        scatter(plan_ref, ch, 1 - slot)
        gate = jnp.minimum(gu[:, :D_MODEL], SWIGLU_LIMIT)
        up = jnp.clip(gu[:, D_MODEL:], -SWIGLU_LIMIT, SWIGLU_LIMIT)
        h = (up + 1.0) * gate * _sigmoid(SWIGLU_ALPHA * gate)
        y = jnp.dot(h.astype(BF16), wd_b[...], preferred_element_type=F32) + bd_ref[...]
        ybuf[slot] = y.astype(BF16)

    @pl.when(i == n_used - 1)
    def _():
        scatter(plan_ref, 2 * ch, slot)
        wait_in(1 - slot)
        wait_out(1 - slot)
        wait_out(slot)


def _ffn(xs, src0, plan, blk_e, n_used, p, layer, bm):
    n_blk = blk_e.shape[0]
    ch = src0.shape[-1]
    smem = functools.partial(pl.BlockSpec, memory_space=pltpu.SMEM)
    grid_spec = pltpu.PrefetchScalarGridSpec(
        num_scalar_prefetch=2,
        grid=(n_blk,),
        in_specs=[smem((None, 1, ch), lambda i, be, nu: (0, 0, 0)),
                  smem((None, 1, 3 * ch), lambda i, be, nu: (i, 0, 0)),
                  pl.BlockSpec(memory_space=pl.ANY),
                  pl.BlockSpec((None, None, D_MODEL, 2 * D_MODEL), lambda i, be, nu: (layer, be[i], 0, 0)),
                  pl.BlockSpec((None, None, 1, 2 * D_MODEL), lambda i, be, nu: (layer, be[i], 0, 0)),
                  pl.BlockSpec((None, None, D_MODEL, D_MODEL), lambda i, be, nu: (layer, be[i], 0, 0)),
                  pl.BlockSpec((None, None, 1, D_MODEL), lambda i, be, nu: (layer, be[i], 0, 0))],
        out_specs=pl.BlockSpec(memory_space=pl.ANY),
        scratch_shapes=[pltpu.VMEM((2, bm, D_MODEL), BF16), pltpu.VMEM((2, bm, D_MODEL), BF16),
                        pltpu.VMEM((D_MODEL, 2 * D_MODEL), BF16), pltpu.VMEM((D_MODEL, D_MODEL), BF16),
                        pltpu.SemaphoreType.DMA((2,)), pltpu.SemaphoreType.DMA((2,))],
    )
    return pl.pallas_call(
        functools.partial(_ffn_kernel, bm=bm),
        grid_spec=grid_spec,
        out_shape=jax.ShapeDtypeStruct(xs.shape, BF16),
        input_output_aliases={4: 0},
        compiler_params=pltpu.CompilerParams(
            dimension_semantics=("arbitrary",), vmem_limit_bytes=VMEM_LIMIT),
        name="ffn",
    )(blk_e, n_used, src0, plan, xs, p["w_gu"], p["b_gu"], p["w_down"], p["b_down"])


G_ROWS = 128


def _combine_kernel(ys_ref, pos_ref, gate_ref, x_ref, g_ref, b_ref, out0_ref, out1_ref, g_scr, *x_scr,
                    dn_alpha, first_tiles):
    nr = G_ROWS
    lane = lax.broadcasted_iota(jnp.int32, (nr, LANES), 1)
    for r0 in range(0, TS, nr):
        rows = slice(r0, r0 + nr)
        tile_of, gate = [], []
        for k in range(TOP_K):
            rel = jnp.broadcast_to(pos_ref[rows, k:k + 1], (nr, LANES)) - lane
            hit = jnp.where((rel & (LANES - 1)) == 0, rel >> (LANES.bit_length() - 1), -1)
            tile_of.append(hit.astype(F32).astype(BF16))
            gate.append(jnp.broadcast_to(gate_ref[rows, k:k + 1], (nr, LANES)).astype(BF16))
        for c in range(REGION // LANES):
            g = jnp.zeros((nr, LANES), BF16)
            for k in range(TOP_K):
                g = jnp.where(tile_of[k] == float(c), gate[k], g)
            g_scr[rows, c * LANES:(c + 1) * LANES] = g
        y = jnp.dot(g_scr[rows, :], ys_ref[...], preferred_element_type=F32)
        x2 = _layer_norm(dn_alpha * x_ref[rows, :] + y, g_ref[...], b_ref[...])
        if first_tiles is None:
            out0_ref[rows, :] = x2
            out1_ref[rows, :] = x2.astype(BF16)
        else:
            x_scr[0][rows, :] = x2
    if first_tiles is not None:
        @pl.when(pl.program_id(0) < first_tiles)
        def _():
            out0_ref[...] = x_scr[0][...]

        @pl.when(pl.program_id(0) >= first_tiles)
        def _():
            out1_ref[...] = x_scr[0][...]


def _combine(ys, pos, gates, x1, p, layer, dn_alpha, first_tokens=None):
    t = x1.shape[0]
    if first_tokens is None:
        first_tiles = None
        out_specs = [pl.BlockSpec((TS, D_MODEL), lambda i: (i, 0)), pl.BlockSpec((TS, D_MODEL), lambda i: (i, 0))]
        out_shape = [jax.ShapeDtypeStruct((t, D_MODEL), F32), jax.ShapeDtypeStruct((t, D_MODEL), BF16)]
        scratch = []
    else:
        first_tiles = first_tokens // TS
        out_specs = [pl.BlockSpec((TS, D_MODEL), lambda i: (jnp.minimum(i, first_tiles - 1), 0)),
                     pl.BlockSpec((TS, D_MODEL), lambda i: (jnp.maximum(i - first_tiles, 0), 0))]
        out_shape = [jax.ShapeDtypeStruct((first_tokens, D_MODEL), F32),
                     jax.ShapeDtypeStruct((t - first_tokens, D_MODEL), F32)]
        scratch = [pltpu.VMEM((TS, D_MODEL), F32)]
    return pl.pallas_call(
        functools.partial(_combine_kernel, dn_alpha=np.float32(dn_alpha), first_tiles=first_tiles),
        grid=(t // TS,),
        in_specs=[pl.BlockSpec((REGION, D_MODEL), lambda i: (i, 0)),
                  pl.BlockSpec((TS, LANES), lambda i: (i, 0)),
                  pl.BlockSpec((TS, LANES), lambda i: (i, 0)),
                  pl.BlockSpec((TS, D_MODEL), lambda i: (i, 0)),
                  pl.BlockSpec((None, 1, D_MODEL), lambda i: (layer, 0, 0)),
                  pl.BlockSpec((None, 1, D_MODEL), lambda i: (layer, 0, 0))],
        out_specs=out_specs,
        out_shape=out_shape,
        scratch_shapes=[pltpu.VMEM((TS, REGION), BF16)] + scratch,
        compiler_params=pltpu.CompilerParams(
            dimension_semantics=("arbitrary",), vmem_limit_bytes=VMEM_LIMIT),
        name="combine",
    )(ys, pos, gates, x1, p["ln2_g"], p["ln2_b"])


def _prepare_params(w_in, b_in, ln_v_g, ln_v_b, w_s, b_s, w_pa, w_pb, w_o, ln1_g, ln1_b,
                    w_r, b_r, w_gu, b_gu, w_down, b_down, ln2_g, ln2_b):
    depth = w_in.shape[0]
    n_qkv = 3 * len(B_PATTERNS) * B_GROUP_WIDTH
    qkv0, qkv1 = 2 * D_MODEL, 2 * D_MODEL + n_qkv

    def permute_cols(a):
        gap = jnp.zeros(a.shape[:-1] + (MAIN_COL0 - n_qkv,), a.dtype)
        return jnp.concatenate([a[..., qkv0:qkv1], gap, a[..., :qkv0], a[..., qkv1:]], axis=-1)

    def vec(a):
        return a.reshape(depth, 1, a.shape[-1]).astype(F32)

    w_r_pad = jnp.pad(w_r, ((0, 0), (0, 0), (0, LANES - N_EXPERTS))).astype(BF16)
    b_r_pad = jnp.pad(b_r.astype(F32), ((0, 0), (0, LANES - N_EXPERTS)), constant_values=NEG_INF)
    return dict(
        w_in=permute_cols(w_in).astype(BF16), b_in=vec(permute_cols(b_in)),
        ln_v_g=vec(ln_v_g), ln_v_b=vec(ln_v_b),
        w_s=w_s.astype(BF16),
        b_s=jnp.broadcast_to(b_s.astype(F32)[..., None], b_s.shape + (CHUNK,)),
        w_pa=w_pa.astype(BF16), w_pb=w_pb.astype(BF16), w_o=w_o.astype(BF16),
        ln1_g=vec(ln1_g), ln1_b=vec(ln1_b),
        w_r=w_r_pad, b_r=vec(b_r_pad),
        w_gu=w_gu, b_gu=b_gu.astype(F32)[:, :, None, :],
        w_down=w_down, b_down=b_down.astype(F32)[:, :, None, :],
        ln2_g=vec(ln2_g), ln2_b=vec(ln2_b),
    )


def _trunk(x_first, x_rest, params, bm=512):
    n_first, seq, _ = x_first.shape
    n_seq = n_first + x_rest.shape[0]
    depth = params["w_in"].shape[0]
    dn_alpha = (2.0 * depth) ** 0.25
    x = (x_first.reshape(-1, D_MODEL), x_rest.reshape(-1, D_MODEL))
    xb = jnp.concatenate([x[0].astype(BF16), x[1].astype(BF16)], axis=0)
    for layer in range(depth):
        z = _inproj_main(xb, params["w_in"], params["b_in"], layer)
        attn = [_attention(_inproj_qkv(xb, params["w_in"], params["b_in"], layer, g, n_seq, seq), g)
                for g in range(len(B_PATTERNS))]
        x1, x1b, logits = _mixer(z, attn, x, params, layer, dn_alpha, seq)
        xs, pos, gates, cnt = _dispatch(logits, x1b)
        src0, plan, blk_e, n_used = _chunk_plan(cnt[:, 0, :N_EXPERTS], bm)
        ys = _ffn(xs, src0, plan, blk_e, n_used, params, layer, bm)
        last = layer == depth - 1
        x, xb = _combine(ys, pos, gates, x1, params, layer, dn_alpha, n_first * seq if last else None)
    return x.reshape(n_first, seq, D_MODEL), xb.reshape(n_seq - n_first, seq, D_MODEL)


def kernel(x_prompt, x_sample, w_in, b_in, ln_v_g, ln_v_b, w_s, b_s, w_pa, w_pb, w_o, ln1_g, ln1_b,
           w_r, b_r, w_gu, b_gu, w_down, b_down, ln2_g, ln2_b):
    params = _prepare_params(w_in, b_in, ln_v_g, ln_v_b, w_s, b_s, w_pa, w_pb, w_o, ln1_g, ln1_b,
                             w_r, b_r, w_gu, b_gu, w_down, b_down, ln2_g, ln2_b)
    return _trunk(x_prompt, x_sample, params)
```

```python
import functools

import numpy as np
import jax
import jax.numpy as jnp
from jax import lax
from jax.experimental import pallas as pl
from jax.experimental.pallas import tpu as pltpu

D_MODEL = 1024
CHUNK = 128
A_GROUPS = 8
B_PATTERNS = ((128, 1), (512, 4), (2048, 16))
B_HEADS = 4
B_HEAD_DIM = 128
B_GROUP_WIDTH = B_HEADS * B_HEAD_DIM
N_ATTN_HEADS = len(B_PATTERNS) * B_HEADS
N_SIDE = 64
D_IN = 4 * D_MODEL + 3 * len(B_PATTERNS) * B_GROUP_WIDTH
N_EXPERTS = 32
TOP_K = 4
SWIGLU_LIMIT = 7.0
SWIGLU_ALPHA = 1.702
LN_EPS = 1e-5
NEG_INF = -1e30
LANES = 128

MAIN_COLS = 4 * D_MODEL
QKV_COLS = 3 * B_GROUP_WIDTH
MAIN_COL0 = -(-len(B_PATTERNS) * QKV_COLS // D_MODEL) * D_MODEL
GELU_COLS = 2 * D_MODEL

VMEM_LIMIT = 56 * 1024 * 1024

BF16 = jnp.bfloat16
F32 = jnp.float32


def _gelu(x):
    return 0.5 * x * (1.0 + lax.erf(x * np.float32(np.sqrt(0.5))))


def _sigmoid(x):
    return 0.5 * jnp.tanh(0.5 * x) + 0.5


def _layer_norm(x, g, b):
    xc = x - jnp.mean(x, axis=-1, keepdims=True)
    var = jnp.mean(xc * xc, axis=-1, keepdims=True)
    return xc * lax.rsqrt(var + LN_EPS) * g + b


N_CHUNK = 256
MAX_ROW_STRIDE = 4


def _inproj_main_kernel(x_ref, w_ref, b_ref, z_ref, *, n_gelu_tiles):
    j = pl.program_id(1)

    def run(act):
        for c in range(z_ref.shape[1] // N_CHUNK):
            cols = slice(c * N_CHUNK, (c + 1) * N_CHUNK)
            acc = jnp.dot(x_ref[...], w_ref[:, cols], preferred_element_type=F32) + b_ref[:, cols]
            z_ref[:, cols] = act(acc).astype(BF16)

    @pl.when(j < n_gelu_tiles)
    def _():
        run(_gelu)

    @pl.when(j >= n_gelu_tiles)
    def _():
        run(lambda a: a)


def _inproj_main(xb, w_in, b_in, layer, tm=2048, tn=1024):
    t = xb.shape[0]
    return pl.pallas_call(
        functools.partial(_inproj_main_kernel, n_gelu_tiles=GELU_COLS // tn),
        grid=(t // tm, MAIN_COLS // tn),
        in_specs=[
            pl.BlockSpec((tm, D_MODEL), lambda i, j: (i, 0)),
            pl.BlockSpec((None, D_MODEL, tn), lambda i, j: (layer, 0, MAIN_COL0 // tn + j)),
            pl.BlockSpec((None, 1, tn), lambda i, j: (layer, 0, MAIN_COL0 // tn + j)),
        ],
        out_specs=pl.BlockSpec((tm, tn), lambda i, j: (i, j)),
        out_shape=jax.ShapeDtypeStruct((t, MAIN_COLS), BF16),
        compiler_params=pltpu.CompilerParams(
            dimension_semantics=("parallel", "arbitrary"), vmem_limit_bytes=VMEM_LIMIT),
        name="inproj_main",
    )(xb, w_in, b_in)


def _inproj_qkv_kernel(x_ref, w_ref, b_ref, o_ref, *scratch, dilation):
    rows = x_ref.shape[0] // dilation
    for c in range(o_ref.shape[-1] // N_CHUNK):
        cols = slice(c * N_CHUNK, (c + 1) * N_CHUNK)
        acc = jnp.dot(x_ref[...], w_ref[:, cols], preferred_element_type=F32) + b_ref[:, cols]
        if dilation == 1:
            o_ref[0, :, cols] = acc.astype(BF16)
            continue
        scr, scr2 = scratch[0], scratch[-1]
        d1 = min(dilation, MAX_ROW_STRIDE)
        d2 = dilation // d1
        for q in range(c * N_CHUNK // LANES, (c + 1) * N_CHUNK // LANES):
            scr[q] = acc[:, q * LANES - c * N_CHUNK:(q + 1) * LANES - c * N_CHUNK]
            for r1 in range(d1):
                part = scr[q, pl.ds(r1, rows * d2, stride=d1), :]
                if d2 == 1:
                    o_ref[r1, :, q * LANES:(q + 1) * LANES] = part.astype(BF16)
                    continue
                scr2[q, r1 * rows * d2:(r1 + 1) * rows * d2, :] = part
                for r2 in range(d2):
                    o_ref[r2 * d1 + r1, :, q * LANES:(q + 1) * LANES] = (
                        scr2[q, pl.ds(r1 * rows * d2 + r2, rows, stride=d2), :].astype(BF16))


def _inproj_qkv(xb, w_in, b_in, layer, group, n_seq, seq, tn=QKV_COLS):
    _, dilation = B_PATTERNS[group]
    tm = 2048 if dilation <= MAX_ROW_STRIDE else 1024
    tiles_per_seq = seq // tm
    col0 = group * QKV_COLS // tn
    return pl.pallas_call(
        functools.partial(_inproj_qkv_kernel, dilation=dilation),
        grid=(n_seq * tiles_per_seq, QKV_COLS // tn),
        in_specs=[
            pl.BlockSpec((tm, D_MODEL), lambda i, j: (i, 0)),
            pl.BlockSpec((None, D_MODEL, tn), lambda i, j: (layer, 0, col0 + j)),
            pl.BlockSpec((None, 1, tn), lambda i, j: (layer, 0, col0 + j)),
        ],
        out_specs=pl.BlockSpec((None, dilation, tm // dilation, tn),
                               lambda i, j: (i // tiles_per_seq, 0, i % tiles_per_seq, j)),
        out_shape=jax.ShapeDtypeStruct((n_seq, dilation, seq // dilation, QKV_COLS), BF16),
        scratch_shapes=[pltpu.VMEM((tn // LANES, tm, LANES), F32)] * (
            0 if dilation == 1 else 1 if dilation <= MAX_ROW_STRIDE else 2),
        compiler_params=pltpu.CompilerParams(
            dimension_semantics=("parallel", "arbitrary"), vmem_limit_bytes=VMEM_LIMIT),
        name=f"inproj_qkv{group}",
    )(xb, w_in, b_in)


def _attn_kernel(cur_ref, prev_ref, next_ref, bias_ref, o_ref, lse_ref, *scratch, bq):
    for rb in range(cur_ref.shape[0]):
        _attn_rows(cur_ref.at[rb], prev_ref.at[rb], next_ref.at[rb], bias_ref, o_ref.at[rb], lse_ref.at[rb],
                   scratch, bq)


def _attn_rows(cur_ref, prev_ref, next_ref, bias_ref, o_ref, lse_ref, scratch, bq):
    s_scr, p_scr = scratch[:B_HEADS], scratch[B_HEADS:]
    jq = pl.program_id(2)
    nq = pl.num_programs(2)
    n_piece = bq // N_SIDE
    col = lax.broadcasted_iota(jnp.int32, (N_SIDE, 3 * N_SIDE), 1)
    lane = lax.broadcasted_iota(jnp.int32, (N_SIDE, LANES), 1)
    scale = np.float32(B_HEAD_DIM ** -0.5)

    def window(p, part, h):
        c0 = part * B_GROUP_WIDTH + h * B_HEAD_DIM
        cols = slice(c0, c0 + B_HEAD_DIM)
        if n_piece == 1:
            return jnp.concatenate([prev_ref[:, cols], cur_ref[:, cols], next_ref[:, cols]], axis=0)
        if p == 0:
            return jnp.concatenate([prev_ref[:, cols], cur_ref[0:2 * N_SIDE, cols]], axis=0)
        if p == n_piece - 1:
            return jnp.concatenate([cur_ref[bq - 2 * N_SIDE:bq, cols], next_ref[:, cols]], axis=0)
        return cur_ref[(p - 1) * N_SIDE:(p + 2) * N_SIDE, cols]

    def scores(p, h):
        qh = cur_ref[p * N_SIDE:(p + 1) * N_SIDE, h * B_HEAD_DIM:(h + 1) * B_HEAD_DIM]
        s = lax.dot_general(qh, window(p, 1, h), (((1,), (1,)), ((), ())), preferred_element_type=F32) * scale
        if p == 0:
            s = jnp.where(jnp.logical_and(col < N_SIDE, jq == 0), NEG_INF, s)
        if p == n_piece - 1:
            s = jnp.where(jnp.logical_and(col >= 2 * N_SIDE, jq == nq - 1), NEG_INF, s)
        return s + bias_ref[h]

    def softmax(s):
        m = jnp.max(s, axis=-1, keepdims=True)
        e = jnp.exp(s - m)
        den = jnp.sum(e, axis=-1, keepdims=True)
        return (e / den).astype(BF16), m + jnp.log(den)

    def values(p, h, pn):
        o = jnp.dot(pn, window(p, 2, h), preferred_element_type=F32)
        o_ref[p * N_SIDE:(p + 1) * N_SIDE, h * B_HEAD_DIM:(h + 1) * B_HEAD_DIM] = o.astype(BF16)

    lse = {}
    for t in range(B_HEADS + 2):
        if t < B_HEADS:
            for p in range(n_piece):
                s_scr[t][p * N_SIDE:(p + 1) * N_SIDE, :] = scores(p, t)
        if 1 <= t <= B_HEADS:
            for p in range(n_piece):
                rows = slice(p * N_SIDE, (p + 1) * N_SIDE)
                p_scr[t - 1][rows, :], lse[p, t - 1] = softmax(s_scr[t - 1][rows, :])
        if t >= 2:
            for p in range(n_piece):
                values(p, t - 2, p_scr[t - 2][p * N_SIDE:(p + 1) * N_SIDE, :])
    for p in range(n_piece):
        tile = jnp.zeros((N_SIDE, LANES), F32)
        for h in range(B_HEADS):
            tile = jnp.where(lane == h, lse[p, h], tile)
        lse_ref[p * N_SIDE:(p + 1) * N_SIDE, :] = tile


def _attn_bias(group):
    _, dilation = B_PATTERNS[group]
    slopes = np.array([2.0 ** (-8.0 * (group * B_HEADS + h + 1) / N_ATTN_HEADS) for h in range(B_HEADS)],
                      dtype=np.float32)
    a = np.arange(N_SIDE)[:, None]
    c = np.arange(3 * N_SIDE)[None, :]
    rel = c - N_SIDE - a
    alibi = (-(slopes[:, None, None] * (dilation * np.abs(rel))[None])).astype(np.float32)
    return np.where((np.abs(rel) <= N_SIDE)[None], alibi, np.float32(NEG_INF)).astype(np.float32)


def _attention(qkv, group, rows_per_step=1024):
    n_seq, dilation, sub_len, _ = qkv.shape
    bq = min(rows_per_step, sub_len)
    nq = sub_len // bq
    rb = min(dilation, rows_per_step // bq)
    halo_per_bq = bq // N_SIDE
    n_halo = sub_len // N_SIDE
    return pl.pallas_call(
        functools.partial(_attn_kernel, bq=bq),
        grid=(n_seq, dilation // rb, nq),
        in_specs=[pl.BlockSpec((None, rb, bq, QKV_COLS), lambda b, r, j: (b, r, j, 0)),
                  pl.BlockSpec((None, rb, N_SIDE, QKV_COLS),
                               lambda b, r, j: (b, r, jnp.maximum(j * halo_per_bq - 1, 0), 0)),
                  pl.BlockSpec((None, rb, N_SIDE, QKV_COLS),
                               lambda b, r, j: (b, r, jnp.minimum((j + 1) * halo_per_bq, n_halo - 1), 0)),
                  pl.BlockSpec((B_HEADS, N_SIDE, 3 * N_SIDE), lambda b, r, j: (0, 0, 0))],
        out_specs=[pl.BlockSpec((None, rb, bq, B_GROUP_WIDTH), lambda b, r, j: (b, r, j, 0)),
                   pl.BlockSpec((None, rb, bq, LANES), lambda b, r, j: (b, r, j, 0))],
        out_shape=[jax.ShapeDtypeStruct((n_seq, dilation, sub_len, B_GROUP_WIDTH), BF16),
                   jax.ShapeDtypeStruct((n_seq, dilation, sub_len, LANES), F32)],
        scratch_shapes=([pltpu.VMEM((bq, 3 * N_SIDE), F32)] * B_HEADS
                        + [pltpu.VMEM((bq, 3 * N_SIDE), BF16)] * B_HEADS),
        compiler_params=pltpu.CompilerParams(
            dimension_semantics=("parallel", "parallel", "arbitrary"), vmem_limit_bytes=VMEM_LIMIT),
        name=f"attn{group}",
    )(qkv, qkv, qkv, jnp.asarray(_attn_bias(group)))


def _mixer_kernel(u_ref, v_ref, ga_ref, gb_ref, o0_ref, o1_ref, o2_ref, l0_ref, l1_ref, l2_ref, x_ref, xrest_ref,
                  lnvg_ref, lnvb_ref, ws_ref, bs_ref, wpa_ref, wpb_ref, wo_ref, ln1g_ref, ln1b_ref,
                  wr_ref, br_ref, x1_ref, x1b_ref, logit_ref, a_scr, b_scr, o_scr, l_scr, t_scr,
                  *, tm, dn_alpha, first_tiles):
    for g, (o_ref, l_ref) in enumerate(((o0_ref, l0_ref), (o1_ref, l1_ref), (o2_ref, l2_ref))):
        dilation = o_ref.shape[0]
        rows = tm // dilation
        d1 = min(dilation, MAX_ROW_STRIDE)
        d2 = dilation // d1

        def pieces(r):
            return [o_ref[r, :, h * B_HEAD_DIM:(h + 1) * B_HEAD_DIM].astype(F32) for h in range(B_HEADS)] + [l_ref[r]]

        def place(idx, vals):
            for h in range(B_HEADS):
                o_scr[g, h, idx, :] = vals[h]
            l_scr[g, idx, :] = vals[B_HEADS]

        for r1 in range(d1):
            idx = pl.ds(r1, rows * d2, stride=d1) if d1 > 1 else slice(None)
            if d2 == 1:
                place(idx, pieces(r1))
                continue
            for r2 in range(d2):
                for j, val in enumerate(pieces(r2 * d1 + r1)):
                    t_scr[r1, j, pl.ds(r2, rows, stride=d2), :] = val
            place(idx, [t_scr[r1, j] for j in range(B_HEADS + 1)])
    n_groups = len(B_PATTERNS)

    def branches(r0, nr):
        rows = slice(r0, r0 + nr)
        vn = _layer_norm(v_ref[rows, :].astype(F32), lnvg_ref[...], lnvb_ref[...]).astype(BF16)
        for g in range(A_GROUPS):
            c0 = g * CHUNK
            chunks = jnp.concatenate([vn[q0:q0 + CHUNK, c0:c0 + CHUNK] for q0 in range(0, nr, CHUNK)], axis=1)
            mixed_all = jnp.dot(ws_ref[g], chunks, preferred_element_type=F32)
            for c in range(nr // CHUNK):
                q0 = c * CHUNK
                mixed = mixed_all[:, q0:q0 + CHUNK] + bs_ref[g]
                a = u_ref[r0 + q0:r0 + q0 + CHUNK, c0:c0 + CHUNK].astype(F32) * mixed
                a_scr[r0 + q0:r0 + q0 + CHUNK, c0:c0 + CHUNK] = a.astype(BF16)
        ls = [l_scr[g, rows, :] for g in range(n_groups)]
        mx = jnp.maximum(jnp.maximum(ls[0], ls[1]), ls[2])
        es = [jnp.exp(l - mx) for l in ls]
        den = es[0] + es[1] + es[2]
        wts = [e / den for e in es]
        for h in range(B_HEADS):
            c0 = h * B_HEAD_DIM
            acc = jnp.zeros((nr, B_HEAD_DIM), F32)
            for g in range(n_groups):
                acc = acc + wts[g][:, h:h + 1] * o_scr[g, h, rows, :]
            b_scr[rows, c0:c0 + B_HEAD_DIM] = acc.astype(BF16)

    def project(r0, nr):
        rows = slice(r0, r0 + nr)
        pa = jnp.dot(a_scr[rows, :], wpa_ref[...], preferred_element_type=F32)
        pb = jnp.dot(b_scr[rows, :], wpb_ref[...], preferred_element_type=F32)
        merged = _sigmoid(ga_ref[rows, :].astype(F32)) * pa + _sigmoid(gb_ref[rows, :].astype(F32)) * pb
        mix = jnp.dot(merged.astype(BF16), wo_ref[...], preferred_element_type=F32)
        x = x_ref[rows, :]
        if first_tiles is not None:
            x = jnp.where(pl.program_id(0) < first_tiles, x, xrest_ref[rows, :])
        x1 = _layer_norm(dn_alpha * x + mix, ln1g_ref[...], ln1b_ref[...])
        x1_ref[rows, :] = x1
        x1b = x1.astype(BF16)
        x1b_ref[rows, :] = x1b
        logit_ref[rows, :] = jnp.dot(x1b, wr_ref[...], preferred_element_type=F32) + br_ref[...]

    half = tm // 2
    branches(0, half)
    project(0, half)
    branches(half, half)
    project(half, half)


def _mixer(z, attn, x, p, layer, dn_alpha, seq, tm=512):
    xa, xb = x if isinstance(x, tuple) else (x, x)
    first_tiles = xa.shape[0] // tm
    t = z.shape[0]
    tiles_per_seq = seq // tm
    if isinstance(x, tuple):
        x_specs = [pl.BlockSpec((tm, D_MODEL), lambda i: (jnp.minimum(i, first_tiles - 1), 0)),
                   pl.BlockSpec((tm, D_MODEL), lambda i: (jnp.maximum(i - first_tiles, 0), 0))]
    else:
        x_specs = [pl.BlockSpec((tm, D_MODEL), lambda i: (i, 0)), pl.BlockSpec((tm, D_MODEL), lambda i: (0, 0))]

    def zspec(cb):
        return pl.BlockSpec((tm, D_MODEL), lambda i: (i, cb))

    def row(width):
        return pl.BlockSpec((tm, width), lambda i: (i, 0))

    def split(a):
        _, dilation, _, width = a.shape
        return pl.BlockSpec((None, dilation, tm // dilation, width),
                            lambda i: (i // tiles_per_seq, 0, i % tiles_per_seq, 0))

    def vec(width):
        return pl.BlockSpec((None, 1, width), lambda i: (layer, 0, 0))

    def mat(k, n):
        return pl.BlockSpec((None, k, n), lambda i: (layer, 0, 0))

    (o0, l0), (o1, l1), (o2, l2) = attn
    return pl.pallas_call(
        functools.partial(_mixer_kernel, tm=tm, dn_alpha=np.float32(dn_alpha),
                          first_tiles=first_tiles if isinstance(x, tuple) else None),
        grid=(t // tm,),
        in_specs=[zspec(0), zspec(1), zspec(2), zspec(3),
                  split(o0), split(o1), split(o2), split(l0), split(l1), split(l2), *x_specs,
                  vec(D_MODEL), vec(D_MODEL),
                  pl.BlockSpec((None, A_GROUPS, CHUNK, CHUNK), lambda i: (layer, 0, 0, 0)),
                  pl.BlockSpec((None, A_GROUPS, CHUNK, CHUNK), lambda i: (layer, 0, 0, 0)),
                  mat(D_MODEL, D_MODEL), mat(B_GROUP_WIDTH, D_MODEL), mat(D_MODEL, D_MODEL),
                  vec(D_MODEL), vec(D_MODEL), mat(D_MODEL, LANES), vec(LANES)],
        out_specs=[row(D_MODEL), row(D_MODEL), row(LANES)],
        out_shape=[jax.ShapeDtypeStruct((t, D_MODEL), F32), jax.ShapeDtypeStruct((t, D_MODEL), BF16),
                   jax.ShapeDtypeStruct((t, LANES), F32)],
        scratch_shapes=[pltpu.VMEM((tm, D_MODEL), BF16), pltpu.VMEM((tm, B_GROUP_WIDTH), BF16),
                        pltpu.VMEM((len(B_PATTERNS), B_HEADS, tm, B_HEAD_DIM), F32),
                        pltpu.VMEM((len(B_PATTERNS), tm, LANES), F32),
                        pltpu.VMEM((MAX_ROW_STRIDE, B_HEADS + 1, tm // MAX_ROW_STRIDE, LANES), F32)],
        compiler_params=pltpu.CompilerParams(
            dimension_semantics=("parallel",), vmem_limit_bytes=VMEM_LIMIT),
        name="mixer",
    )(z, z, z, z, o0, o1, o2, l0, l1, l2, xa, xb,
      p["ln_v_g"], p["ln_v_b"], p["w_s"], p["b_s"], p["w_pa"], p["w_pb"], p["w_o"],
      p["ln1_g"], p["ln1_b"], p["w_r"], p["b_r"])


TS = 512
CHUNK_ROWS = 8
REGION = TS * TOP_K + N_EXPERTS * CHUNK_ROWS
P_ROWS = 256


def _chunks_per_expert(cnt):
    return jnp.floor((cnt + (CHUNK_ROWS - 1)) * (1.0 / CHUNK_ROWS))


def _dispatch_kernel(logit_ref, x_ref, upper_ref, lower_ref, xs_ref, pos_ref, gate_ref, cnt_ref):
    l = jnp.transpose(logit_ref[...])[0:N_EXPERTS, :]
    sub = lax.broadcasted_iota(jnp.int32, (N_EXPERTS, TS), 0)
    onehots, vals = [], []
    for _ in range(TOP_K):
        mx = jnp.max(l, axis=0, keepdims=True)
        idx = jnp.min(jnp.where(l == mx, sub, N_EXPERTS), axis=0, keepdims=True)
        oh = sub == idx
        l = jnp.where(oh, -jnp.inf, l)
        onehots.append(oh)
        vals.append(mx)
    es = [jnp.exp(v - vals[0]) for v in vals]
    den = es[0] + es[1] + es[2] + es[3]
    multi = jnp.zeros((N_EXPERTS, TS), F32)
    for oh in onehots:
        multi = multi + oh.astype(F32)
    multi_b = jnp.concatenate([multi, jnp.zeros((LANES - N_EXPERTS, TS), F32)], axis=0).astype(BF16)
    cnt_row = lax.dot_general(jnp.ones((8, TS), BF16), multi_b, (((1,), (1,)), ((), ())),
                              preferred_element_type=F32)
    cnt_ref[...] = cnt_row.astype(jnp.int32)
    cnt_col = jnp.concatenate([jnp.sum(multi, axis=1, keepdims=True),
                               jnp.zeros((LANES - N_EXPERTS, 1), F32)], axis=0)
    n_chunks = jnp.broadcast_to(_chunks_per_expert(cnt_col), (LANES, LANES)).astype(BF16)
    run_off = CHUNK_ROWS * jnp.dot(lower_ref[...], n_chunks, preferred_element_type=F32)[0:N_EXPERTS, 0:1]
    rank = jnp.dot(multi_b, upper_ref[...], preferred_element_type=F32)[0:N_EXPERTS, :]
    slot = rank + run_off
    pos_rows = [jnp.sum(jnp.where(onehots[k], slot, 0.0), axis=0, keepdims=True) for k in range(TOP_K)]
    pad = jnp.full((LANES - TOP_K, TS), -1.0, F32)
    pos_all = jnp.concatenate(pos_rows + [pad], axis=0)
    pos_ref[...] = jnp.transpose(pos_all).astype(jnp.int32)
    gate_ref[...] = jnp.transpose(jnp.concatenate([e / den for e in es] + [pad], axis=0))
    live = pl.program_id(0) < pl.num_programs(0) - 1
    pos_t = jnp.where(live, pos_all[0:8, :], -1.0)
    row = lax.broadcasted_iota(jnp.int32, (P_ROWS, TS), 0).astype(F32).astype(BF16)
    for c in range(REGION // P_ROWS):
        rel = pos_t - float(c * P_ROWS)
        rel = jnp.where(jnp.logical_and(rel >= 0.0, rel < float(P_ROWS)), rel, -1.0).astype(BF16)
        perm = jnp.zeros((P_ROWS, TS), BF16)
        for k in range(TOP_K):
            perm = jnp.where(row == rel[k:k + 1, :], jnp.ones((), BF16), perm)
        xs_ref[c * P_ROWS:(c + 1) * P_ROWS, :] = jnp.dot(
            perm, x_ref[...], preferred_element_type=F32).astype(BF16)


def _dispatch(logits, x1b):
    t = logits.shape[0]
    n_tiles = t // TS
    upper = jnp.asarray(np.triu(np.ones((TS, TS), np.float32), 1), dtype=BF16)
    lower = jnp.asarray(np.tril(np.ones((LANES, LANES), np.float32), -1), dtype=BF16)

    def tile(i):
        return jnp.minimum(i, n_tiles - 1)

    return pl.pallas_call(
        _dispatch_kernel,
        grid=(n_tiles + 1,),
        in_specs=[pl.BlockSpec((TS, LANES), lambda i: (tile(i), 0)),
                  pl.BlockSpec((TS, D_MODEL), lambda i: (tile(i), 0)),
                  pl.BlockSpec((TS, TS), lambda i: (0, 0)),
                  pl.BlockSpec((LANES, LANES), lambda i: (0, 0))],
        out_specs=[pl.BlockSpec((REGION, D_MODEL), lambda i: (i, 0)),
                   pl.BlockSpec((TS, LANES), lambda i: (tile(i), 0)),
                   pl.BlockSpec((TS, LANES), lambda i: (tile(i), 0)),
                   pl.BlockSpec((None, 8, LANES), lambda i: (tile(i), 0, 0))],
        out_shape=[jax.ShapeDtypeStruct(((n_tiles + 1) * REGION, D_MODEL), BF16),
                   jax.ShapeDtypeStruct((t, LANES), jnp.int32), jax.ShapeDtypeStruct((t, LANES), F32),
                   jax.ShapeDtypeStruct((n_tiles, 8, LANES), jnp.int32)],
        compiler_params=pltpu.CompilerParams(
            dimension_semantics=("arbitrary",), vmem_limit_bytes=VMEM_LIMIT),
        name="dispatch",
    )(logits, x1b, upper, lower)


def _chunk_plan(cnt, bm):
    assert REGION >= 3 * bm
    n_tiles = cnt.shape[0]
    ch = bm // CHUNK_ROWS
    n_rows_max = n_tiles * REGION + N_EXPERTS * (bm - CHUNK_ROWS)
    n_blk = -(-n_rows_max // bm)
    run_chunks = (cnt + CHUNK_ROWS - 1) // CHUNK_ROWS
    run_off = (jnp.cumsum(run_chunks, axis=1) - run_chunks) * CHUNK_ROWS
    rows_e = jnp.sum(run_chunks, axis=0) * CHUNK_ROWS
    padded_e = (rows_e + bm - 1) // bm * bm
    pend = jnp.cumsum(padded_e)
    pstart = pend - padded_e
    runs_t = run_chunks.T
    run_start = pstart[:, None] // CHUNK_ROWS + jnp.cumsum(runs_t, axis=1) - runs_t
    run_row = jnp.arange(n_tiles, dtype=jnp.int32)[None, :] * REGION + run_off.T
    blk = jnp.arange(n_blk + 1, dtype=jnp.int32)
    blk_e = jnp.minimum(jnp.sum((pend[None, :] <= blk[:, None] * bm).astype(jnp.int32), axis=1), N_EXPERTS - 1)
    is_e = blk_e[:, None, None] == jnp.arange(N_EXPERTS, dtype=jnp.int32)[None, :, None]
    c = (blk[:, None] * ch + jnp.arange(ch, dtype=jnp.int32)[None, :]).reshape(-1)

    def of_run(v):
        v_b = jnp.sum(jnp.where(is_e, v[None], 0), axis=1)
        step = v_b - jnp.concatenate([jnp.zeros_like(v_b[:, :1]), v_b[:, :-1]], axis=1)
        return jnp.sum(jnp.where(started, step[:, None, :], 0), axis=-1).reshape(-1)

    start_b = jnp.sum(jnp.where(is_e, run_start[None], 0), axis=1)
    started = start_b[:, None, :] <= c.reshape(n_blk + 1, ch)[:, :, None]
    j = c - of_run(run_start)
    valid = j < of_run(runs_t)
    row = of_run(run_row) + j * CHUNK_ROWS
    spare = n_tiles * REGION + (c % ch) * CHUNK_ROWS
    src = jnp.where(valid, row, spare + 2 * bm).astype(jnp.int32).reshape(n_blk + 1, 1, ch)
    dst = jnp.where(valid, row, spare + ((c // ch) % 2) * bm).astype(jnp.int32).reshape(n_blk + 1, 1, ch)
    dst = jnp.concatenate([dst[n_blk:], dst[:n_blk]])
    blk_e = blk_e[:n_blk].astype(jnp.int32)
    n_used = (pend[-1:] // bm).astype(jnp.int32)
    plan = jnp.concatenate([src[1:], dst[:n_blk], dst[1:]], axis=-1)
    return src[:1], plan, blk_e, n_used


def _ffn_kernel(blk_e_ref, n_used_ref, src0_ref, plan_ref, xs_hbm, wgu_ref, bgu_ref, wd_ref,
                bd_ref, ys_hbm, xbuf, ybuf, wgu_b, wd_b, in_sem, out_sem, *, bm):
    i = pl.program_id(0)
    n_used = n_used_ref[0]
    slot = i % 2
    ch = bm // CHUNK_ROWS

    def gather(tbl_ref, col0, s):
        for j in range(ch):
            row = pl.multiple_of(tbl_ref[0, col0 + j], CHUNK_ROWS)
            pltpu.make_async_copy(xs_hbm.at[pl.ds(row, CHUNK_ROWS)],
                                  xbuf.at[s, pl.ds(j * CHUNK_ROWS, CHUNK_ROWS)], in_sem.at[s]).start()

    def scatter(tbl_ref, col0, s):
        for j in range(ch):
            row = pl.multiple_of(tbl_ref[0, col0 + j], CHUNK_ROWS)
            pltpu.make_async_copy(ybuf.at[s, pl.ds(j * CHUNK_ROWS, CHUNK_ROWS)],
                                  ys_hbm.at[pl.ds(row, CHUNK_ROWS)], out_sem.at[s]).start()

    def wait_in(s):
        pltpu.make_async_copy(xs_hbm.at[pl.ds(0, bm)], xbuf.at[s], in_sem.at[s]).wait()

    def wait_out(s):
        pltpu.make_async_copy(ybuf.at[s], ys_hbm.at[pl.ds(0, bm)], out_sem.at[s]).wait()

    used = i < n_used
    e_changed = jnp.logical_or(i == 0, blk_e_ref[i] != blk_e_ref[jnp.maximum(i - 1, 0)])

    @pl.when(jnp.logical_and(i == 0, used))
    def _():
        gather(src0_ref, 0, 0)
        ybuf[1] = jnp.zeros((bm, D_MODEL), BF16)

    @pl.when(jnp.logical_and(used, e_changed))
    def _():
        wgu_b[...] = wgu_ref[...].astype(BF16)
        wd_b[...] = wd_ref[...].astype(BF16)

    @pl.when(jnp.logical_and(used, i >= 1))
    def _():
        wait_out(slot)

    @pl.when(used)
    def _():
        wait_in(slot)
        gu = jnp.dot(xbuf[slot], wgu_b[...], preferred_element_type=F32) + bgu_ref[...]
        gather(plan_ref, 0, 1 - slot)
        scatter(plan_ref, ch, 1 - slot)
        gate = jnp.minimum(gu[:, :D_MODEL], SWIGLU_LIMIT)
        up = jnp.clip(gu[:, D_MODEL:], -SWIGLU_LIMIT, SWIGLU_LIMIT)
        h = (up + 1.0) * gate * _sigmoid(SWIGLU_ALPHA * gate)
        y = jnp.dot(h.astype(BF16), wd_b[...], preferred_element_type=F32) + bd_ref[...]
        ybuf[slot] = y.astype(BF16)

    @pl.when(i == n_used - 1)
    def _():
        scatter(plan_ref, 2 * ch, slot)
        wait_in(1 - slot)
        wait_out(1 - slot)
        wait_out(slot)


def _ffn(xs, src0, plan, blk_e, n_used, p, layer, bm):
    n_blk = blk_e.shape[0]
    ch = src0.shape[-1]
    smem = functools.partial(pl.BlockSpec, memory_space=pltpu.SMEM)
    grid_spec = pltpu.PrefetchScalarGridSpec(
        num_scalar_prefetch=2,
        grid=(n_blk,),
        in_specs=[smem((None, 1, ch), lambda i, be, nu: (0, 0, 0)),
                  smem((None, 1, 3 * ch), lambda i, be, nu: (i, 0, 0)),
                  pl.BlockSpec(memory_space=pl.ANY),
                  pl.BlockSpec((None, None, D_MODEL, 2 * D_MODEL), lambda i, be, nu: (layer, be[i], 0, 0)),
                  pl.BlockSpec((None, None, 1, 2 * D_MODEL), lambda i, be, nu: (layer, be[i], 0, 0)),
                  pl.BlockSpec((None, None, D_MODEL, D_MODEL), lambda i, be, nu: (layer, be[i], 0, 0)),
                  pl.BlockSpec((None, None, 1, D_MODEL), lambda i, be, nu: (layer, be[i], 0, 0))],
        out_specs=pl.BlockSpec(memory_space=pl.ANY),
        scratch_shapes=[pltpu.VMEM((2, bm, D_MODEL), BF16), pltpu.VMEM((2, bm, D_MODEL), BF16),
                        pltpu.VMEM((D_MODEL, 2 * D_MODEL), BF16), pltpu.VMEM((D_MODEL, D_MODEL), BF16),
                        pltpu.SemaphoreType.DMA((2,)), pltpu.SemaphoreType.DMA((2,))],
    )
    return pl.pallas_call(
        functools.partial(_ffn_kernel, bm=bm),
        grid_spec=grid_spec,
        out_shape=jax.ShapeDtypeStruct(xs.shape, BF16),
        input_output_aliases={4: 0},
        compiler_params=pltpu.CompilerParams(
            dimension_semantics=("arbitrary",), vmem_limit_bytes=VMEM_LIMIT),
        name="ffn",
    )(blk_e, n_used, src0, plan, xs, p["w_gu"], p["b_gu"], p["w_down"], p["b_down"])


G_ROWS = 128


def _combine_kernel(ys_ref, pos_ref, gate_ref, x_ref, g_ref, b_ref, out0_ref, out1_ref, g_scr, *x_scr,
                    dn_alpha, first_tiles):
    nr = G_ROWS
    lane = lax.broadcasted_iota(jnp.int32, (nr, LANES), 1)
    for r0 in range(0, TS, nr):
        rows = slice(r0, r0 + nr)
        tile_of, gate = [], []
        for k in range(TOP_K):
            rel = jnp.broadcast_to(pos_ref[rows, k:k + 1], (nr, LANES)) - lane
            hit = jnp.where((rel & (LANES - 1)) == 0, rel >> (LANES.bit_length() - 1), -1)
            tile_of.append(hit.astype(F32).astype(BF16))
            gate.append(jnp.broadcast_to(gate_ref[rows, k:k + 1], (nr, LANES)).astype(BF16))
        for c in range(REGION // LANES):
            g = jnp.zeros((nr, LANES), BF16)
            for k in range(TOP_K):
                g = jnp.where(tile_of[k] == float(c), gate[k], g)
            g_scr[rows, c * LANES:(c + 1) * LANES] = g
        y = jnp.dot(g_scr[rows, :], ys_ref[...], preferred_element_type=F32)
        x2 = _layer_norm(dn_alpha * x_ref[rows, :] + y, g_ref[...], b_ref[...])
        if first_tiles is None:
            out0_ref[rows, :] = x2
            out1_ref[rows, :] = x2.astype(BF16)
        else:
            x_scr[0][rows, :] = x2
    if first_tiles is not None:
        @pl.when(pl.program_id(0) < first_tiles)
        def _():
            out0_ref[...] = x_scr[0][...]

        @pl.when(pl.program_id(0) >= first_tiles)
        def _():
            out1_ref[...] = x_scr[0][...]


def _combine(ys, pos, gates, x1, p, layer, dn_alpha, first_tokens=None):
    t = x1.shape[0]
    if first_tokens is None:
        first_tiles = None
        out_specs = [pl.BlockSpec((TS, D_MODEL), lambda i: (i, 0)), pl.BlockSpec((TS, D_MODEL), lambda i: (i, 0))]
        out_shape = [jax.ShapeDtypeStruct((t, D_MODEL), F32), jax.ShapeDtypeStruct((t, D_MODEL), BF16)]
        scratch = []
    else:
        first_tiles = first_tokens // TS
        out_specs = [pl.BlockSpec((TS, D_MODEL), lambda i: (jnp.minimum(i, first_tiles - 1), 0)),
                     pl.BlockSpec((TS, D_MODEL), lambda i: (jnp.maximum(i - first_tiles, 0), 0))]
        out_shape = [jax.ShapeDtypeStruct((first_tokens, D_MODEL), F32),
                     jax.ShapeDtypeStruct((t - first_tokens, D_MODEL), F32)]
        scratch = [pltpu.VMEM((TS, D_MODEL), F32)]
    return pl.pallas_call(
        functools.partial(_combine_kernel, dn_alpha=np.float32(dn_alpha), first_tiles=first_tiles),
        grid=(t // TS,),
        in_specs=[pl.BlockSpec((REGION, D_MODEL), lambda i: (i, 0)),
                  pl.BlockSpec((TS, LANES), lambda i: (i, 0)),
                  pl.BlockSpec((TS, LANES), lambda i: (i, 0)),
                  pl.BlockSpec((TS, D_MODEL), lambda i: (i, 0)),
                  pl.BlockSpec((None, 1, D_MODEL), lambda i: (layer, 0, 0)),
                  pl.BlockSpec((None, 1, D_MODEL), lambda i: (layer, 0, 0))],
        out_specs=out_specs,
        out_shape=out_shape,
        scratch_shapes=[pltpu.VMEM((TS, REGION), BF16)] + scratch,
        compiler_params=pltpu.CompilerParams(
            dimension_semantics=("arbitrary",), vmem_limit_bytes=VMEM_LIMIT),
        name="combine",
    )(ys, pos, gates, x1, p["ln2_g"], p["ln2_b"])


def _prepare_params(w_in, b_in, ln_v_g, ln_v_b, w_s, b_s, w_pa, w_pb, w_o, ln1_g, ln1_b,
                    w_r, b_r, w_gu, b_gu, w_down, b_down, ln2_g, ln2_b):
    depth = w_in.shape[0]
    n_qkv = 3 * len(B_PATTERNS) * B_GROUP_WIDTH
    qkv0, qkv1 = 2 * D_MODEL, 2 * D_MODEL + n_qkv

    def permute_cols(a):
        gap = jnp.zeros(a.shape[:-1] + (MAIN_COL0 - n_qkv,), a.dtype)
        return jnp.concatenate([a[..., qkv0:qkv1], gap, a[..., :qkv0], a[..., qkv1:]], axis=-1)

    def vec(a):
        return a.reshape(depth, 1, a.shape[-1]).astype(F32)

    w_r_pad = jnp.pad(w_r, ((0, 0), (0, 0), (0, LANES - N_EXPERTS))).astype(BF16)
    b_r_pad = jnp.pad(b_r.astype(F32), ((0, 0), (0, LANES - N_EXPERTS)), constant_values=NEG_INF)
    return dict(
        w_in=permute_cols(w_in).astype(BF16), b_in=vec(permute_cols(b_in)),
        ln_v_g=vec(ln_v_g), ln_v_b=vec(ln_v_b),
        w_s=w_s.astype(BF16),
        b_s=jnp.broadcast_to(b_s.astype(F32)[..., None], b_s.shape + (CHUNK,)),
        w_pa=w_pa.astype(BF16), w_pb=w_pb.astype(BF16), w_o=w_o.astype(BF16),
        ln1_g=vec(ln1_g), ln1_b=vec(ln1_b),
        w_r=w_r_pad, b_r=vec(b_r_pad),
        w_gu=w_gu, b_gu=b_gu.astype(F32)[:, :, None, :],
        w_down=w_down, b_down=b_down.astype(F32)[:, :, None, :],
        ln2_g=vec(ln2_g), ln2_b=vec(ln2_b),
    )


def _bf16_rows_kernel(a_ref, b_ref, o_ref, *, first_tiles):
    o_ref[...] = jnp.where(pl.program_id(0) < first_tiles, a_ref[...], b_ref[...]).astype(BF16)


def _bf16_rows(a, b, tm=1024):
    first_tiles = a.shape[0] // tm
    t = a.shape[0] + b.shape[0]
    return pl.pallas_call(
        functools.partial(_bf16_rows_kernel, first_tiles=first_tiles),
        grid=(t // tm,),
        in_specs=[pl.BlockSpec((tm, D_MODEL), lambda i: (jnp.minimum(i, first_tiles - 1), 0)),
                  pl.BlockSpec((tm, D_MODEL), lambda i: (jnp.maximum(i - first_tiles, 0), 0))],
        out_specs=pl.BlockSpec((tm, D_MODEL), lambda i: (i, 0)),
        out_shape=jax.ShapeDtypeStruct((t, D_MODEL), BF16),
        compiler_params=pltpu.CompilerParams(
            dimension_semantics=("parallel",), vmem_limit_bytes=VMEM_LIMIT),
        name="bf16_rows",
    )(a, b)


def _trunk(x_first, x_rest, params, bm=512):
    n_first, seq, _ = x_first.shape
    n_seq = n_first + x_rest.shape[0]
    depth = params["w_in"].shape[0]
    dn_alpha = (2.0 * depth) ** 0.25
    x = (x_first.reshape(-1, D_MODEL), x_rest.reshape(-1, D_MODEL))
    xb = _bf16_rows(*x)
    for layer in range(depth):
        z = _inproj_main(xb, params["w_in"], params["b_in"], layer)
        attn = [_attention(_inproj_qkv(xb, params["w_in"], params["b_in"], layer, g, n_seq, seq), g)
                for g in range(len(B_PATTERNS))]
        x1, x1b, logits = _mixer(z, attn, x, params, layer, dn_alpha, seq)
        xs, pos, gates, cnt = _dispatch(logits, x1b)
        src0, plan, blk_e, n_used = _chunk_plan(cnt[:, 0, :N_EXPERTS], bm)
        ys = _ffn(xs, src0, plan, blk_e, n_used, params, layer, bm)
        last = layer == depth - 1
        x, xb = _combine(ys, pos, gates, x1, params, layer, dn_alpha, n_first * seq if last else None)
    return x.reshape(n_first, seq, D_MODEL), xb.reshape(n_seq - n_first, seq, D_MODEL)


def kernel(x_prompt, x_sample, w_in, b_in, ln_v_g, ln_v_b, w_s, b_s, w_pa, w_pb, w_o, ln1_g, ln1_b,
           w_r, b_r, w_gu, b_gu, w_down, b_down, ln2_g, ln2_b):
    params = _prepare_params(w_in, b_in, ln_v_g, ln_v_b, w_s, b_s, w_pa, w_pb, w_o, ln1_g, ln1_b,
                             w_r, b_r, w_gu, b_gu, w_down, b_down, ln2_g, ln2_b)
    return _trunk(x_prompt, x_sample, params)
```

```python
import functools

import numpy as np
import jax
import jax.numpy as jnp
from jax import lax
from jax.experimental import pallas as pl
from jax.experimental.pallas import tpu as pltpu

D_MODEL = 1024
CHUNK = 128
A_GROUPS = 8
B_PATTERNS = ((128, 1), (512, 4), (2048, 16))
B_HEADS = 4
B_HEAD_DIM = 128
B_GROUP_WIDTH = B_HEADS * B_HEAD_DIM
N_ATTN_HEADS = len(B_PATTERNS) * B_HEADS
N_SIDE = 64
D_IN = 4 * D_MODEL + 3 * len(B_PATTERNS) * B_GROUP_WIDTH
N_EXPERTS = 32
TOP_K = 4
SWIGLU_LIMIT = 7.0
SWIGLU_ALPHA = 1.702
LN_EPS = 1e-5
NEG_INF = -1e30
LANES = 128

MAIN_COLS = 4 * D_MODEL
QKV_COLS = 3 * B_GROUP_WIDTH
MAIN_COL0 = -(-len(B_PATTERNS) * QKV_COLS // D_MODEL) * D_MODEL
GELU_COLS = 2 * D_MODEL

VMEM_LIMIT = 56 * 1024 * 1024

BF16 = jnp.bfloat16
F32 = jnp.float32


def _gelu(x):
    return 0.5 * x * (1.0 + lax.erf(x * np.float32(np.sqrt(0.5))))


def _sigmoid(x):
    return 0.5 * jnp.tanh(0.5 * x) + 0.5


def _layer_norm(x, g, b):
    xc = x - jnp.mean(x, axis=-1, keepdims=True)
    var = jnp.mean(xc * xc, axis=-1, keepdims=True)
    return xc * lax.rsqrt(var + LN_EPS) * g + b


N_CHUNK = 256
MAX_ROW_STRIDE = 4


def _inproj_main_kernel(x_ref, w_ref, b_ref, z_ref, *, n_gelu_tiles):
    j = pl.program_id(1)

    def run(act):
        for c in range(z_ref.shape[1] // N_CHUNK):
            cols = slice(c * N_CHUNK, (c + 1) * N_CHUNK)
            acc = jnp.dot(x_ref[...], w_ref[:, cols], preferred_element_type=F32) + b_ref[:, cols]
            z_ref[:, cols] = act(acc).astype(BF16)

    @pl.when(j < n_gelu_tiles)
    def _():
        run(_gelu)

    @pl.when(j >= n_gelu_tiles)
    def _():
        run(lambda a: a)


def _inproj_main(xb, w_in, b_in, layer, tm=2048, tn=1024):
    t = xb.shape[0]
    return pl.pallas_call(
        functools.partial(_inproj_main_kernel, n_gelu_tiles=GELU_COLS // tn),
        grid=(t // tm, MAIN_COLS // tn),
        in_specs=[
            pl.BlockSpec((tm, D_MODEL), lambda i, j: (i, 0)),
            pl.BlockSpec((None, D_MODEL, tn), lambda i, j: (layer, 0, MAIN_COL0 // tn + j)),
            pl.BlockSpec((None, 1, tn), lambda i, j: (layer, 0, MAIN_COL0 // tn + j)),
        ],
        out_specs=pl.BlockSpec((tm, tn), lambda i, j: (i, j)),
        out_shape=jax.ShapeDtypeStruct((t, MAIN_COLS), BF16),
        compiler_params=pltpu.CompilerParams(
            dimension_semantics=("parallel", "arbitrary"), vmem_limit_bytes=VMEM_LIMIT),
        name="inproj_main",
    )(xb, w_in, b_in)


def _inproj_qkv_kernel(x_ref, w_ref, b_ref, o_ref, *scratch, dilation):
    rows = x_ref.shape[0] // dilation
    for c in range(o_ref.shape[-1] // N_CHUNK):
        cols = slice(c * N_CHUNK, (c + 1) * N_CHUNK)
        acc = jnp.dot(x_ref[...], w_ref[:, cols], preferred_element_type=F32) + b_ref[:, cols]
        if dilation == 1:
            o_ref[0, :, cols] = acc.astype(BF16)
            continue
        scr, scr2 = scratch[0], scratch[-1]
        d1 = min(dilation, MAX_ROW_STRIDE)
        d2 = dilation // d1
        for q in range(c * N_CHUNK // LANES, (c + 1) * N_CHUNK // LANES):
            scr[q] = acc[:, q * LANES - c * N_CHUNK:(q + 1) * LANES - c * N_CHUNK]
            for r1 in range(d1):
                part = scr[q, pl.ds(r1, rows * d2, stride=d1), :]
                if d2 == 1:
                    o_ref[r1, :, q * LANES:(q + 1) * LANES] = part.astype(BF16)
                    continue
                scr2[q, r1 * rows * d2:(r1 + 1) * rows * d2, :] = part
                for r2 in range(d2):
                    o_ref[r2 * d1 + r1, :, q * LANES:(q + 1) * LANES] = (
                        scr2[q, pl.ds(r1 * rows * d2 + r2, rows, stride=d2), :].astype(BF16))


def _inproj_qkv(xb, w_in, b_in, layer, group, n_seq, seq, tn=QKV_COLS):
    _, dilation = B_PATTERNS[group]
    tm = 2048 if dilation <= MAX_ROW_STRIDE else 1024
    tiles_per_seq = seq // tm
    col0 = group * QKV_COLS // tn
    return pl.pallas_call(
        functools.partial(_inproj_qkv_kernel, dilation=dilation),
        grid=(n_seq * tiles_per_seq, QKV_COLS // tn),
        in_specs=[
            pl.BlockSpec((tm, D_MODEL), lambda i, j: (i, 0)),
            pl.BlockSpec((None, D_MODEL, tn), lambda i, j: (layer, 0, col0 + j)),
            pl.BlockSpec((None, 1, tn), lambda i, j: (layer, 0, col0 + j)),
        ],
        out_specs=pl.BlockSpec((None, dilation, tm // dilation, tn),
                               lambda i, j: (i // tiles_per_seq, 0, i % tiles_per_seq, j)),
        out_shape=jax.ShapeDtypeStruct((n_seq, dilation, seq // dilation, QKV_COLS), BF16),
        scratch_shapes=[pltpu.VMEM((tn // LANES, tm, LANES), F32)] * (
            0 if dilation == 1 else 1 if dilation <= MAX_ROW_STRIDE else 2),
        compiler_params=pltpu.CompilerParams(
            dimension_semantics=("parallel", "arbitrary"), vmem_limit_bytes=VMEM_LIMIT),
        name=f"inproj_qkv{group}",
    )(xb, w_in, b_in)


def _attn_kernel(cur_ref, prev_ref, next_ref, bias_ref, o_ref, lse_ref, *scratch, bq):
    for rb in range(cur_ref.shape[0]):
        _attn_rows(cur_ref.at[rb], prev_ref.at[rb], next_ref.at[rb], bias_ref, o_ref.at[rb], lse_ref.at[rb],
                   scratch, bq)


def _attn_rows(cur_ref, prev_ref, next_ref, bias_ref, o_ref, lse_ref, scratch, bq):
    s_scr, p_scr = scratch[:B_HEADS], scratch[B_HEADS:]
    jq = pl.program_id(2)
    nq = pl.num_programs(2)
    n_piece = bq // N_SIDE
    col = lax.broadcasted_iota(jnp.int32, (N_SIDE, 3 * N_SIDE), 1)
    lane = lax.broadcasted_iota(jnp.int32, (N_SIDE, LANES), 1)
    scale = np.float32(B_HEAD_DIM ** -0.5)

    def window(p, part, h):
        c0 = part * B_GROUP_WIDTH + h * B_HEAD_DIM
        cols = slice(c0, c0 + B_HEAD_DIM)
        if n_piece == 1:
            return jnp.concatenate([prev_ref[:, cols], cur_ref[:, cols], next_ref[:, cols]], axis=0)
        if p == 0:
            return jnp.concatenate([prev_ref[:, cols], cur_ref[0:2 * N_SIDE, cols]], axis=0)
        if p == n_piece - 1:
            return jnp.concatenate([cur_ref[bq - 2 * N_SIDE:bq, cols], next_ref[:, cols]], axis=0)
        return cur_ref[(p - 1) * N_SIDE:(p + 2) * N_SIDE, cols]

    def scores(p, h):
        qh = cur_ref[p * N_SIDE:(p + 1) * N_SIDE, h * B_HEAD_DIM:(h + 1) * B_HEAD_DIM]
        s = lax.dot_general(qh, window(p, 1, h), (((1,), (1,)), ((), ())), preferred_element_type=F32) * scale
        if p == 0:
            s = jnp.where(jnp.logical_and(col < N_SIDE, jq == 0), NEG_INF, s)
        if p == n_piece - 1:
            s = jnp.where(jnp.logical_and(col >= 2 * N_SIDE, jq == nq - 1), NEG_INF, s)
        return s + bias_ref[h]

    def softmax(s):
        m = jnp.max(s, axis=-1, keepdims=True)
        e = jnp.exp(s - m)
        den = jnp.sum(e, axis=-1, keepdims=True)
        return (e / den).astype(BF16), m + jnp.log(den)

    def values(p, h, pn):
        o = jnp.dot(pn, window(p, 2, h), preferred_element_type=F32)
        o_ref[p * N_SIDE:(p + 1) * N_SIDE, h * B_HEAD_DIM:(h + 1) * B_HEAD_DIM] = o.astype(BF16)

    lse = {}
    for t in range(B_HEADS + 2):
        if t < B_HEADS:
            for p in range(n_piece):
                s_scr[t][p * N_SIDE:(p + 1) * N_SIDE, :] = scores(p, t)
        if 1 <= t <= B_HEADS:
            for p in range(n_piece):
                rows = slice(p * N_SIDE, (p + 1) * N_SIDE)
                p_scr[t - 1][rows, :], lse[p, t - 1] = softmax(s_scr[t - 1][rows, :])
        if t >= 2:
            for p in range(n_piece):
                values(p, t - 2, p_scr[t - 2][p * N_SIDE:(p + 1) * N_SIDE, :])
    for p in range(n_piece):
        tile = jnp.zeros((N_SIDE, LANES), F32)
        for h in range(B_HEADS):
            tile = jnp.where(lane == h, lse[p, h], tile)
        lse_ref[p * N_SIDE:(p + 1) * N_SIDE, :] = tile


def _attn_bias(group):
    _, dilation = B_PATTERNS[group]
    slopes = np.array([2.0 ** (-8.0 * (group * B_HEADS + h + 1) / N_ATTN_HEADS) for h in range(B_HEADS)],
                      dtype=np.float32)
    a = np.arange(N_SIDE)[:, None]
    c = np.arange(3 * N_SIDE)[None, :]
    rel = c - N_SIDE - a
    alibi = (-(slopes[:, None, None] * (dilation * np.abs(rel))[None])).astype(np.float32)
    return np.where((np.abs(rel) <= N_SIDE)[None], alibi, np.float32(NEG_INF)).astype(np.float32)


def _attention(qkv, group, rows_per_step=1024):
    n_seq, dilation, sub_len, _ = qkv.shape
    bq = min(rows_per_step, sub_len)
    nq = sub_len // bq
    rb = min(dilation, rows_per_step // bq)
    halo_per_bq = bq // N_SIDE
    n_halo = sub_len // N_SIDE
    return pl.pallas_call(
        functools.partial(_attn_kernel, bq=bq),
        grid=(n_seq, dilation // rb, nq),
        in_specs=[pl.BlockSpec((None, rb, bq, QKV_COLS), lambda b, r, j: (b, r, j, 0)),
                  pl.BlockSpec((None, rb, N_SIDE, QKV_COLS),
                               lambda b, r, j: (b, r, jnp.maximum(j * halo_per_bq - 1, 0), 0)),
                  pl.BlockSpec((None, rb, N_SIDE, QKV_COLS),
                               lambda b, r, j: (b, r, jnp.minimum((j + 1) * halo_per_bq, n_halo - 1), 0)),
                  pl.BlockSpec((B_HEADS, N_SIDE, 3 * N_SIDE), lambda b, r, j: (0, 0, 0))],
        out_specs=[pl.BlockSpec((None, rb, bq, B_GROUP_WIDTH), lambda b, r, j: (b, r, j, 0)),
                   pl.BlockSpec((None, rb, bq, LANES), lambda b, r, j: (b, r, j, 0))],
        out_shape=[jax.ShapeDtypeStruct((n_seq, dilation, sub_len, B_GROUP_WIDTH), BF16),
                   jax.ShapeDtypeStruct((n_seq, dilation, sub_len, LANES), F32)],
        scratch_shapes=([pltpu.VMEM((bq, 3 * N_SIDE), F32)] * B_HEADS
                        + [pltpu.VMEM((bq, 3 * N_SIDE), BF16)] * B_HEADS),
        compiler_params=pltpu.CompilerParams(
            dimension_semantics=("parallel", "parallel", "arbitrary"), vmem_limit_bytes=VMEM_LIMIT),
        name=f"attn{group}",
    )(qkv, qkv, qkv, jnp.asarray(_attn_bias(group)))


def _mixer_kernel(u_ref, v_ref, ga_ref, gb_ref, o0_ref, o1_ref, o2_ref, l0_ref, l1_ref, l2_ref, x_ref, xrest_ref,
                  lnvg_ref, lnvb_ref, ws_ref, bs_ref, wpa_ref, wpb_ref, wo_ref, ln1g_ref, ln1b_ref,
                  wr_ref, br_ref, x1_ref, x1b_ref, logit_ref, a_scr, b_scr, o_scr, l_scr, t_scr,
                  *, tm, dn_alpha, first_tiles):
    for g, (o_ref, l_ref) in enumerate(((o0_ref, l0_ref), (o1_ref, l1_ref), (o2_ref, l2_ref))):
        dilation = o_ref.shape[0]
        rows = tm // dilation
        d1 = min(dilation, MAX_ROW_STRIDE)
        d2 = dilation // d1

        def pieces(r):
            return [o_ref[r, :, h * B_HEAD_DIM:(h + 1) * B_HEAD_DIM].astype(F32) for h in range(B_HEADS)] + [l_ref[r]]

        def place(idx, vals):
            for h in range(B_HEADS):
                o_scr[g, h, idx, :] = vals[h]
            l_scr[g, idx, :] = vals[B_HEADS]

        for r1 in range(d1):
            idx = pl.ds(r1, rows * d2, stride=d1) if d1 > 1 else slice(None)
            if d2 == 1:
                place(idx, pieces(r1))
                continue
            for r2 in range(d2):
                for j, val in enumerate(pieces(r2 * d1 + r1)):
                    t_scr[r1, j, pl.ds(r2, rows, stride=d2), :] = val
            place(idx, [t_scr[r1, j] for j in range(B_HEADS + 1)])
    n_groups = len(B_PATTERNS)

    def branches(r0, nr):
        rows = slice(r0, r0 + nr)
        vn = _layer_norm(v_ref[rows, :].astype(F32), lnvg_ref[...], lnvb_ref[...]).astype(BF16)
        for g in range(A_GROUPS):
            c0 = g * CHUNK
            chunks = jnp.concatenate([vn[q0:q0 + CHUNK, c0:c0 + CHUNK] for q0 in range(0, nr, CHUNK)], axis=1)
            mixed_all = jnp.dot(ws_ref[g], chunks, preferred_element_type=F32)
            for c in range(nr // CHUNK):
                q0 = c * CHUNK
                mixed = mixed_all[:, q0:q0 + CHUNK] + bs_ref[g]
                a = u_ref[r0 + q0:r0 + q0 + CHUNK, c0:c0 + CHUNK].astype(F32) * mixed
                a_scr[r0 + q0:r0 + q0 + CHUNK, c0:c0 + CHUNK] = a.astype(BF16)
        ls = [l_scr[g, rows, :] for g in range(n_groups)]
        mx = jnp.maximum(jnp.maximum(ls[0], ls[1]), ls[2])
        es = [jnp.exp(l - mx) for l in ls]
        den = es[0] + es[1] + es[2]
        wts = [e / den for e in es]
        for h in range(B_HEADS):
            c0 = h * B_HEAD_DIM
            acc = jnp.zeros((nr, B_HEAD_DIM), F32)
            for g in range(n_groups):
                acc = acc + wts[g][:, h:h + 1] * o_scr[g, h, rows, :]
            b_scr[rows, c0:c0 + B_HEAD_DIM] = acc.astype(BF16)

    def project(r0, nr):
        rows = slice(r0, r0 + nr)
        pa = jnp.dot(a_scr[rows, :], wpa_ref[...], preferred_element_type=F32)
        pb = jnp.dot(b_scr[rows, :], wpb_ref[...], preferred_element_type=F32)
        merged = _sigmoid(ga_ref[rows, :].astype(F32)) * pa + _sigmoid(gb_ref[rows, :].astype(F32)) * pb
        mix = jnp.dot(merged.astype(BF16), wo_ref[...], preferred_element_type=F32)
        x = x_ref[rows, :]
        if first_tiles is not None:
            x = jnp.where(pl.program_id(0) < first_tiles, x, xrest_ref[rows, :])
        x1 = _layer_norm(dn_alpha * x + mix, ln1g_ref[...], ln1b_ref[...])
        x1_ref[rows, :] = x1
        x1b = x1.astype(BF16)
        x1b_ref[rows, :] = x1b
        logit_ref[rows, :] = jnp.dot(x1b, wr_ref[...], preferred_element_type=F32) + br_ref[...]

    half = tm // 2
    branches(0, half)
    project(0, half)
    branches(half, half)
    project(half, half)


def _mixer(z, attn, x, p, layer, dn_alpha, seq, tm=512):
    xa, xb = x if isinstance(x, tuple) else (x, x)
    first_tiles = xa.shape[0] // tm
    t = z.shape[0]
    tiles_per_seq = seq // tm
    if isinstance(x, tuple):
        x_specs = [pl.BlockSpec((tm, D_MODEL), lambda i: (jnp.minimum(i, first_tiles - 1), 0)),
                   pl.BlockSpec((tm, D_MODEL), lambda i: (jnp.maximum(i - first_tiles, 0), 0))]
    else:
        x_specs = [pl.BlockSpec((tm, D_MODEL), lambda i: (i, 0)), pl.BlockSpec((tm, D_MODEL), lambda i: (0, 0))]

    def zspec(cb):
        return pl.BlockSpec((tm, D_MODEL), lambda i: (i, cb))

    def row(width):
        return pl.BlockSpec((tm, width), lambda i: (i, 0))

    def split(a):
        _, dilation, _, width = a.shape
        return pl.BlockSpec((None, dilation, tm // dilation, width),
                            lambda i: (i // tiles_per_seq, 0, i % tiles_per_seq, 0))

    def vec(width):
        return pl.BlockSpec((None, 1, width), lambda i: (layer, 0, 0))

    def mat(k, n):
        return pl.BlockSpec((None, k, n), lambda i: (layer, 0, 0))

    (o0, l0), (o1, l1), (o2, l2) = attn
    return pl.pallas_call(
        functools.partial(_mixer_kernel, tm=tm, dn_alpha=np.float32(dn_alpha),
                          first_tiles=first_tiles if isinstance(x, tuple) else None),
        grid=(t // tm,),
        in_specs=[zspec(0), zspec(1), zspec(2), zspec(3),
                  split(o0), split(o1), split(o2), split(l0), split(l1), split(l2), *x_specs,
                  vec(D_MODEL), vec(D_MODEL),
                  pl.BlockSpec((None, A_GROUPS, CHUNK, CHUNK), lambda i: (layer, 0, 0, 0)),
                  pl.BlockSpec((None, A_GROUPS, CHUNK, CHUNK), lambda i: (layer, 0, 0, 0)),
                  mat(D_MODEL, D_MODEL), mat(B_GROUP_WIDTH, D_MODEL), mat(D_MODEL, D_MODEL),
                  vec(D_MODEL), vec(D_MODEL), mat(D_MODEL, LANES), vec(LANES)],
        out_specs=[row(D_MODEL), row(D_MODEL), row(LANES)],
        out_shape=[jax.ShapeDtypeStruct((t, D_MODEL), F32), jax.ShapeDtypeStruct((t, D_MODEL), BF16),
                   jax.ShapeDtypeStruct((t, LANES), F32)],
        scratch_shapes=[pltpu.VMEM((tm, D_MODEL), BF16), pltpu.VMEM((tm, B_GROUP_WIDTH), BF16),
                        pltpu.VMEM((len(B_PATTERNS), B_HEADS, tm, B_HEAD_DIM), F32),
                        pltpu.VMEM((len(B_PATTERNS), tm, LANES), F32),
                        pltpu.VMEM((MAX_ROW_STRIDE, B_HEADS + 1, tm // MAX_ROW_STRIDE, LANES), F32)],
        compiler_params=pltpu.CompilerParams(
            dimension_semantics=("parallel",), vmem_limit_bytes=VMEM_LIMIT),
        name="mixer",
    )(z, z, z, z, o0, o1, o2, l0, l1, l2, xa, xb,
      p["ln_v_g"], p["ln_v_b"], p["w_s"], p["b_s"], p["w_pa"], p["w_pb"], p["w_o"],
      p["ln1_g"], p["ln1_b"], p["w_r"], p["b_r"])


TS = 512
CHUNK_ROWS = 8
REGION = TS * TOP_K + N_EXPERTS * CHUNK_ROWS
P_ROWS = 256


def _chunks_per_expert(cnt):
    return jnp.floor((cnt + (CHUNK_ROWS - 1)) * (1.0 / CHUNK_ROWS))


def _dispatch_kernel(logit_ref, x_ref, upper_ref, lower_ref, xs_ref, pos_ref, gate_ref, cnt_ref):
    l = jnp.transpose(logit_ref[...])[0:N_EXPERTS, :]
    sub = lax.broadcasted_iota(jnp.int32, (N_EXPERTS, TS), 0)
    onehots, vals = [], []
    for _ in range(TOP_K):
        mx = jnp.max(l, axis=0, keepdims=True)
        idx = jnp.min(jnp.where(l == mx, sub, N_EXPERTS), axis=0, keepdims=True)
        oh = sub == idx
        l = jnp.where(oh, -jnp.inf, l)
        onehots.append(oh)
        vals.append(mx)
    es = [jnp.exp(v - vals[0]) for v in vals]
    den = es[0] + es[1] + es[2] + es[3]
    multi = jnp.zeros((N_EXPERTS, TS), F32)
    for oh in onehots:
        multi = multi + oh.astype(F32)
    multi_b = jnp.concatenate([multi, jnp.zeros((LANES - N_EXPERTS, TS), F32)], axis=0).astype(BF16)
    cnt_row = lax.dot_general(jnp.ones((8, TS), BF16), multi_b, (((1,), (1,)), ((), ())),
                              preferred_element_type=F32)
    cnt_ref[...] = cnt_row.astype(jnp.int32)
    cnt_col = jnp.concatenate([jnp.sum(multi, axis=1, keepdims=True),
                               jnp.zeros((LANES - N_EXPERTS, 1), F32)], axis=0)
    n_chunks = jnp.broadcast_to(_chunks_per_expert(cnt_col), (LANES, LANES)).astype(BF16)
    run_off = CHUNK_ROWS * jnp.dot(lower_ref[...], n_chunks, preferred_element_type=F32)[0:N_EXPERTS, 0:1]
    rank = jnp.dot(multi_b, upper_ref[...], preferred_element_type=F32)[0:N_EXPERTS, :]
    slot = rank + run_off
    pos_rows = [jnp.sum(jnp.where(onehots[k], slot, 0.0), axis=0, keepdims=True) for k in range(TOP_K)]
    pad = jnp.full((LANES - TOP_K, TS), -1.0, F32)
    pos_all = jnp.concatenate(pos_rows + [pad], axis=0)
    pos_ref[...] = jnp.transpose(pos_all).astype(jnp.int32)
    gate_ref[...] = jnp.transpose(jnp.concatenate([e / den for e in es] + [pad], axis=0))
    live = pl.program_id(0) < pl.num_programs(0) - 1
    pos_t = jnp.where(live, pos_all[0:8, :], -1.0)
    row = lax.broadcasted_iota(jnp.int32, (P_ROWS, TS), 0).astype(F32).astype(BF16)
    for c in range(REGION // P_ROWS):
        rel = pos_t - float(c * P_ROWS)
        rel = jnp.where(jnp.logical_and(rel >= 0.0, rel < float(P_ROWS)), rel, -1.0).astype(BF16)
        perm = jnp.zeros((P_ROWS, TS), BF16)
        for k in range(TOP_K):
            perm = jnp.where(row == rel[k:k + 1, :], jnp.ones((), BF16), perm)
        xs_ref[c * P_ROWS:(c + 1) * P_ROWS, :] = jnp.dot(
            perm, x_ref[...], preferred_element_type=F32).astype(BF16)


def _dispatch(logits, x1b):
    t = logits.shape[0]
    n_tiles = t // TS
    upper = jnp.asarray(np.triu(np.ones((TS, TS), np.float32), 1), dtype=BF16)
    lower = jnp.asarray(np.tril(np.ones((LANES, LANES), np.float32), -1), dtype=BF16)

    def tile(i):
        return jnp.minimum(i, n_tiles - 1)

    return pl.pallas_call(
        _dispatch_kernel,
        grid=(n_tiles + 1,),
        in_specs=[pl.BlockSpec((TS, LANES), lambda i: (tile(i), 0)),
                  pl.BlockSpec((TS, D_MODEL), lambda i: (tile(i), 0)),
                  pl.BlockSpec((TS, TS), lambda i: (0, 0)),
                  pl.BlockSpec((LANES, LANES), lambda i: (0, 0))],
        out_specs=[pl.BlockSpec((REGION, D_MODEL), lambda i: (i, 0)),
                   pl.BlockSpec((TS, LANES), lambda i: (tile(i), 0)),
                   pl.BlockSpec((TS, LANES), lambda i: (tile(i), 0)),
                   pl.BlockSpec((None, 8, LANES), lambda i: (tile(i), 0, 0))],
        out_shape=[jax.ShapeDtypeStruct(((n_tiles + 1) * REGION, D_MODEL), BF16),
                   jax.ShapeDtypeStruct((t, LANES), jnp.int32), jax.ShapeDtypeStruct((t, LANES), F32),
                   jax.ShapeDtypeStruct((n_tiles, 8, LANES), jnp.int32)],
        compiler_params=pltpu.CompilerParams(
            dimension_semantics=("arbitrary",), vmem_limit_bytes=VMEM_LIMIT),
        name="dispatch",
    )(logits, x1b, upper, lower)


def _chunk_plan(cnt, bm):
    assert REGION >= 3 * bm
    n_tiles = cnt.shape[0]
    ch = bm // CHUNK_ROWS
    n_rows_max = n_tiles * REGION + N_EXPERTS * (bm - CHUNK_ROWS)
    n_blk = -(-n_rows_max // bm)
    run_chunks = (cnt + CHUNK_ROWS - 1) // CHUNK_ROWS
    run_off = (jnp.cumsum(run_chunks, axis=1) - run_chunks) * CHUNK_ROWS
    rows_e = jnp.sum(run_chunks, axis=0) * CHUNK_ROWS
    padded_e = (rows_e + bm - 1) // bm * bm
    pend = jnp.cumsum(padded_e)
    pstart = pend - padded_e
    runs_t = run_chunks.T
    run_start = pstart[:, None] // CHUNK_ROWS + jnp.cumsum(runs_t, axis=1) - runs_t
    run_row = jnp.arange(n_tiles, dtype=jnp.int32)[None, :] * REGION + run_off.T
    blk = jnp.arange(n_blk + 1, dtype=jnp.int32)
    blk_e = jnp.minimum(jnp.sum((pend[None, :] <= blk[:, None] * bm).astype(jnp.int32), axis=1), N_EXPERTS - 1)
    is_e = blk_e[:, None, None] == jnp.arange(N_EXPERTS, dtype=jnp.int32)[None, :, None]
    c = (blk[:, None] * ch + jnp.arange(ch, dtype=jnp.int32)[None, :]).reshape(-1)

    def of_run(v):
        v_b = jnp.sum(jnp.where(is_e, v[None], 0), axis=1)
        step = v_b - jnp.concatenate([jnp.zeros_like(v_b[:, :1]), v_b[:, :-1]], axis=1)
        return jnp.sum(jnp.where(started, step[:, None, :], 0), axis=-1).reshape(-1)

    start_b = jnp.sum(jnp.where(is_e, run_start[None], 0), axis=1)
    started = start_b[:, None, :] <= c.reshape(n_blk + 1, ch)[:, :, None]
    j = c - of_run(run_start)
    valid = j < of_run(runs_t)
    row = of_run(run_row) + j * CHUNK_ROWS
    spare = n_tiles * REGION + (c % ch) * CHUNK_ROWS
    src = jnp.where(valid, row, spare + 2 * bm).astype(jnp.int32).reshape(n_blk + 1, 1, ch)
    dst = jnp.where(valid, row, spare + ((c // ch) % 2) * bm).astype(jnp.int32).reshape(n_blk + 1, 1, ch)
    dst = jnp.concatenate([dst[n_blk:], dst[:n_blk]])
    blk_e = blk_e[:n_blk].astype(jnp.int32)
    n_used = (pend[-1:] // bm).astype(jnp.int32)
    plan = jnp.concatenate([src[1:], dst[:n_blk], dst[1:]], axis=-1)
    return src[:1], plan, blk_e, n_used


def _ffn_kernel(blk_e_ref, n_used_ref, src0_ref, plan_ref, xs_hbm, wgu_ref, bgu_ref, wd_ref,
                bd_ref, ys_hbm, xbuf, ybuf, wgu_b, wd_b, in_sem, out_sem, *, bm):
    i = pl.program_id(0)
    n_used = n_used_ref[0]
    slot = i % 2
    ch = bm // CHUNK_ROWS

    def gather(tbl_ref, col0, s):
        for j in range(ch):
            row = pl.multiple_of(tbl_ref[0, col0 + j], CHUNK_ROWS)
            pltpu.make_async_copy(xs_hbm.at[pl.ds(row, CHUNK_ROWS)],
                                  xbuf.at[s, pl.ds(j * CHUNK_ROWS, CHUNK_ROWS)], in_sem.at[s]).start()

    def scatter(tbl_ref, col0, s):
        for j in range(ch):
            row = pl.multiple_of(tbl_ref[0, col0 + j], CHUNK_ROWS)
            pltpu.make_async_copy(ybuf.at[s, pl.ds(j * CHUNK_ROWS, CHUNK_ROWS)],
                                  ys_hbm.at[pl.ds(row, CHUNK_ROWS)], out_sem.at[s]).start(priority=j % 2)

    def wait_in(s):
        pltpu.make_async_copy(xs_hbm.at[pl.ds(0, bm)], xbuf.at[s], in_sem.at[s]).wait()

    def wait_out(s):
        pltpu.make_async_copy(ybuf.at[s], ys_hbm.at[pl.ds(0, bm)], out_sem.at[s]).wait()

    used = i < n_used
    e_changed = jnp.logical_or(i == 0, blk_e_ref[i] != blk_e_ref[jnp.maximum(i - 1, 0)])

    @pl.when(jnp.logical_and(i == 0, used))
    def _():
        gather(src0_ref, 0, 0)
        ybuf[1] = jnp.zeros((bm, D_MODEL), BF16)

    @pl.when(jnp.logical_and(used, e_changed))
    def _():
        wgu_b[...] = wgu_ref[...].astype(BF16)
        wd_b[...] = wd_ref[...].astype(BF16)

    @pl.when(jnp.logical_and(used, i >= 1))
    def _():
        wait_out(slot)

    @pl.when(used)
    def _():
        wait_in(slot)
        gu = jnp.dot(xbuf[slot], wgu_b[...], preferred_element_type=F32) + bgu_ref[...]
        gather(plan_ref, 0, 1 - slot)
        scatter(plan_ref, ch, 1 - slot)
        gate = jnp.minimum(gu[:, :D_MODEL], SWIGLU_LIMIT)
        up = jnp.clip(gu[:, D_MODEL:], -SWIGLU_LIMIT, SWIGLU_LIMIT)
        h = (up + 1.0) * gate * _sigmoid(SWIGLU_ALPHA * gate)
        y = jnp.dot(h.astype(BF16), wd_b[...], preferred_element_type=F32) + bd_ref[...]
        ybuf[slot] = y.astype(BF16)

    @pl.when(i == n_used - 1)
    def _():
        scatter(plan_ref, 2 * ch, slot)
        wait_in(1 - slot)
        wait_out(1 - slot)
        wait_out(slot)


def _ffn(xs, src0, plan, blk_e, n_used, p, layer, bm):
    n_blk = blk_e.shape[0]
    ch = src0.shape[-1]
    smem = functools.partial(pl.BlockSpec, memory_space=pltpu.SMEM)
    grid_spec = pltpu.PrefetchScalarGridSpec(
        num_scalar_prefetch=2,
        grid=(n_blk,),
        in_specs=[smem((None, 1, ch), lambda i, be, nu: (0, 0, 0)),
                  smem((None, 1, 3 * ch), lambda i, be, nu: (i, 0, 0)),
                  pl.BlockSpec(memory_space=pl.ANY),
                  pl.BlockSpec((None, None, D_MODEL, 2 * D_MODEL), lambda i, be, nu: (layer, be[i], 0, 0)),
                  pl.BlockSpec((None, None, 1, 2 * D_MODEL), lambda i, be, nu: (layer, be[i], 0, 0)),
                  pl.BlockSpec((None, None, D_MODEL, D_MODEL), lambda i, be, nu: (layer, be[i], 0, 0)),
                  pl.BlockSpec((None, None, 1, D_MODEL), lambda i, be, nu: (layer, be[i], 0, 0))],
        out_specs=pl.BlockSpec(memory_space=pl.ANY),
        scratch_shapes=[pltpu.VMEM((2, bm, D_MODEL), BF16), pltpu.VMEM((2, bm, D_MODEL), BF16),
                        pltpu.VMEM((D_MODEL, 2 * D_MODEL), BF16), pltpu.VMEM((D_MODEL, D_MODEL), BF16),
                        pltpu.SemaphoreType.DMA((2,)), pltpu.SemaphoreType.DMA((2,))],
    )
    return pl.pallas_call(
        functools.partial(_ffn_kernel, bm=bm),
        grid_spec=grid_spec,
        out_shape=jax.ShapeDtypeStruct(xs.shape, BF16),
        input_output_aliases={4: 0},
        compiler_params=pltpu.CompilerParams(
            dimension_semantics=("arbitrary",), vmem_limit_bytes=VMEM_LIMIT),
        name="ffn",
    )(blk_e, n_used, src0, plan, xs, p["w_gu"], p["b_gu"], p["w_down"], p["b_down"])


G_ROWS = 128


def _combine_kernel(ys_ref, pos_ref, gate_ref, x_ref, g_ref, b_ref, out0_ref, out1_ref, g_scr, *x_scr,
                    dn_alpha, first_tiles):
    nr = G_ROWS
    lane = lax.broadcasted_iota(jnp.int32, (nr, LANES), 1)
    for r0 in range(0, TS, nr):
        rows = slice(r0, r0 + nr)
        tile_of, gate = [], []
        for k in range(TOP_K):
            rel = jnp.broadcast_to(pos_ref[rows, k:k + 1], (nr, LANES)) - lane
            hit = jnp.where((rel & (LANES - 1)) == 0, rel >> (LANES.bit_length() - 1), -1)
            tile_of.append(hit.astype(F32).astype(BF16))
            gate.append(jnp.broadcast_to(gate_ref[rows, k:k + 1], (nr, LANES)).astype(BF16))
        for c in range(REGION // LANES):
            g = jnp.zeros((nr, LANES), BF16)
            for k in range(TOP_K):
                g = jnp.where(tile_of[k] == float(c), gate[k], g)
            g_scr[rows, c * LANES:(c + 1) * LANES] = g
        y = jnp.dot(g_scr[rows, :], ys_ref[...], preferred_element_type=F32)
        x2 = _layer_norm(dn_alpha * x_ref[rows, :] + y, g_ref[...], b_ref[...])
        if first_tiles is None:
            out0_ref[rows, :] = x2
            out1_ref[rows, :] = x2.astype(BF16)
        else:
            x_scr[0][rows, :] = x2
    if first_tiles is not None:
        @pl.when(pl.program_id(0) < first_tiles)
        def _():
            out0_ref[...] = x_scr[0][...]

        @pl.when(pl.program_id(0) >= first_tiles)
        def _():
            out1_ref[...] = x_scr[0][...]


def _combine(ys, pos, gates, x1, p, layer, dn_alpha, first_tokens=None):
    t = x1.shape[0]
    if first_tokens is None:
        first_tiles = None
        out_specs = [pl.BlockSpec((TS, D_MODEL), lambda i: (i, 0)), pl.BlockSpec((TS, D_MODEL), lambda i: (i, 0))]
        out_shape = [jax.ShapeDtypeStruct((t, D_MODEL), F32), jax.ShapeDtypeStruct((t, D_MODEL), BF16)]
        scratch = []
    else:
        first_tiles = first_tokens // TS
        out_specs = [pl.BlockSpec((TS, D_MODEL), lambda i: (jnp.minimum(i, first_tiles - 1), 0)),
                     pl.BlockSpec((TS, D_MODEL), lambda i: (jnp.maximum(i - first_tiles, 0), 0))]
        out_shape = [jax.ShapeDtypeStruct((first_tokens, D_MODEL), F32),
                     jax.ShapeDtypeStruct((t - first_tokens, D_MODEL), F32)]
        scratch = [pltpu.VMEM((TS, D_MODEL), F32)]
    return pl.pallas_call(
        functools.partial(_combine_kernel, dn_alpha=np.float32(dn_alpha), first_tiles=first_tiles),
        grid=(t // TS,),
        in_specs=[pl.BlockSpec((REGION, D_MODEL), lambda i: (i, 0)),
                  pl.BlockSpec((TS, LANES), lambda i: (i, 0)),
                  pl.BlockSpec((TS, LANES), lambda i: (i, 0)),
                  pl.BlockSpec((TS, D_MODEL), lambda i: (i, 0)),
                  pl.BlockSpec((None, 1, D_MODEL), lambda i: (layer, 0, 0)),
                  pl.BlockSpec((None, 1, D_MODEL), lambda i: (layer, 0, 0))],
        out_specs=out_specs,
        out_shape=out_shape,
        scratch_shapes=[pltpu.VMEM((TS, REGION), BF16)] + scratch,
        compiler_params=pltpu.CompilerParams(
            dimension_semantics=("arbitrary",), vmem_limit_bytes=VMEM_LIMIT),
        name="combine",
    )(ys, pos, gates, x1, p["ln2_g"], p["ln2_b"])


def _prepare_params(w_in, b_in, ln_v_g, ln_v_b, w_s, b_s, w_pa, w_pb, w_o, ln1_g, ln1_b,
                    w_r, b_r, w_gu, b_gu, w_down, b_down, ln2_g, ln2_b):
    depth = w_in.shape[0]
    n_qkv = 3 * len(B_PATTERNS) * B_GROUP_WIDTH
    qkv0, qkv1 = 2 * D_MODEL, 2 * D_MODEL + n_qkv

    def permute_cols(a):
        gap = jnp.zeros(a.shape[:-1] + (MAIN_COL0 - n_qkv,), a.dtype)
        return jnp.concatenate([a[..., qkv0:qkv1], gap, a[..., :qkv0], a[..., qkv1:]], axis=-1)

    def vec(a):
        return a.reshape(depth, 1, a.shape[-1]).astype(F32)

    w_r_pad = jnp.pad(w_r, ((0, 0), (0, 0), (0, LANES - N_EXPERTS))).astype(BF16)
    b_r_pad = jnp.pad(b_r.astype(F32), ((0, 0), (0, LANES - N_EXPERTS)), constant_values=NEG_INF)
    return dict(
        w_in=permute_cols(w_in).astype(BF16), b_in=vec(permute_cols(b_in)),
        ln_v_g=vec(ln_v_g), ln_v_b=vec(ln_v_b),
        w_s=w_s.astype(BF16),
        b_s=jnp.broadcast_to(b_s.astype(F32)[..., None], b_s.shape + (CHUNK,)),
        w_pa=w_pa.astype(BF16), w_pb=w_pb.astype(BF16), w_o=w_o.astype(BF16),
        ln1_g=vec(ln1_g), ln1_b=vec(ln1_b),
        w_r=w_r_pad, b_r=vec(b_r_pad),
        w_gu=w_gu, b_gu=b_gu.astype(F32)[:, :, None, :],
        w_down=w_down, b_down=b_down.astype(F32)[:, :, None, :],
        ln2_g=vec(ln2_g), ln2_b=vec(ln2_b),
    )


def _bf16_rows_kernel(a_ref, b_ref, o_ref, *, first_tiles):
    o_ref[...] = jnp.where(pl.program_id(0) < first_tiles, a_ref[...], b_ref[...]).astype(BF16)


def _bf16_rows(a, b, tm=1024):
    first_tiles = a.shape[0] // tm
    t = a.shape[0] + b.shape[0]
    return pl.pallas_call(
        functools.partial(_bf16_rows_kernel, first_tiles=first_tiles),
        grid=(t // tm,),
        in_specs=[pl.BlockSpec((tm, D_MODEL), lambda i: (jnp.minimum(i, first_tiles - 1), 0)),
                  pl.BlockSpec((tm, D_MODEL), lambda i: (jnp.maximum(i - first_tiles, 0), 0))],
        out_specs=pl.BlockSpec((tm, D_MODEL), lambda i: (i, 0)),
        out_shape=jax.ShapeDtypeStruct((t, D_MODEL), BF16),
        compiler_params=pltpu.CompilerParams(
            dimension_semantics=("parallel",), vmem_limit_bytes=VMEM_LIMIT),
        name="bf16_rows",
    )(a, b)


def _trunk(x_first, x_rest, params, bm=512):
    n_first, seq, _ = x_first.shape
    n_seq = n_first + x_rest.shape[0]
    depth = params["w_in"].shape[0]
    dn_alpha = (2.0 * depth) ** 0.25
    x = (x_first.reshape(-1, D_MODEL), x_rest.reshape(-1, D_MODEL))
    xb = _bf16_rows(*x)
    for layer in range(depth):
        z = _inproj_main(xb, params["w_in"], params["b_in"], layer)
        attn = [_attention(_inproj_qkv(xb, params["w_in"], params["b_in"], layer, g, n_seq, seq), g)
                for g in range(len(B_PATTERNS))]
        x1, x1b, logits = _mixer(z, attn, x, params, layer, dn_alpha, seq)
        xs, pos, gates, cnt = _dispatch(logits, x1b)
        src0, plan, blk_e, n_used = _chunk_plan(cnt[:, 0, :N_EXPERTS], bm)
        ys = _ffn(xs, src0, plan, blk_e, n_used, params, layer, bm)
        last = layer == depth - 1
        x, xb = _combine(ys, pos, gates, x1, params, layer, dn_alpha, n_first * seq if last else None)
    return x.reshape(n_first, seq, D_MODEL), xb.reshape(n_seq - n_first, seq, D_MODEL)


def kernel(x_prompt, x_sample, w_in, b_in, ln_v_g, ln_v_b, w_s, b_s, w_pa, w_pb, w_o, ln1_g, ln1_b,
           w_r, b_r, w_gu, b_gu, w_down, b_down, ln2_g, ln2_b):
    params = _prepare_params(w_in, b_in, ln_v_g, ln_v_b, w_s, b_s, w_pa, w_pb, w_o, ln1_g, ln1_b,
                             w_r, b_r, w_gu, b_gu, w_down, b_down, ln2_g, ln2_b)
    return _trunk(x_prompt, x_sample, params)
```
